```python
import math
import jax, jax.numpy as jnp
from jax import lax
import numpy as np

D_MODEL = 1024
BATCH = 4
SEQ = 4096
DEPTH = 2

CONV_WIDTH = 3
A_WIDTH = D_MODEL // 2
A_HEADS = 8
B_WIDTH = D_MODEL // 2
POOL_WINDOWS = (2, 4, 8, 16)
N_POOL_GROUPS = len(POOL_WINDOWS)
POOL_GROUP = B_WIDTH // N_POOL_GROUPS
EVEN_IN = 3 * A_WIDTH + B_WIDTH
HEAD_DIM = 64
N_HEADS = D_MODEL // HEAD_DIM
DILATED_PAIRS = ((128, 1), (512, 4), (2048, 16))
N_REL_BUCKETS = 32
REL_MAX_DISTANCE = 2048
D_FF = 2816
EPS = 1e-6
MASK_VALUE = -1e30
N_EVEN = (DEPTH + 1) // 2
N_ODD = DEPTH // 2

kernel_name = "hybrid_conv_pool_dilated_attn_trunk"


def rmsnorm(x, g):
    x32 = x.astype(jnp.float32)
    y = x32 * lax.rsqrt(jnp.mean(x32 * x32, axis=-1, keepdims=True) + EPS)
    return (y * g.astype(jnp.float32)).astype(x.dtype)


def causal_dwconv3(x, w):
    S = x.shape[1]
    xp = jnp.pad(x, ((0, 0), (CONV_WIDTH - 1, 0), (0, 0)))
    y = xp[:, 0:S] * w[0]
    for i in range(1, CONV_WIDTH):
        y = y + xp[:, i:i + S] * w[i]
    return y


def causal_window_mean(p, k):
    S = p.shape[1]
    cs = jnp.cumsum(p, axis=1)
    cs_prev = jnp.pad(cs, ((0, 0), (k, 0), (0, 0)))[:, :S]
    cnt = jnp.minimum(jnp.arange(1, S + 1), k).astype(jnp.float32)[None, :, None]
    return (cs - cs_prev) / cnt


def t5_causal_bucket(dist):
    max_exact = N_REL_BUCKETS // 2
    d = jnp.maximum(dist, 1).astype(jnp.float32)
    large = max_exact + (jnp.log(d / max_exact) / math.log(REL_MAX_DISTANCE / max_exact)
                         * (N_REL_BUCKETS - max_exact)).astype(jnp.int32)
    large = jnp.minimum(large, N_REL_BUCKETS - 1)
    return jnp.where(dist < max_exact, dist, large)


def conv_pool_mixer(xn, w_in, conv_w, pool_w, pool_scale, w_out):
    B, S, _ = xn.shape
    proj = xn @ w_in
    h, gate_b, gate_c, pin = jnp.split(proj, [A_WIDTH, 2 * A_WIDTH, 3 * A_WIDTH], axis=-1)
    ya = gate_b * causal_dwconv3(gate_c * h, conv_w)
    p32 = pin.astype(jnp.float32).reshape(B, S, N_POOL_GROUPS, POOL_GROUP)
    pooled = jnp.stack([causal_window_mean(p32[:, :, g], k)
                        for g, k in enumerate(POOL_WINDOWS)], axis=2) - p32
    yb = jnp.einsum('bsgc,gcd->bsgd', pooled, pool_w.astype(jnp.float32))
    yb = (yb.reshape(B, S, B_WIDTH) * pool_scale.astype(jnp.float32)).astype(xn.dtype)
    return jnp.concatenate([ya, yb], axis=-1) @ w_out


def dilated_branch(q, k, v, rel_table, window, dil):
    B, S, H, E = q.shape
    n = window // dil
    QB = n
    L = S // dil
    nb = -(-L // QB)
    Lp = nb * QB

    def residues(t):
        return jnp.swapaxes(t.reshape(B, L, dil, H, E), 1, 2)

    qr = jnp.pad(residues(q), ((0, 0), (0, 0), (0, Lp - L), (0, 0), (0, 0)))
    kp = jnp.pad(residues(k), ((0, 0), (0, 0), (n, Lp - L), (0, 0), (0, 0)))
    vp = jnp.pad(residues(v), ((0, 0), (0, 0), (n, Lp - L), (0, 0), (0, 0)))

    def key_blocks(t):
        prev = t[:, :, :Lp].reshape(B, dil, nb, QB, H, E)
        cur = t[:, :, n:].reshape(B, dil, nb, QB, H, E)
        return jnp.concatenate([prev, cur], axis=3)

    qb = qr.reshape(B, dil, nb, QB, H, E)
    kb = key_blocks(kp)
    vb = key_blocks(vp)

    a = jnp.arange(QB)[:, None]
    c = jnp.arange(2 * QB)[None, :]
    rel = a + n - c
    key_pos = jnp.arange(nb)[:, None] * QB - n + jnp.arange(2 * QB)[None, :]
    valid = ((rel >= 0) & (rel <= n))[None] & (key_pos >= 0)[:, None, :]
    bias = rel_table.astype(jnp.float32)[t5_causal_bucket(jnp.clip(rel, 0, n) * dil)]
    bias = jnp.transpose(bias, (2, 0, 1))

    s = jnp.einsum('bdnqhe,bdnkhe->bdnhqk', qb, kb) * (HEAD_DIM ** -0.5) + bias
    s = jnp.where(valid[None, None, :, None], s, MASK_VALUE)
    m = jnp.max(s, axis=-1)
    p = jnp.exp(s - m[..., None])
    den = jnp.sum(p, axis=-1)
    m = jnp.swapaxes(m, 3, 4)
    den = jnp.swapaxes(den, 3, 4)
    o = jnp.einsum('bdnhqk,bdnkhe->bdnqhe', p, vb) / den[..., None]

    def from_blocks(t):
        rest = t.shape[5:]
        t = t.reshape((B, dil, Lp, H) + rest)[:, :, :L]
        return jnp.swapaxes(t, 1, 2).reshape((B, S, H) + rest)

    return from_blocks(o), from_blocks(m), from_blocks(den)


def dilated_attention(xn, w_qkv, g_q, g_k, w_o, rel_table):
    B, S, D = xn.shape
    qkv = (xn @ w_qkv).reshape(B, S, 3, N_HEADS, HEAD_DIM).astype(jnp.float32)
    q = rmsnorm(qkv[:, :, 0], g_q)
    k = rmsnorm(qkv[:, :, 1], g_k)
    v = qkv[:, :, 2]
    outs, maxes, dens = [], [], []
    for window, dil in DILATED_PAIRS:
        o, m, den = dilated_branch(q, k, v, rel_table, window, dil)
        outs.append(o)
        maxes.append(m)
        dens.append(den)
    outs = jnp.stack(outs)
    maxes = jnp.stack(maxes)
    dens = jnp.stack(dens)
    wts = dens * jnp.exp(maxes - jnp.max(maxes, axis=0, keepdims=True))
    o = jnp.sum(wts[..., None] * outs, axis=0) / jnp.sum(wts, axis=0)[..., None]
    return o.reshape(B, S, D).astype(xn.dtype) @ w_o


def conv_glu_ffn(xn, w_up, conv_w, conv_b, w_down):
    u = causal_dwconv3(xn @ w_up, conv_w) + conv_b
    gate, up = jnp.split(u, 2, axis=-1)
    return (jax.nn.silu(gate) * up) @ w_down


def setup_inputs(seed: int = 0) -> dict:
    key = jax.random.key(seed)
    ks = jax.random.split(key, 20)
    f32 = jnp.float32

    def nrm(k, shape, scale):
        return jax.random.normal(k, shape, f32) * scale

    return {
        "x": nrm(ks[0], (BATCH, SEQ, D_MODEL), 1.0),
        "rel_bias": nrm(ks[1], (N_REL_BUCKETS, N_HEADS), 0.5),
        "even_norm": 1.0 + nrm(ks[2], (N_EVEN, D_MODEL), 0.02),
        "even_w_in": nrm(ks[3], (N_EVEN, D_MODEL, EVEN_IN), D_MODEL ** -0.5),
        "even_conv_w": nrm(ks[4], (N_EVEN, CONV_WIDTH, A_WIDTH), CONV_WIDTH ** -0.5),
        "even_pool_w": nrm(ks[5], (N_EVEN, N_POOL_GROUPS, POOL_GROUP, POOL_GROUP), POOL_GROUP ** -0.5),
        "even_pool_scale": 1.0 + nrm(ks[6], (N_EVEN, B_WIDTH), 0.02),
        "even_w_out": nrm(ks[7], (N_EVEN, D_MODEL, D_MODEL), D_MODEL ** -0.5),
        "odd_norm": 1.0 + nrm(ks[8], (N_ODD, D_MODEL), 0.02),
        "odd_w_qkv": nrm(ks[9], (N_ODD, D_MODEL, 3 * D_MODEL), D_MODEL ** -0.5),
        "odd_q_norm": 1.0 + nrm(ks[10], (N_ODD, HEAD_DIM), 0.02),
        "odd_k_norm": 1.0 + nrm(ks[11], (N_ODD, HEAD_DIM), 0.02),
        "odd_w_o": nrm(ks[12], (N_ODD, D_MODEL, D_MODEL), D_MODEL ** -0.5),
        "ffn_norm": 1.0 + nrm(ks[13], (DEPTH, D_MODEL), 0.02),
        "ffn_w_up": nrm(ks[14], (DEPTH, D_MODEL, 2 * D_FF), D_MODEL ** -0.5),
        "ffn_conv_w": nrm(ks[15], (DEPTH, CONV_WIDTH, 2 * D_FF), CONV_WIDTH ** -0.5),
        "ffn_conv_b": nrm(ks[16], (DEPTH, 2 * D_FF), 0.02),
        "ffn_w_down": nrm(ks[17], (DEPTH, D_FF, D_MODEL), D_FF ** -0.5),
    }


def reference(x, rel_bias, even_norm, even_w_in, even_conv_w, even_pool_w, even_pool_scale,
              even_w_out, odd_norm, odd_w_qkv, odd_q_norm, odd_k_norm, odd_w_o,
              ffn_norm, ffn_w_up, ffn_conv_w, ffn_conv_b, ffn_w_down):
    for layer in range(DEPTH):
        i = layer // 2
        if layer % 2 == 0:
            x = x + conv_pool_mixer(rmsnorm(x, even_norm[i]), even_w_in[i], even_conv_w[i],
                                    even_pool_w[i], even_pool_scale[i], even_w_out[i])
        else:
            x = x + dilated_attention(rmsnorm(x, odd_norm[i]), odd_w_qkv[i], odd_q_norm[i],
                                      odd_k_norm[i], odd_w_o[i], rel_bias)
        x = x + conv_glu_ffn(rmsnorm(x, ffn_norm[layer]), ffn_w_up[layer], ffn_conv_w[layer],
                             ffn_conv_b[layer], ffn_w_down[layer])
    return x
```

```python
import functools
import math

import jax
import jax.numpy as jnp
from jax import lax
from jax.experimental import pallas as pl
from jax.experimental.pallas import tpu as pltpu

D_MODEL = 1024
CONV_WIDTH = 3
A_WIDTH = 512
B_WIDTH = 512
POOL_WINDOWS = (2, 4, 8, 16)
POOL_GROUP = 128
EVEN_IN = 3 * A_WIDTH + B_WIDTH
HEAD_DIM = 64
N_HEADS = 16
DILATED_PAIRS = ((128, 1), (512, 4), (2048, 16))
N_REL_BUCKETS = 32
REL_MAX_DISTANCE = 2048
D_FF = 2816
EPS = 1e-6
MASK_VALUE = -1e30

BF16 = jnp.bfloat16
F32 = jnp.float32

LANES = 128
ATT_BLOCK = 128
ROW_TILE = 512
POOL_HALO = 16
CONV_HALO = 8
FF_CHUNK = 256
VMEM_LIMIT = 56 * 1024 * 1024


def _const_spec(shape):
    nd = len(shape)
    return pl.BlockSpec(shape, lambda *_: (0,) * nd, pipeline_mode=pl.Buffered(1))


def _params(n_axes):
    return pltpu.CompilerParams(
        dimension_semantics=("arbitrary",) * n_axes, vmem_limit_bytes=VMEM_LIMIT)


def _rmsnorm(x, g):
    ms = jnp.mean(x * x, axis=-1, keepdims=True)
    return x * lax.rsqrt(ms + EPS) * g


def _mixer_kernel(x_ref, g_ref, win_ref, cw_ref, pw_ref, ps_ref, wout_ref, o_ref,
                  ch_buf, p_buf):
    ts = x_ref.shape[1]
    s = pl.program_id(1)

    @pl.when(s == 0)
    def _():
        ch_buf[0:POOL_HALO, :] = jnp.zeros((POOL_HALO, A_WIDTH), F32)
        p_buf[0:POOL_HALO, :] = jnp.zeros((POOL_HALO, B_WIDTH), F32)

    x = x_ref[0]
    xn = _rmsnorm(x, g_ref[...]).astype(BF16)
    proj = jnp.dot(xn, win_ref[...], preferred_element_type=F32)
    h = proj[:, 0:A_WIDTH]
    gate_b = proj[:, A_WIDTH:2 * A_WIDTH]
    gate_c = proj[:, 2 * A_WIDTH:3 * A_WIDTH]
    pin = proj[:, 3 * A_WIDTH:]

    ch = gate_c * h
    ch_buf[POOL_HALO:POOL_HALO + ts, :] = ch
    p_buf[POOL_HALO:POOL_HALO + ts, :] = pin

    cw = cw_ref[...]
    conv = (cw[0:1] * ch_buf[POOL_HALO - 2:POOL_HALO - 2 + ts, :]
            + cw[1:2] * ch_buf[POOL_HALO - 1:POOL_HALO - 1 + ts, :]
            + cw[2:3] * ch)
    ya = gate_b * conv

    pos = s * ts + lax.broadcasted_iota(jnp.int32, (ts, 1), 0)
    parts = [ya]
    for g, k in enumerate(POOL_WINDOWS):
        cols = slice(g * POOL_GROUP, (g + 1) * POOL_GROUP)
        cur = pin[:, cols]
        acc = cur
        for j in range(1, k):
            acc = acc + p_buf[POOL_HALO - j:POOL_HALO - j + ts, cols]
        cnt = jnp.minimum(pos + 1, k).astype(F32)
        pooled = acc / cnt - cur
        yb = jnp.dot(pooled.astype(BF16), pw_ref[g], preferred_element_type=F32)
        parts.append(yb * ps_ref[:, cols])
    y = jnp.concatenate(parts, axis=-1).astype(BF16)
    o_ref[0] = x + jnp.dot(y, wout_ref[...], preferred_element_type=F32)

    ch_buf[0:POOL_HALO, :] = ch_buf[ts:ts + POOL_HALO, :]
    p_buf[0:POOL_HALO, :] = p_buf[ts:ts + POOL_HALO, :]


def _mixer(x, g, w_in, conv_w, pool_w, pool_scale, w_out):
    b, s, d = x.shape
    ts = ROW_TILE
    row_spec = pl.BlockSpec((1, ts, d), lambda i, j: (i, j, 0))
    return pl.pallas_call(
        _mixer_kernel,
        grid=(b, s // ts),
        in_specs=[row_spec, _const_spec((1, d)), _const_spec((d, EVEN_IN)),
                  _const_spec((CONV_WIDTH, A_WIDTH)),
                  _const_spec((len(POOL_WINDOWS), POOL_GROUP, POOL_GROUP)),
                  _const_spec((1, B_WIDTH)), _const_spec((d, d))],
        out_specs=row_spec,
        out_shape=jax.ShapeDtypeStruct(x.shape, F32),
        scratch_shapes=[pltpu.VMEM((POOL_HALO + ts, A_WIDTH), F32),
                        pltpu.VMEM((POOL_HALO + ts, B_WIDTH), F32)],
        compiler_params=_params(2),
        name="mixer",
    )(x, g, w_in, conv_w, pool_w, pool_scale, w_out)


def _ffn_kernel(x_ref, g_ref, wup_ref, cw_ref, cb_ref, wdn_ref, o_ref, u_buf, carry):
    ts = x_ref.shape[1]
    s = pl.program_id(1)

    @pl.when(s == 0)
    def _():
        carry[...] = jnp.zeros(carry.shape, F32)

    x = x_ref[0]
    xn = _rmsnorm(x, g_ref[...]).astype(BF16)
    acc = x
    for j in range(D_FF // FF_CHUNK):
        halves = []
        for half in range(2):
            c0 = half * D_FF + j * FF_CHUNK
            cols = slice(c0, c0 + FF_CHUNK)
            slot = 2 * (j % 2) + half
            u = jnp.dot(xn, wup_ref[:, cols], preferred_element_type=F32)
            u_buf[slot, 0:CONV_HALO, :] = carry[:, cols]
            u_buf[slot, CONV_HALO:CONV_HALO + ts, :] = u
            carry[:, cols] = u[ts - CONV_HALO:ts, :]
            w = cw_ref[:, cols]
            halves.append(w[0:1] * u_buf[slot, CONV_HALO - 2:CONV_HALO - 2 + ts, :]
                          + w[1:2] * u_buf[slot, CONV_HALO - 1:CONV_HALO - 1 + ts, :]
                          + w[2:3] * u + cb_ref[:, cols])
        gate, up = halves
        hj = (gate * jax.nn.sigmoid(gate) * up).astype(BF16)
        acc = acc + jnp.dot(hj, wdn_ref[j * FF_CHUNK:(j + 1) * FF_CHUNK, :],
                            preferred_element_type=F32)
    o_ref[0] = acc


def _ffn(x, g, w_up, conv_w, conv_b, w_down):
    b, s, d = x.shape
    ts = ROW_TILE
    row_spec = pl.BlockSpec((1, ts, d), lambda i, j: (i, j, 0))
    return pl.pallas_call(
        _ffn_kernel,
        grid=(b, s // ts),
        in_specs=[row_spec, _const_spec((1, d)), _const_spec((d, 2 * D_FF)),
                  _const_spec((CONV_WIDTH, 2 * D_FF)), _const_spec((1, 2 * D_FF)),
                  _const_spec((D_FF, d))],
        out_specs=row_spec,
        out_shape=jax.ShapeDtypeStruct(x.shape, F32),
        scratch_shapes=[pltpu.VMEM((4, CONV_HALO + ts, FF_CHUNK), F32),
                        pltpu.VMEM((CONV_HALO, 2 * D_FF), F32)],
        compiler_params=_params(2),
        name="ffn",
    )(x, g, w_up, conv_w, conv_b, w_down)


def _head_pair_norm(t, g_pair, scale):
    lane = lax.broadcasted_iota(jnp.int32, t.shape, 1)
    lo = lane < HEAD_DIM
    sq = t * t
    s_lo = jnp.sum(jnp.where(lo, sq, 0.0), axis=-1, keepdims=True)
    s_hi = jnp.sum(jnp.where(lo, 0.0, sq), axis=-1, keepdims=True)
    r = lax.rsqrt(jnp.where(lo, s_lo, s_hi) * (1.0 / HEAD_DIM) + EPS)
    return t * r * (g_pair * scale)


def _qkv_kernel(x_ref, g_ref, w_ref, gq_ref, gk_ref, q_ref, k_ref, v_ref):
    d = D_MODEL
    x = x_ref[0]
    xn = _rmsnorm(x, g_ref[...]).astype(BF16)
    qkv = jnp.dot(xn, w_ref[...], preferred_element_type=F32)
    for hp in range(d // LANES):
        cols = slice(hp * LANES, (hp + 1) * LANES)
        q_ref[0, :, cols] = _head_pair_norm(
            qkv[:, hp * LANES:(hp + 1) * LANES], gq_ref[...], HEAD_DIM ** -0.5).astype(BF16)
        k_ref[0, :, cols] = _head_pair_norm(
            qkv[:, d + hp * LANES:d + (hp + 1) * LANES], gk_ref[...], 1.0).astype(BF16)
    v_ref[0] = qkv[:, 2 * d:].astype(BF16)


def _qkv(x, g, w_qkv, gq_pair, gk_pair):
    b, s, d = x.shape
    ts = ROW_TILE
    row_spec = pl.BlockSpec((1, ts, d), lambda i, j: (i, j, 0))
    out = jax.ShapeDtypeStruct((b, s, d), BF16)
    return pl.pallas_call(
        _qkv_kernel,
        grid=(b, s // ts),
        in_specs=[row_spec, _const_spec((1, d)), _const_spec((d, 3 * d)),
                  _const_spec((1, LANES)), _const_spec((1, LANES))],
        out_specs=[row_spec, row_spec, row_spec],
        out_shape=[out, out, out],
        compiler_params=_params(2),
        name="qkv",
    )(x, g, w_qkv, gq_pair, gk_pair)


def _attn_kernel(q_ref, k_ref, v_ref, kh_ref, vh_ref, bias_ref, o_ref, m_ref, l_ref,
                 k_buf, v_buf, *, units):
    nb = ATT_BLOCK
    first_step = pl.program_id(2) == 0
    k_buf[0:nb, :] = kh_ref[0]
    v_buf[0:nb, :] = vh_ref[0]
    k_buf[nb:, :] = k_ref[0]
    v_buf[nb:, :] = v_ref[0]

    lane = lax.broadcasted_iota(jnp.int32, (nb, LANES), 1)
    lo = lane < HEAD_DIM

    def unit(u, carry):
        r0 = pl.multiple_of(u * nb, nb)
        table = jnp.where(jnp.logical_and(first_step, u == 0), 1, 0)
        m_tile = jnp.zeros((nb, LANES), F32)
        l_tile = jnp.ones((nb, LANES), F32)
        for hp in range(N_HEADS // 2):
            cols = slice(hp * LANES, (hp + 1) * LANES)
            q_pair = q_ref[0, pl.ds(r0, nb), cols]
            k_win = k_buf[pl.ds(r0, 2 * nb), cols]
            v_win = v_buf[pl.ds(r0, 2 * nb), cols]
            outs = []
            for sub in range(2):
                h = 2 * hp + sub
                q_h = jnp.where(lo if sub == 0 else jnp.logical_not(lo), q_pair,
                                jnp.zeros_like(q_pair))
                sc = lax.dot_general(q_h, k_win, (((1,), (1,)), ((), ())),
                                     preferred_element_type=F32)
                sc = sc + bias_ref[table, h]
                m = jnp.max(sc, axis=-1, keepdims=True)
                p = jnp.exp(sc - m)
                den = jnp.sum(p, axis=-1, keepdims=True)
                pv = jnp.dot(p.astype(BF16), v_win, preferred_element_type=F32)
                outs.append(pv / den)
                m_tile = jnp.where(lane == h, m, m_tile)
                l_tile = jnp.where(lane == h, den, l_tile)
            o_ref[0, pl.ds(r0, nb), cols] = jnp.where(lo, outs[0], outs[1]).astype(BF16)
        m_ref[0, pl.ds(r0, nb), :] = m_tile
        l_ref[0, pl.ds(r0, nb), :] = l_tile
        return carry

    lax.fori_loop(0, units, unit, 0)


def _attn_branch(q, k, v, bias, dil):
    b, s, d = q.shape
    nb = ATT_BLOCK
    length = s // dil
    units = min(8, length // nb)
    rows = units * nb
    view = lambda t: t.reshape(b, length, dil * t.shape[-1])
    main = pl.BlockSpec((1, rows, d), lambda bi, r, i: (bi, i, r))
    halo = pl.BlockSpec((1, nb, d), lambda bi, r, i: (bi, jnp.maximum(i * units - 1, 0), r))
    stat = pl.BlockSpec((1, rows, LANES), lambda bi, r, i: (bi, i, r))
    o, m, l = pl.pallas_call(
        functools.partial(_attn_kernel, units=units),
        grid=(b, dil, length // rows),
        in_specs=[main, main, main, halo, halo, _const_spec(bias.shape)],
        out_specs=[main, stat, stat],
        out_shape=[jax.ShapeDtypeStruct((b, length, dil * d), BF16),
                   jax.ShapeDtypeStruct((b, length, dil * LANES), F32),
                   jax.ShapeDtypeStruct((b, length, dil * LANES), F32)],
        scratch_shapes=[pltpu.VMEM((nb + rows, d), BF16), pltpu.VMEM((nb + rows, d), BF16)],
        compiler_params=_params(3),
        name=f"attn_d{dil}",
    )(view(q), view(k), view(v), view(k), view(v), bias)
    return o.reshape(b, s, d), m.reshape(b, s, LANES), l.reshape(b, s, LANES)


def _merge_kernel(x_ref, o1_ref, o2_ref, o3_ref, m1_ref, m2_ref, m3_ref,
                  l1_ref, l2_ref, l3_ref, e_ref, wo_ref, out_ref):
    ms = [m1_ref[0], m2_ref[0], m3_ref[0]]
    ls = [l1_ref[0], l2_ref[0], l3_ref[0]]
    os_ = [o1_ref, o2_ref, o3_ref]
    m_max = jnp.maximum(jnp.maximum(ms[0], ms[1]), ms[2])
    ws = [l * jnp.exp(m - m_max) for m, l in zip(ms, ls)]
    total = ws[0] + ws[1] + ws[2]
    merged = None
    for w, o_ref in zip(ws, os_):
        c = w / total
        c_hi = c.astype(BF16)
        c_lo = (c - c_hi.astype(F32)).astype(BF16)
        c_wide = jnp.dot(jnp.concatenate([c_hi, c_lo], axis=-1), e_ref[...],
                         preferred_element_type=F32)
        term = c_wide * o_ref[0].astype(F32)
        merged = term if merged is None else merged + term
    out_ref[0] = x_ref[0] + jnp.dot(merged.astype(BF16), wo_ref[...],
                                    preferred_element_type=F32)


def _merge_out_proj(x, outs, maxes, dens, expand, w_o):
    b, s, d = x.shape
    ts = ROW_TILE
    row_spec = pl.BlockSpec((1, ts, d), lambda i, j: (i, j, 0))
    stat_spec = pl.BlockSpec((1, ts, LANES), lambda i, j: (i, j, 0))
    return pl.pallas_call(
        _merge_kernel,
        grid=(b, s // ts),
        in_specs=[row_spec] + [row_spec] * 3 + [stat_spec] * 6
                 + [_const_spec(expand.shape), _const_spec((d, d))],
        out_specs=row_spec,
        out_shape=jax.ShapeDtypeStruct(x.shape, F32),
        compiler_params=_params(2),
        name="merge_out_proj",
    )(x, *outs, *maxes, *dens, expand, w_o)


def _t5_causal_bucket(dist):
    max_exact = N_REL_BUCKETS // 2
    dd = jnp.maximum(dist, 1).astype(F32)
    large = max_exact + (jnp.log(dd / max_exact) / math.log(REL_MAX_DISTANCE / max_exact)
                         * (N_REL_BUCKETS - max_exact)).astype(jnp.int32)
    large = jnp.minimum(large, N_REL_BUCKETS - 1)
    return jnp.where(dist < max_exact, dist, large)


def _bias_tables(rel_table, dil):
    n = ATT_BLOCK
    a = jnp.arange(n)[:, None]
    c = jnp.arange(2 * n)[None, :]
    rel = a + n - c
    valid = (rel >= 0) & (rel <= n)
    bias = rel_table.astype(F32)[_t5_causal_bucket(jnp.clip(rel, 0, n) * dil)]
    bias = jnp.transpose(bias, (2, 0, 1))
    regular = jnp.where(valid[None], bias, MASK_VALUE)
    start = jnp.where((valid & (c >= n))[None], bias, MASK_VALUE)
    return jnp.stack([regular, start])


def _head_expand_matrix():
    row = jnp.arange(2 * LANES)[:, None] % LANES
    col = jnp.arange(D_MODEL)[None, :] // HEAD_DIM
    return (row == col).astype(BF16)


def kernel(x, rel_bias, even_norm, even_w_in, even_conv_w, even_pool_w, even_pool_scale,
           even_w_out, odd_norm, odd_w_qkv, odd_q_norm, odd_k_norm, odd_w_o, ffn_norm,
           ffn_w_up, ffn_conv_w, ffn_conv_b, ffn_w_down):
    def ffn(t, layer):
        return _ffn(t, ffn_norm[layer][None], ffn_w_up[layer].astype(BF16), ffn_conv_w[layer],
                    ffn_conv_b[layer][None], ffn_w_down[layer].astype(BF16))

    x = _mixer(x, even_norm[0][None], even_w_in[0].astype(BF16), even_conv_w[0],
               even_pool_w[0].astype(BF16), even_pool_scale[0][None],
               even_w_out[0].astype(BF16))
    x = ffn(x, 0)

    gq_pair = jnp.tile(odd_q_norm[0], 2)[None]
    gk_pair = jnp.tile(odd_k_norm[0], 2)[None]
    q, k, v = _qkv(x, odd_norm[0][None], odd_w_qkv[0].astype(BF16), gq_pair, gk_pair)
    outs, maxes, dens = [], [], []
    for _, dil in DILATED_PAIRS:
        o, m, l = _attn_branch(q, k, v, _bias_tables(rel_bias, dil), dil)
        outs.append(o)
        maxes.append(m)
        dens.append(l)
    x = _merge_out_proj(x, outs, maxes, dens, _head_expand_matrix(), odd_w_o[0].astype(BF16))
    x = ffn(x, 1)
    return x
```

```python
import functools
import math

import jax
import jax.numpy as jnp
from jax import lax
from jax.experimental import pallas as pl
from jax.experimental.pallas import tpu as pltpu

D_MODEL = 1024
CONV_WIDTH = 3
A_WIDTH = 512
B_WIDTH = 512
POOL_WINDOWS = (2, 4, 8, 16)
POOL_GROUP = 128
EVEN_IN = 3 * A_WIDTH + B_WIDTH
HEAD_DIM = 64
N_HEADS = 16
DILATED_PAIRS = ((128, 1), (512, 4), (2048, 16))
DILATIONS = tuple(dil for _, dil in DILATED_PAIRS)
N_REL_BUCKETS = 32
REL_MAX_DISTANCE = 2048
D_FF = 2816
EPS = 1e-6
MASK_VALUE = -1e30

BF16 = jnp.bfloat16
F32 = jnp.float32

LANES = 128
ATT_BLOCK = 128
ROW_TILE = 512
POOL_HALO = 16
CONV_HALO = 8
FF_CHUNK = 256
VMEM_LIMIT = 56 * 1024 * 1024


def _const_spec(shape):
    nd = len(shape)
    return pl.BlockSpec(shape, lambda *_: (0,) * nd, pipeline_mode=pl.Buffered(1))


def _params(n_axes):
    return pltpu.CompilerParams(
        dimension_semantics=("arbitrary",) * n_axes, vmem_limit_bytes=VMEM_LIMIT)


def _rmsnorm(x, g):
    ms = jnp.mean(x * x, axis=-1, keepdims=True)
    return x * lax.rsqrt(ms + EPS) * g


def _mixer_kernel(x_ref, g_ref, win_ref, cw_ref, pw_ref, ps_ref, wout_ref, o_ref,
                  ch_buf, p_buf):
    ts = x_ref.shape[1]
    s = pl.program_id(1)

    @pl.when(s == 0)
    def _():
        ch_buf[0:POOL_HALO, :] = jnp.zeros((POOL_HALO, A_WIDTH), F32)
        p_buf[0:POOL_HALO, :] = jnp.zeros((POOL_HALO, B_WIDTH), F32)

    x = x_ref[0]
    xn = _rmsnorm(x, g_ref[...]).astype(BF16)
    proj = jnp.dot(xn, win_ref[...], preferred_element_type=F32)
    h = proj[:, 0:A_WIDTH]
    gate_b = proj[:, A_WIDTH:2 * A_WIDTH]
    gate_c = proj[:, 2 * A_WIDTH:3 * A_WIDTH]
    pin = proj[:, 3 * A_WIDTH:]

    ch = gate_c * h
    ch_buf[POOL_HALO:POOL_HALO + ts, :] = ch
    p_buf[POOL_HALO:POOL_HALO + ts, :] = pin

    cw = cw_ref[...]
    conv = (cw[0:1] * ch_buf[POOL_HALO - 2:POOL_HALO - 2 + ts, :]
            + cw[1:2] * ch_buf[POOL_HALO - 1:POOL_HALO - 1 + ts, :]
            + cw[2:3] * ch)
    ya = gate_b * conv

    pos = s * ts + lax.broadcasted_iota(jnp.int32, (ts, 1), 0)
    parts = [ya]
    for g, k in enumerate(POOL_WINDOWS):
        cols = slice(g * POOL_GROUP, (g + 1) * POOL_GROUP)
        cur = pin[:, cols]
        acc = cur
        for j in range(1, k):
            acc = acc + p_buf[POOL_HALO - j:POOL_HALO - j + ts, cols]
        cnt = jnp.minimum(pos + 1, k).astype(F32)
        pooled = acc / cnt - cur
        yb = jnp.dot(pooled.astype(BF16), pw_ref[g], preferred_element_type=F32)
        parts.append(yb * ps_ref[:, cols])
    y = jnp.concatenate(parts, axis=-1).astype(BF16)
    o_ref[0] = x + jnp.dot(y, wout_ref[...], preferred_element_type=F32)

    ch_buf[0:POOL_HALO, :] = ch_buf[ts:ts + POOL_HALO, :]
    p_buf[0:POOL_HALO, :] = p_buf[ts:ts + POOL_HALO, :]


def _mixer(x, g, w_in, conv_w, pool_w, pool_scale, w_out):
    b, s, d = x.shape
    ts = ROW_TILE
    row_spec = pl.BlockSpec((1, ts, d), lambda i, j: (i, j, 0))
    return pl.pallas_call(
        _mixer_kernel,
        grid=(b, s // ts),
        in_specs=[row_spec, _const_spec((1, d)), _const_spec((d, EVEN_IN)),
                  _const_spec((CONV_WIDTH, A_WIDTH)),
                  _const_spec((len(POOL_WINDOWS), POOL_GROUP, POOL_GROUP)),
                  _const_spec((1, B_WIDTH)), _const_spec((d, d))],
        out_specs=row_spec,
        out_shape=jax.ShapeDtypeStruct(x.shape, F32),
        scratch_shapes=[pltpu.VMEM((POOL_HALO + ts, A_WIDTH), F32),
                        pltpu.VMEM((POOL_HALO + ts, B_WIDTH), F32)],
        compiler_params=_params(2),
        name="mixer",
    )(x, g, w_in, conv_w, pool_w, pool_scale, w_out)


def _ffn_kernel(x_ref, g_ref, wup_ref, cw_ref, cb_ref, wdn_ref, o_ref, u_buf, carry):
    ts = x_ref.shape[1]
    s = pl.program_id(1)

    @pl.when(s == 0)
    def _():
        carry[...] = jnp.zeros(carry.shape, F32)

    x = x_ref[0]
    xn = _rmsnorm(x, g_ref[...]).astype(BF16)
    acc = x
    for j in range(D_FF // FF_CHUNK):
        halves = []
        for half in range(2):
            c0 = half * D_FF + j * FF_CHUNK
            cols = slice(c0, c0 + FF_CHUNK)
            slot = 2 * (j % 2) + half
            u = jnp.dot(xn, wup_ref[:, cols], preferred_element_type=F32)
            u_buf[slot, 0:CONV_HALO, :] = carry[:, cols]
            u_buf[slot, CONV_HALO:CONV_HALO + ts, :] = u
            carry[:, cols] = u[ts - CONV_HALO:ts, :]
            w = cw_ref[:, cols]
            halves.append(w[0:1] * u_buf[slot, CONV_HALO - 2:CONV_HALO - 2 + ts, :]
                          + w[1:2] * u_buf[slot, CONV_HALO - 1:CONV_HALO - 1 + ts, :]
                          + w[2:3] * u + cb_ref[:, cols])
        gate, up = halves
        hj = (gate * jax.nn.sigmoid(gate) * up).astype(BF16)
        acc = acc + jnp.dot(hj, wdn_ref[j * FF_CHUNK:(j + 1) * FF_CHUNK, :],
                            preferred_element_type=F32)
    o_ref[0] = acc


def _ffn(x, g, w_up, conv_w, conv_b, w_down):
    b, s, d = x.shape
    ts = ROW_TILE
    row_spec = pl.BlockSpec((1, ts, d), lambda i, j: (i, j, 0))
    return pl.pallas_call(
        _ffn_kernel,
        grid=(b, s // ts),
        in_specs=[row_spec, _const_spec((1, d)), _const_spec((d, 2 * D_FF)),
                  _const_spec((CONV_WIDTH, 2 * D_FF)), _const_spec((1, 2 * D_FF)),
                  _const_spec((D_FF, d))],
        out_specs=row_spec,
        out_shape=jax.ShapeDtypeStruct(x.shape, F32),
        scratch_shapes=[pltpu.VMEM((4, CONV_HALO + ts, FF_CHUNK), F32),
                        pltpu.VMEM((CONV_HALO, 2 * D_FF), F32)],
        compiler_params=_params(2),
        name="ffn",
    )(x, g, w_up, conv_w, conv_b, w_down)


def _head_pair_norm(t, g_pair, scale):
    lane = lax.broadcasted_iota(jnp.int32, t.shape, 1)
    lo = lane < HEAD_DIM
    sq = t * t
    s_lo = jnp.sum(jnp.where(lo, sq, 0.0), axis=-1, keepdims=True)
    s_hi = jnp.sum(jnp.where(lo, 0.0, sq), axis=-1, keepdims=True)
    r = lax.rsqrt(jnp.where(lo, s_lo, s_hi) * (1.0 / HEAD_DIM) + EPS)
    return t * r * (g_pair * scale)


def _qkv_kernel(x_ref, g_ref, w_ref, gq_ref, gk_ref, *refs):
    n_dil = len(DILATIONS)
    out_refs = [refs[3 * di:3 * di + 3] for di in range(n_dil)]
    stage = refs[3 * n_dil]
    d = D_MODEL
    ts = x_ref.shape[1]
    x = x_ref[0]
    xn = _rmsnorm(x, g_ref[...]).astype(BF16)
    qkv = jnp.dot(xn, w_ref[...], preferred_element_type=F32)
    n_slabs = d // LANES
    for which in range(3):
        for hp in range(n_slabs):
            cols = slice(hp * LANES, (hp + 1) * LANES)
            slab = qkv[:, which * d + hp * LANES:which * d + (hp + 1) * LANES]
            if which == 0:
                slab = _head_pair_norm(slab, gq_ref[...], HEAD_DIM ** -0.5)
            elif which == 1:
                slab = _head_pair_norm(slab, gk_ref[...], 1.0)
            slot = which * n_slabs + hp
            stage[slot] = slab
            for di, dil in enumerate(DILATIONS):
                o_ref = out_refs[di][which]
                if dil == 1:
                    o_ref[0, :, cols] = slab.astype(BF16)
                else:
                    for r in range(dil):
                        o_ref[r, :, cols] = stage[slot, pl.ds(r, ts // dil, stride=dil), :].astype(BF16)


def _qkv(x, g, w_qkv, gq_pair, gk_pair):
    b, s, d = x.shape
    ts = ROW_TILE
    row_spec = pl.BlockSpec((1, ts, d), lambda i, j: (i, j, 0))
    out_specs, out_shapes = [], []
    for dil in DILATIONS:
        out_specs += [pl.BlockSpec((None, dil, ts // dil, d), lambda i, j: (i, 0, j, 0))] * 3
        out_shapes += [jax.ShapeDtypeStruct((b, dil, s // dil, d), BF16)] * 3
    outs = pl.pallas_call(
        _qkv_kernel,
        grid=(b, s // ts),
        in_specs=[row_spec, _const_spec((1, d)), _const_spec((d, 3 * d)),
                  _const_spec((1, LANES)), _const_spec((1, LANES))],
        out_specs=out_specs,
        out_shape=out_shapes,
        scratch_shapes=[pltpu.VMEM((3 * d // LANES, ts, LANES), F32)],
        compiler_params=_params(2),
        name="qkv",
    )(x, g, w_qkv, gq_pair, gk_pair)
    return [outs[3 * di:3 * di + 3] for di in range(len(DILATIONS))]


def _attn_kernel(q_ref, k_ref, v_ref, kh_ref, vh_ref, bias_ref, o_ref, m_ref, l_ref,
                 k_buf, v_buf, *, units):
    nb = ATT_BLOCK
    first_step = pl.program_id(2) == 0
    k_buf[0:nb, :] = kh_ref[...]
    v_buf[0:nb, :] = vh_ref[...]
    k_buf[nb:, :] = k_ref[...]
    v_buf[nb:, :] = v_ref[...]

    lane = lax.broadcasted_iota(jnp.int32, (nb, LANES), 1)
    lo = lane < HEAD_DIM

    def unit(u, carry):
        r0 = pl.multiple_of(u * nb, nb)
        table = jnp.where(jnp.logical_and(first_step, u == 0), 1, 0)
        m_tile = jnp.zeros((nb, LANES), F32)
        l_tile = jnp.ones((nb, LANES), F32)
        for hp in range(N_HEADS // 2):
            cols = slice(hp * LANES, (hp + 1) * LANES)
            q_pair = q_ref[pl.ds(r0, nb), cols]
            k_win = k_buf[pl.ds(r0, 2 * nb), cols]
            v_win = v_buf[pl.ds(r0, 2 * nb), cols]
            outs = []
            for sub in range(2):
                h = 2 * hp + sub
                q_h = jnp.where(lo if sub == 0 else jnp.logical_not(lo), q_pair,
                                jnp.zeros_like(q_pair))
                sc = lax.dot_general(q_h, k_win, (((1,), (1,)), ((), ())),
                                     preferred_element_type=F32)
                sc = sc + bias_ref[table, h]
                m = jnp.max(sc, axis=-1, keepdims=True)
                p = jnp.exp(sc - m)
                den = jnp.sum(p, axis=-1, keepdims=True)
                pv = jnp.dot(p.astype(BF16), v_win, preferred_element_type=F32)
                outs.append(pv / den)
                m_tile = jnp.where(lane == h, m, m_tile)
                l_tile = jnp.where(lane == h, den, l_tile)
            o_ref[pl.ds(r0, nb), cols] = jnp.where(lo, outs[0], outs[1]).astype(BF16)
        m_ref[pl.ds(r0, nb), :] = m_tile
        l_ref[pl.ds(r0, nb), :] = l_tile
        return carry

    lax.fori_loop(0, units, unit, 0)


def _attn_branch(q, k, v, bias):
    b, dil, length, d = q.shape
    nb = ATT_BLOCK
    units = min(8, length // nb)
    rows = units * nb
    main = pl.BlockSpec((None, None, rows, d), lambda bi, r, i: (bi, r, i, 0))
    halo = pl.BlockSpec((None, None, nb, d),
                        lambda bi, r, i: (bi, r, jnp.maximum(i * units - 1, 0), 0))
    stat = pl.BlockSpec((None, None, rows, LANES), lambda bi, r, i: (bi, r, i, 0))
    stat_shape = jax.ShapeDtypeStruct((b, dil, length, LANES), F32)
    return pl.pallas_call(
        functools.partial(_attn_kernel, units=units),
        grid=(b, dil, length // rows),
        in_specs=[main, main, main, halo, halo, _const_spec(bias.shape)],
        out_specs=[main, stat, stat],
        out_shape=[jax.ShapeDtypeStruct(q.shape, BF16), stat_shape, stat_shape],
        scratch_shapes=[pltpu.VMEM((nb + rows, d), BF16), pltpu.VMEM((nb + rows, d), BF16)],
        compiler_params=_params(3),
        name=f"attn_d{dil}",
    )(q, k, v, k, v, bias)


def _merge_kernel(x_ref, *refs):
    n_dil = len(DILATIONS)
    branch_refs = [refs[3 * di:3 * di + 3] for di in range(n_dil)]
    e_ref, wo_ref, out_ref, o_scr, st_scr = refs[3 * n_dil:]
    ts = x_ref.shape[1]
    n_slabs = D_MODEL // LANES

    def token_order(ref, dil, scr, slot, cols=slice(None)):
        if dil == 1:
            return ref[0, :, cols].astype(F32)
        for r in range(dil):
            scr[slot, pl.ds(r, ts // dil, stride=dil), :] = ref[r, :, cols].astype(F32)
        return scr[slot]

    ms, ls = [], []
    for di, dil in enumerate(DILATIONS):
        ms.append(token_order(branch_refs[di][1], dil, st_scr, 2 * di))
        ls.append(token_order(branch_refs[di][2], dil, st_scr, 2 * di + 1))
    m_max = functools.reduce(jnp.maximum, ms)
    ws = [l * jnp.exp(m - m_max) for m, l in zip(ms, ls)]
    total = functools.reduce(jnp.add, ws)
    c_wides = []
    for w in ws:
        c = w / total
        c_hi = c.astype(BF16)
        c_lo = (c - c_hi.astype(F32)).astype(BF16)
        c_wides.append(jnp.dot(jnp.concatenate([c_hi, c_lo], axis=-1), e_ref[...],
                               preferred_element_type=F32))
    slabs = []
    for hp in range(n_slabs):
        cols = slice(hp * LANES, (hp + 1) * LANES)
        merged = None
        for di, dil in enumerate(DILATIONS):
            o = token_order(branch_refs[di][0], dil, o_scr, di * n_slabs + hp, cols)
            term = c_wides[di][:, cols] * o
            merged = term if merged is None else merged + term
        slabs.append(merged.astype(BF16))
    merged = jnp.concatenate(slabs, axis=-1)
    out_ref[0] = x_ref[0] + jnp.dot(merged, wo_ref[...], preferred_element_type=F32)


def _merge_out_proj(x, branches, expand, w_o):
    b, s, d = x.shape
    ts = ROW_TILE
    row_spec = pl.BlockSpec((1, ts, d), lambda i, j: (i, j, 0))
    in_specs, args = [row_spec], [x]
    for dil, (o, m, l) in zip(DILATIONS, branches):
        in_specs.append(pl.BlockSpec((None, dil, ts // dil, d), lambda i, j: (i, 0, j, 0)))
        in_specs += [pl.BlockSpec((None, dil, ts // dil, LANES), lambda i, j: (i, 0, j, 0))] * 2
        args += [o, m, l]
    n_dil = len(DILATIONS)
    return pl.pallas_call(
        _merge_kernel,
        grid=(b, s // ts),
        in_specs=in_specs + [_const_spec(expand.shape), _const_spec((d, d))],
        out_specs=row_spec,
        out_shape=jax.ShapeDtypeStruct(x.shape, F32),
        scratch_shapes=[pltpu.VMEM((n_dil * d // LANES, ts, LANES), F32),
                        pltpu.VMEM((2 * n_dil, ts, LANES), F32)],
        compiler_params=_params(2),
        name="merge_out_proj",
    )(x, *args[1:], expand, w_o)


def _t5_causal_bucket(dist):
    max_exact = N_REL_BUCKETS // 2
    dd = jnp.maximum(dist, 1).astype(F32)
    large = max_exact + (jnp.log(dd / max_exact) / math.log(REL_MAX_DISTANCE / max_exact)
                         * (N_REL_BUCKETS - max_exact)).astype(jnp.int32)
    large = jnp.minimum(large, N_REL_BUCKETS - 1)
    return jnp.where(dist < max_exact, dist, large)


def _bias_tables(rel_table, dil):
    n = ATT_BLOCK
    n_heads = rel_table.shape[1]
    by_offset = rel_table.astype(F32)[_t5_causal_bucket((n - jnp.arange(n + 1)) * dil)].T
    period = 3 * n + 1
    row = jnp.concatenate(
        [by_offset, jnp.full((n_heads, period - (n + 1)), MASK_VALUE, F32)], axis=1)
    band = jnp.tile(row, (1, n))[:, :n * (period - 1)].reshape(n_heads, n, period - 1)
    regular = band[:, :, :2 * n]
    start = jnp.where(jnp.arange(2 * n) >= n, regular, MASK_VALUE)
    return jnp.stack([regular, start])


def _head_expand_matrix():
    row = jnp.arange(2 * LANES)[:, None] % LANES
    col = jnp.arange(D_MODEL)[None, :] // HEAD_DIM
    return (row == col).astype(BF16)


def kernel(x, rel_bias, even_norm, even_w_in, even_conv_w, even_pool_w, even_pool_scale,
           even_w_out, odd_norm, odd_w_qkv, odd_q_norm, odd_k_norm, odd_w_o, ffn_norm,
           ffn_w_up, ffn_conv_w, ffn_conv_b, ffn_w_down):
    def ffn(t, layer):
        return _ffn(t, ffn_norm[layer][None], ffn_w_up[layer].astype(BF16), ffn_conv_w[layer],
                    ffn_conv_b[layer][None], ffn_w_down[layer].astype(BF16))

    x = _mixer(x, even_norm[0][None], even_w_in[0].astype(BF16), even_conv_w[0],
               even_pool_w[0].astype(BF16), even_pool_scale[0][None],
               even_w_out[0].astype(BF16))
    x = ffn(x, 0)

    gq_pair = jnp.tile(odd_q_norm[0], 2)[None]
    gk_pair = jnp.tile(odd_k_norm[0], 2)[None]
    qkv = _qkv(x, odd_norm[0][None], odd_w_qkv[0].astype(BF16), gq_pair, gk_pair)
    branches = [_attn_branch(q, k, v, _bias_tables(rel_bias, dil))
                for dil, (q, k, v) in zip(DILATIONS, qkv)]
    x = _merge_out_proj(x, branches, _head_expand_matrix(), odd_w_o[0].astype(BF16))
    x = ffn(x, 1)
    return x
```

```python
import functools
import math

import jax
import jax.numpy as jnp
from jax import lax
from jax.experimental import pallas as pl
from jax.experimental.pallas import tpu as pltpu

D_MODEL = 1024
CONV_WIDTH = 3
A_WIDTH = 512
B_WIDTH = 512
POOL_WINDOWS = (2, 4, 8, 16)
POOL_GROUP = 128
EVEN_IN = 3 * A_WIDTH + B_WIDTH
HEAD_DIM = 64
N_HEADS = 16
DILATED_PAIRS = ((128, 1), (512, 4), (2048, 16))
DILATIONS = tuple(dil for _, dil in DILATED_PAIRS)
N_REL_BUCKETS = 32
REL_MAX_DISTANCE = 2048
D_FF = 2816
EPS = 1e-6
MASK_VALUE = -1e30

BF16 = jnp.bfloat16
F32 = jnp.float32

LANES = 128
ATT_BLOCK = 128
ROW_TILE = 512
FFN_ROW_TILE = 1024
POOL_HALO = 16
CONV_HALO = 8
FF_CHUNK = 256
SHIFT_BASE = 8
VMEM_LIMIT = 56 * 1024 * 1024


def _const_spec(shape):
    nd = len(shape)
    return pl.BlockSpec(shape, lambda *_: (0,) * nd, pipeline_mode=pl.Buffered(1))


def _params(n_axes):
    return pltpu.CompilerParams(
        dimension_semantics=("arbitrary",) * n_axes, vmem_limit_bytes=VMEM_LIMIT)


def _rmsnorm(x, g):
    ms = jnp.mean(x * x, axis=-1, keepdims=True)
    return x * lax.rsqrt(ms + EPS) * g


def _mixer_kernel(x_ref, g_ref, win_ref, cw_ref, pw_ref, ps_ref, wout_ref, o_ref,
                  ch_buf, p_buf):
    ts = x_ref.shape[1]
    s = pl.program_id(1)

    @pl.when(s == 0)
    def _():
        ch_buf[0:POOL_HALO, :] = jnp.zeros((POOL_HALO, A_WIDTH), F32)
        p_buf[0:POOL_HALO, :] = jnp.zeros((POOL_HALO, B_WIDTH), F32)

    x = x_ref[0]
    xn = _rmsnorm(x, g_ref[...]).astype(BF16)
    proj = jnp.dot(xn, win_ref[...], preferred_element_type=F32)
    h = proj[:, 0:A_WIDTH]
    gate_b = proj[:, A_WIDTH:2 * A_WIDTH]
    gate_c = proj[:, 2 * A_WIDTH:3 * A_WIDTH]
    pin = proj[:, 3 * A_WIDTH:]

    ch = gate_c * h
    ch_buf[POOL_HALO:POOL_HALO + ts, :] = ch
    p_buf[POOL_HALO:POOL_HALO + ts, :] = pin

    cw = cw_ref[...]
    conv = (cw[0:1] * ch_buf[POOL_HALO - 2:POOL_HALO - 2 + ts, :]
            + cw[1:2] * ch_buf[POOL_HALO - 1:POOL_HALO - 1 + ts, :]
            + cw[2:3] * ch)
    ya = gate_b * conv

    pos = s * ts + lax.broadcasted_iota(jnp.int32, (ts, 1), 0)
    parts = [ya]
    for g, k in enumerate(POOL_WINDOWS):
        cols = slice(g * POOL_GROUP, (g + 1) * POOL_GROUP)
        cur = pin[:, cols]
        acc = cur
        for j in range(1, k):
            acc = acc + p_buf[POOL_HALO - j:POOL_HALO - j + ts, cols]
        cnt = jnp.minimum(pos + 1, k).astype(F32)
        pooled = acc / cnt - cur
        yb = jnp.dot(pooled.astype(BF16), pw_ref[g], preferred_element_type=F32)
        parts.append(yb * ps_ref[:, cols])
    y = jnp.concatenate(parts, axis=-1).astype(BF16)
    o_ref[0] = x + jnp.dot(y, wout_ref[...], preferred_element_type=F32)

    ch_buf[0:POOL_HALO, :] = ch_buf[ts:ts + POOL_HALO, :]
    p_buf[0:POOL_HALO, :] = p_buf[ts:ts + POOL_HALO, :]


def _mixer(x, g, w_in, conv_w, pool_w, pool_scale, w_out):
    b, s, d = x.shape
    ts = ROW_TILE
    row_spec = pl.BlockSpec((1, ts, d), lambda i, j: (i, j, 0))
    return pl.pallas_call(
        _mixer_kernel,
        grid=(b, s // ts),
        in_specs=[row_spec, _const_spec((1, d)), _const_spec((d, EVEN_IN)),
                  _const_spec((CONV_WIDTH, A_WIDTH)),
                  _const_spec((len(POOL_WINDOWS), POOL_GROUP, POOL_GROUP)),
                  _const_spec((1, B_WIDTH)), _const_spec((d, d))],
        out_specs=row_spec,
        out_shape=jax.ShapeDtypeStruct(x.shape, F32),
        scratch_shapes=[pltpu.VMEM((POOL_HALO + ts, A_WIDTH), F32),
                        pltpu.VMEM((POOL_HALO + ts, B_WIDTH), F32)],
        compiler_params=_params(2),
        name="mixer",
    )(x, g, w_in, conv_w, pool_w, pool_scale, w_out)


def _ffn_kernel(x_ref, g_ref, wup_ref, cw_ref, cb_ref, wdn_ref, o_ref, h_buf, carry, u_scr):
    ts = x_ref.shape[1]
    s = pl.program_id(1)

    @pl.when(s == 0)
    def _():
        carry[...] = jnp.zeros(carry.shape, F32)

    x = x_ref[0]
    xn = _rmsnorm(x, g_ref[...]).astype(BF16)
    slabs = FF_CHUNK // LANES
    for j in range(D_FF // FF_CHUNK):
        halves = []
        for half in range(2):
            c0 = half * D_FF + j * FF_CHUNK
            u = jnp.dot(xn, wup_ref[:, c0:c0 + FF_CHUNK], preferred_element_type=F32)
            parts = []
            for sl in range(slabs):
                cols = slice(c0 + sl * LANES, c0 + (sl + 1) * LANES)
                slot = ((j % 2) * 2 + half) * slabs + sl
                u_sl = u[:, sl * LANES:(sl + 1) * LANES]
                u_scr[slot, pl.ds(SHIFT_BASE - 4, 2, stride=2), :] = carry[CONV_HALO - 2:, cols]
                u_scr[slot, pl.ds(SHIFT_BASE, ts, stride=2), :] = u_sl
                carry[:, cols] = u_sl[ts - CONV_HALO:ts, :]
                w = cw_ref[:, cols]
                parts.append(w[0:1] * u_scr[slot, pl.ds(SHIFT_BASE - 4, ts, stride=2), :]
                             + w[1:2] * u_scr[slot, pl.ds(SHIFT_BASE - 2, ts, stride=2), :]
                             + w[2:3] * u_sl + cb_ref[:, cols])
            halves.append(jnp.concatenate(parts, axis=-1))
        gate, up = halves
        h_buf[:, j * FF_CHUNK:(j + 1) * FF_CHUNK] = (gate * jax.nn.sigmoid(gate) * up).astype(BF16)
    o_ref[0] = x + jnp.dot(h_buf[...], wdn_ref[...], preferred_element_type=F32)


def _ffn(x, g, w_up, conv_w, conv_b, w_down):
    b, s, d = x.shape
    ts = FFN_ROW_TILE
    row_spec = pl.BlockSpec((1, ts, d), lambda i, j: (i, j, 0))
    return pl.pallas_call(
        _ffn_kernel,
        grid=(b, s // ts),
        in_specs=[row_spec, _const_spec((1, d)), _const_spec((d, 2 * D_FF)),
                  _const_spec((CONV_WIDTH, 2 * D_FF)), _const_spec((1, 2 * D_FF)),
                  _const_spec((D_FF, d))],
        out_specs=row_spec,
        out_shape=jax.ShapeDtypeStruct(x.shape, F32),
        scratch_shapes=[pltpu.VMEM((ts, D_FF), BF16),
                        pltpu.VMEM((CONV_HALO, 2 * D_FF), F32),
                        pltpu.VMEM((4 * FF_CHUNK // LANES, SHIFT_BASE + 2 * ts, LANES), F32)],
        compiler_params=_params(2),
        name="ffn",
    )(x, g, w_up, conv_w, conv_b, w_down)


def _head_pair_norm(t, g_pair, scale):
    lane = lax.broadcasted_iota(jnp.int32, t.shape, 1)
    lo = lane < HEAD_DIM
    sq = t * t
    s_lo = jnp.sum(jnp.where(lo, sq, 0.0), axis=-1, keepdims=True)
    s_hi = jnp.sum(jnp.where(lo, 0.0, sq), axis=-1, keepdims=True)
    r = lax.rsqrt(jnp.where(lo, s_lo, s_hi) * (1.0 / HEAD_DIM) + EPS)
    return t * r * (g_pair * scale)


def _qkv_kernel(x_ref, g_ref, w_ref, gq_ref, gk_ref, *refs):
    n_dil = len(DILATIONS)
    out_refs = [refs[3 * di:3 * di + 3] for di in range(n_dil)]
    stage = refs[3 * n_dil]
    d = D_MODEL
    ts = x_ref.shape[1]
    x = x_ref[0]
    xn = _rmsnorm(x, g_ref[...]).astype(BF16)
    qkv = jnp.dot(xn, w_ref[...], preferred_element_type=F32)
    n_slabs = d // LANES
    for which in range(3):
        for hp in range(n_slabs):
            cols = slice(hp * LANES, (hp + 1) * LANES)
            slab = qkv[:, which * d + hp * LANES:which * d + (hp + 1) * LANES]
            if which == 0:
                slab = _head_pair_norm(slab, gq_ref[...], HEAD_DIM ** -0.5)
            elif which == 1:
                slab = _head_pair_norm(slab, gk_ref[...], 1.0)
            slot = which * n_slabs + hp
            stage[slot] = slab
            for di, dil in enumerate(DILATIONS):
                o_ref = out_refs[di][which]
                if dil == 1:
                    o_ref[0, :, cols] = slab.astype(BF16)
                else:
                    for r in range(dil):
                        o_ref[r, :, cols] = stage[slot, pl.ds(r, ts // dil, stride=dil), :].astype(BF16)


def _qkv(x, g, w_qkv, gq_pair, gk_pair):
    b, s, d = x.shape
    ts = ROW_TILE
    row_spec = pl.BlockSpec((1, ts, d), lambda i, j: (i, j, 0))
    out_specs, out_shapes = [], []
    for dil in DILATIONS:
        out_specs += [pl.BlockSpec((None, dil, ts // dil, d), lambda i, j: (i, 0, j, 0))] * 3
        out_shapes += [jax.ShapeDtypeStruct((b, dil, s // dil, d), BF16)] * 3
    outs = pl.pallas_call(
        _qkv_kernel,
        grid=(b, s // ts),
        in_specs=[row_spec, _const_spec((1, d)), _const_spec((d, 3 * d)),
                  _const_spec((1, LANES)), _const_spec((1, LANES))],
        out_specs=out_specs,
        out_shape=out_shapes,
        scratch_shapes=[pltpu.VMEM((3 * d // LANES, ts, LANES), F32)],
        compiler_params=_params(2),
        name="qkv",
    )(x, g, w_qkv, gq_pair, gk_pair)
    return [outs[3 * di:3 * di + 3] for di in range(len(DILATIONS))]


def _attn_kernel(q_ref, k_ref, v_ref, kh_ref, vh_ref, bias_ref, o_ref, m_ref, l_ref,
                 k_buf, v_buf, *, units):
    nb = ATT_BLOCK
    first_step = pl.program_id(2) == 0
    k_buf[0:nb, :] = kh_ref[...]
    v_buf[0:nb, :] = vh_ref[...]
    k_buf[nb:, :] = k_ref[...]
    v_buf[nb:, :] = v_ref[...]

    lane = lax.broadcasted_iota(jnp.int32, (nb, LANES), 1)
    lo = lane < HEAD_DIM

    def unit(u, carry):
        r0 = pl.multiple_of(u * nb, nb)
        table = jnp.where(jnp.logical_and(first_step, u == 0), 1, 0)
        m_tile = jnp.zeros((nb, LANES), F32)
        l_tile = jnp.ones((nb, LANES), F32)
        for hp in range(N_HEADS // 2):
            cols = slice(hp * LANES, (hp + 1) * LANES)
            q_pair = q_ref[pl.ds(r0, nb), cols]
            k_win = k_buf[pl.ds(r0, 2 * nb), cols]
            v_win = v_buf[pl.ds(r0, 2 * nb), cols]
            outs = []
            for sub in range(2):
                h = 2 * hp + sub
                q_h = jnp.where(lo if sub == 0 else jnp.logical_not(lo), q_pair,
                                jnp.zeros_like(q_pair))
                sc = lax.dot_general(q_h, k_win, (((1,), (1,)), ((), ())),
                                     preferred_element_type=F32)
                sc = sc + bias_ref[table, h]
                m = jnp.max(sc, axis=-1, keepdims=True)
                p = jnp.exp(sc - m)
                den = jnp.sum(p, axis=-1, keepdims=True)
                pv = jnp.dot(p.astype(BF16), v_win, preferred_element_type=F32)
                outs.append(pv / den)
                m_tile = jnp.where(lane == h, m, m_tile)
                l_tile = jnp.where(lane == h, den, l_tile)
            o_ref[pl.ds(r0, nb), cols] = jnp.where(lo, outs[0], outs[1]).astype(BF16)
        m_ref[pl.ds(r0, nb), :] = m_tile
        l_ref[pl.ds(r0, nb), :] = l_tile
        return carry

    lax.fori_loop(0, units, unit, 0)


def _attn_branch(q, k, v, bias):
    b, dil, length, d = q.shape
    nb = ATT_BLOCK
    units = min(8, length // nb)
    rows = units * nb
    main = pl.BlockSpec((None, None, rows, d), lambda bi, r, i: (bi, r, i, 0))
    halo = pl.BlockSpec((None, None, nb, d),
                        lambda bi, r, i: (bi, r, jnp.maximum(i * units - 1, 0), 0))
    stat = pl.BlockSpec((None, None, rows, LANES), lambda bi, r, i: (bi, r, i, 0))
    stat_shape = jax.ShapeDtypeStruct((b, dil, length, LANES), F32)
    return pl.pallas_call(
        functools.partial(_attn_kernel, units=units),
        grid=(b, dil, length // rows),
        in_specs=[main, main, main, halo, halo, _const_spec(bias.shape)],
        out_specs=[main, stat, stat],
        out_shape=[jax.ShapeDtypeStruct(q.shape, BF16), stat_shape, stat_shape],
        scratch_shapes=[pltpu.VMEM((nb + rows, d), BF16), pltpu.VMEM((nb + rows, d), BF16)],
        compiler_params=_params(3),
        name=f"attn_d{dil}",
    )(q, k, v, k, v, bias)


def _merge_kernel(x_ref, *refs):
    n_dil = len(DILATIONS)
    branch_refs = [refs[3 * di:3 * di + 3] for di in range(n_dil)]
    e_ref, wo_ref, out_ref, o_scr, st_scr = refs[3 * n_dil:]
    ts = x_ref.shape[1]
    n_slabs = D_MODEL // LANES

    def token_order(ref, dil, scr, slot, cols=slice(None)):
        if dil == 1:
            return ref[0, :, cols].astype(F32)
        for r in range(dil):
            scr[slot, pl.ds(r, ts // dil, stride=dil), :] = ref[r, :, cols].astype(F32)
        return scr[slot]

    ms, ls = [], []
    for di, dil in enumerate(DILATIONS):
        ms.append(token_order(branch_refs[di][1], dil, st_scr, 2 * di))
        ls.append(token_order(branch_refs[di][2], dil, st_scr, 2 * di + 1))
    m_max = functools.reduce(jnp.maximum, ms)
    ws = [l * jnp.exp(m - m_max) for m, l in zip(ms, ls)]
    total = functools.reduce(jnp.add, ws)
    c_wides = []
    for w in ws:
        c = w / total
        c_hi = c.astype(BF16)
        c_lo = (c - c_hi.astype(F32)).astype(BF16)
        c_wides.append(jnp.dot(jnp.concatenate([c_hi, c_lo], axis=-1), e_ref[...],
                               preferred_element_type=F32))
    slabs = []
    for hp in range(n_slabs):
        cols = slice(hp * LANES, (hp + 1) * LANES)
        merged = None
        for di, dil in enumerate(DILATIONS):
            o = token_order(branch_refs[di][0], dil, o_scr, di * n_slabs + hp, cols)
            term = c_wides[di][:, cols] * o
            merged = term if merged is None else merged + term
        slabs.append(merged.astype(BF16))
    merged = jnp.concatenate(slabs, axis=-1)
    out_ref[0] = x_ref[0] + jnp.dot(merged, wo_ref[...], preferred_element_type=F32)


def _merge_out_proj(x, branches, expand, w_o):
    b, s, d = x.shape
    ts = ROW_TILE
    row_spec = pl.BlockSpec((1, ts, d), lambda i, j: (i, j, 0))
    in_specs, args = [row_spec], [x]
    for dil, (o, m, l) in zip(DILATIONS, branches):
        in_specs.append(pl.BlockSpec((None, dil, ts // dil, d), lambda i, j: (i, 0, j, 0)))
        in_specs += [pl.BlockSpec((None, dil, ts // dil, LANES), lambda i, j: (i, 0, j, 0))] * 2
        args += [o, m, l]
    n_dil = len(DILATIONS)
    return pl.pallas_call(
        _merge_kernel,
        grid=(b, s // ts),
        in_specs=in_specs + [_const_spec(expand.shape), _const_spec((d, d))],
        out_specs=row_spec,
        out_shape=jax.ShapeDtypeStruct(x.shape, F32),
        scratch_shapes=[pltpu.VMEM((n_dil * d // LANES, ts, LANES), F32),
                        pltpu.VMEM((2 * n_dil, ts, LANES), F32)],
        compiler_params=_params(2),
        name="merge_out_proj",
    )(x, *args[1:], expand, w_o)


def _t5_causal_bucket(dist):
    max_exact = N_REL_BUCKETS // 2
    dd = jnp.maximum(dist, 1).astype(F32)
    large = max_exact + (jnp.log(dd / max_exact) / math.log(REL_MAX_DISTANCE / max_exact)
                         * (N_REL_BUCKETS - max_exact)).astype(jnp.int32)
    large = jnp.minimum(large, N_REL_BUCKETS - 1)
    return jnp.where(dist < max_exact, dist, large)


def _bias_tables(rel_table, dil):
    n = ATT_BLOCK
    n_heads = rel_table.shape[1]
    by_offset = rel_table.astype(F32)[_t5_causal_bucket((n - jnp.arange(n + 1)) * dil)].T
    period = 3 * n + 1
    row = jnp.concatenate(
        [by_offset, jnp.full((n_heads, period - (n + 1)), MASK_VALUE, F32)], axis=1)
    band = jnp.tile(row, (1, n))[:, :n * (period - 1)].reshape(n_heads, n, period - 1)
    regular = band[:, :, :2 * n]
    start = jnp.where(jnp.arange(2 * n) >= n, regular, MASK_VALUE)
    return jnp.stack([regular, start])


def _head_expand_matrix():
    row = jnp.arange(2 * LANES)[:, None] % LANES
    col = jnp.arange(D_MODEL)[None, :] // HEAD_DIM
    return (row == col).astype(BF16)


def kernel(x, rel_bias, even_norm, even_w_in, even_conv_w, even_pool_w, even_pool_scale,
           even_w_out, odd_norm, odd_w_qkv, odd_q_norm, odd_k_norm, odd_w_o, ffn_norm,
           ffn_w_up, ffn_conv_w, ffn_conv_b, ffn_w_down):
    def ffn(t, layer):
        return _ffn(t, ffn_norm[layer][None], ffn_w_up[layer].astype(BF16), ffn_conv_w[layer],
                    ffn_conv_b[layer][None], ffn_w_down[layer].astype(BF16))

    x = _mixer(x, even_norm[0][None], even_w_in[0].astype(BF16), even_conv_w[0],
               even_pool_w[0].astype(BF16), even_pool_scale[0][None],
               even_w_out[0].astype(BF16))
    x = ffn(x, 0)

    gq_pair = jnp.tile(odd_q_norm[0], 2)[None]
    gk_pair = jnp.tile(odd_k_norm[0], 2)[None]
    qkv = _qkv(x, odd_norm[0][None], odd_w_qkv[0].astype(BF16), gq_pair, gk_pair)
    branches = [_attn_branch(q, k, v, _bias_tables(rel_bias, dil))
                for dil, (q, k, v) in zip(DILATIONS, qkv)]
    x = _merge_out_proj(x, branches, _head_expand_matrix(), odd_w_o[0].astype(BF16))
    x = ffn(x, 1)
    return x
```

```python
import functools
import math

import jax
import jax.numpy as jnp
from jax import lax
from jax.experimental import pallas as pl
from jax.experimental.pallas import tpu as pltpu

D_MODEL = 1024
CONV_WIDTH = 3
A_WIDTH = 512
B_WIDTH = 512
POOL_WINDOWS = (2, 4, 8, 16)
POOL_GROUP = 128
EVEN_IN = 3 * A_WIDTH + B_WIDTH
HEAD_DIM = 64
N_HEADS = 16
DILATED_PAIRS = ((128, 1), (512, 4), (2048, 16))
DILATIONS = tuple(dil for _, dil in DILATED_PAIRS)
N_REL_BUCKETS = 32
REL_MAX_DISTANCE = 2048
D_FF = 2816
EPS = 1e-6
MASK_VALUE = -1e30

BF16 = jnp.bfloat16
F32 = jnp.float32

LANES = 128
ATT_BLOCK = 128
ATT_UNITS_PER_STEP = 8
LOG2E = math.log2(math.e)
ROW_TILE = 512
FFN_ROW_TILE = 1024
POOL_HALO = 16
CONV_HALO = 8
FF_CHUNK = 256
SHIFT_BASE = 8
VMEM_LIMIT = 56 * 1024 * 1024


def _const_spec(shape):
    nd = len(shape)
    return pl.BlockSpec(shape, lambda *_: (0,) * nd, pipeline_mode=pl.Buffered(1))


def _params(n_axes):
    return pltpu.CompilerParams(
        dimension_semantics=("arbitrary",) * n_axes, vmem_limit_bytes=VMEM_LIMIT)


def _rmsnorm(x, g):
    ms = jnp.mean(x * x, axis=-1, keepdims=True)
    return x * lax.rsqrt(ms + EPS) * g


def _mixer_kernel(x_ref, g_ref, win_ref, cw_ref, pw_ref, ps_ref, wout_ref, o_ref,
                  ch_buf, p_buf):
    ts = x_ref.shape[1]
    s = pl.program_id(1)

    @pl.when(s == 0)
    def _():
        ch_buf[0:POOL_HALO, :] = jnp.zeros((POOL_HALO, A_WIDTH), F32)
        p_buf[0:POOL_HALO, :] = jnp.zeros((POOL_HALO, B_WIDTH), F32)

    x = x_ref[0]
    xn = _rmsnorm(x, g_ref[...]).astype(BF16)
    proj = jnp.dot(xn, win_ref[...], preferred_element_type=F32)
    h = proj[:, 0:A_WIDTH]
    gate_b = proj[:, A_WIDTH:2 * A_WIDTH]
    gate_c = proj[:, 2 * A_WIDTH:3 * A_WIDTH]
    pin = proj[:, 3 * A_WIDTH:]

    ch = gate_c * h
    ch_buf[POOL_HALO:POOL_HALO + ts, :] = ch
    p_buf[POOL_HALO:POOL_HALO + ts, :] = pin

    cw = cw_ref[...]
    conv = (cw[0:1] * ch_buf[POOL_HALO - 2:POOL_HALO - 2 + ts, :]
            + cw[1:2] * ch_buf[POOL_HALO - 1:POOL_HALO - 1 + ts, :]
            + cw[2:3] * ch)
    ya = gate_b * conv

    pos = s * ts + lax.broadcasted_iota(jnp.int32, (ts, 1), 0)
    parts = [ya]
    for g, k in enumerate(POOL_WINDOWS):
        cols = slice(g * POOL_GROUP, (g + 1) * POOL_GROUP)
        cur = pin[:, cols]
        acc = cur
        for j in range(1, k):
            acc = acc + p_buf[POOL_HALO - j:POOL_HALO - j + ts, cols]
        cnt = jnp.minimum(pos + 1, k).astype(F32)
        pooled = acc / cnt - cur
        yb = jnp.dot(pooled.astype(BF16), pw_ref[g], preferred_element_type=F32)
        parts.append(yb * ps_ref[:, cols])
    y = jnp.concatenate(parts, axis=-1).astype(BF16)
    o_ref[0] = x + jnp.dot(y, wout_ref[...], preferred_element_type=F32)

    ch_buf[0:POOL_HALO, :] = ch_buf[ts:ts + POOL_HALO, :]
    p_buf[0:POOL_HALO, :] = p_buf[ts:ts + POOL_HALO, :]


def _mixer(x, g, w_in, conv_w, pool_w, pool_scale, w_out):
    b, s, d = x.shape
    ts = ROW_TILE
    row_spec = pl.BlockSpec((1, ts, d), lambda i, j: (i, j, 0))
    return pl.pallas_call(
        _mixer_kernel,
        grid=(b, s // ts),
        in_specs=[row_spec, _const_spec((1, d)), _const_spec((d, EVEN_IN)),
                  _const_spec((CONV_WIDTH, A_WIDTH)),
                  _const_spec((len(POOL_WINDOWS), POOL_GROUP, POOL_GROUP)),
                  _const_spec((1, B_WIDTH)), _const_spec((d, d))],
        out_specs=row_spec,
        out_shape=jax.ShapeDtypeStruct(x.shape, F32),
        scratch_shapes=[pltpu.VMEM((POOL_HALO + ts, A_WIDTH), F32),
                        pltpu.VMEM((POOL_HALO + ts, B_WIDTH), F32)],
        compiler_params=_params(2),
        name="mixer",
    )(x, g, w_in, conv_w, pool_w, pool_scale, w_out)


def _ffn_kernel(x_ref, g_ref, wup_ref, cw_ref, cb_ref, wdn_ref, o_ref, h_buf, carry, u_scr):
    ts = x_ref.shape[1]
    s = pl.program_id(1)

    @pl.when(s == 0)
    def _():
        carry[...] = jnp.zeros(carry.shape, F32)

    x = x_ref[0]
    xn = _rmsnorm(x, g_ref[...]).astype(BF16)
    slabs = FF_CHUNK // LANES
    for j in range(D_FF // FF_CHUNK):
        halves = []
        for half in range(2):
            c0 = half * D_FF + j * FF_CHUNK
            u = jnp.dot(xn, wup_ref[:, c0:c0 + FF_CHUNK], preferred_element_type=F32)
            parts = []
            for sl in range(slabs):
                cols = slice(c0 + sl * LANES, c0 + (sl + 1) * LANES)
                slot = ((j % 2) * 2 + half) * slabs + sl
                u_sl = u[:, sl * LANES:(sl + 1) * LANES]
                u_scr[slot, pl.ds(SHIFT_BASE - 4, 2, stride=2), :] = carry[CONV_HALO - 2:, cols]
                u_scr[slot, pl.ds(SHIFT_BASE, ts, stride=2), :] = u_sl
                carry[:, cols] = u_sl[ts - CONV_HALO:ts, :]
                w = cw_ref[:, cols]
                parts.append(w[0:1] * u_scr[slot, pl.ds(SHIFT_BASE - 4, ts, stride=2), :]
                             + w[1:2] * u_scr[slot, pl.ds(SHIFT_BASE - 2, ts, stride=2), :]
                             + w[2:3] * u_sl + cb_ref[:, cols])
            halves.append(jnp.concatenate(parts, axis=-1))
        gate, up = halves
        h_buf[:, j * FF_CHUNK:(j + 1) * FF_CHUNK] = (gate * jax.nn.sigmoid(gate) * up).astype(BF16)
    o_ref[0] = x + jnp.dot(h_buf[...], wdn_ref[...], preferred_element_type=F32)


def _ffn(x, g, w_up, conv_w, conv_b, w_down):
    b, s, d = x.shape
    ts = FFN_ROW_TILE
    row_spec = pl.BlockSpec((1, ts, d), lambda i, j: (i, j, 0))
    return pl.pallas_call(
        _ffn_kernel,
        grid=(b, s // ts),
        in_specs=[row_spec, _const_spec((1, d)), _const_spec((d, 2 * D_FF)),
                  _const_spec((CONV_WIDTH, 2 * D_FF)), _const_spec((1, 2 * D_FF)),
                  _const_spec((D_FF, d))],
        out_specs=row_spec,
        out_shape=jax.ShapeDtypeStruct(x.shape, F32),
        scratch_shapes=[pltpu.VMEM((ts, D_FF), BF16),
                        pltpu.VMEM((CONV_HALO, 2 * D_FF), F32),
                        pltpu.VMEM((4 * FF_CHUNK // LANES, SHIFT_BASE + 2 * ts, LANES), F32)],
        compiler_params=_params(2),
        name="ffn",
    )(x, g, w_up, conv_w, conv_b, w_down)


def _head_pair_norm(t, g_pair, scale):
    lane = lax.broadcasted_iota(jnp.int32, t.shape, 1)
    lo = lane < HEAD_DIM
    sq = t * t
    s_lo = jnp.sum(jnp.where(lo, sq, 0.0), axis=-1, keepdims=True)
    s_hi = jnp.sum(jnp.where(lo, 0.0, sq), axis=-1, keepdims=True)
    r = lax.rsqrt(jnp.where(lo, s_lo, s_hi) * (1.0 / HEAD_DIM) + EPS)
    return t * r * (g_pair * scale)


def _qkv_kernel(x_ref, g_ref, w_ref, gq_ref, gk_ref, *refs):
    n_dil = len(DILATIONS)
    out_refs = [refs[3 * di:3 * di + 3] for di in range(n_dil)]
    stage = refs[3 * n_dil]
    d = D_MODEL
    ts = x_ref.shape[1]
    x = x_ref[0]
    xn = _rmsnorm(x, g_ref[...]).astype(BF16)
    qkv = jnp.dot(xn, w_ref[...], preferred_element_type=F32)
    n_slabs = d // LANES
    for which in range(3):
        for hp in range(n_slabs):
            cols = slice(hp * LANES, (hp + 1) * LANES)
            slab = qkv[:, which * d + hp * LANES:which * d + (hp + 1) * LANES]
            if which == 0:
                slab = _head_pair_norm(slab, gq_ref[...], HEAD_DIM ** -0.5 * LOG2E)
            elif which == 1:
                slab = _head_pair_norm(slab, gk_ref[...], 1.0)
            slot = which * n_slabs + hp
            stage[slot] = slab
            for di, dil in enumerate(DILATIONS):
                o_ref = out_refs[di][which]
                if dil == 1:
                    o_ref[0, :, cols] = slab.astype(BF16)
                else:
                    for r in range(dil):
                        o_ref[r, :, cols] = stage[slot, pl.ds(r, ts // dil, stride=dil), :].astype(BF16)


def _qkv(x, g, w_qkv, gq_pair, gk_pair):
    b, s, d = x.shape
    ts = ROW_TILE
    row_spec = pl.BlockSpec((1, ts, d), lambda i, j: (i, j, 0))
    out_specs, out_shapes = [], []
    for dil in DILATIONS:
        out_specs += [pl.BlockSpec((None, dil, ts // dil, d), lambda i, j: (i, 0, j, 0))] * 3
        out_shapes += [jax.ShapeDtypeStruct((b, dil, s // dil, d), BF16)] * 3
    outs = pl.pallas_call(
        _qkv_kernel,
        grid=(b, s // ts),
        in_specs=[row_spec, _const_spec((1, d)), _const_spec((d, 3 * d)),
                  _const_spec((1, LANES)), _const_spec((1, LANES))],
        out_specs=out_specs,
        out_shape=out_shapes,
        scratch_shapes=[pltpu.VMEM((3 * d // LANES, ts, LANES), F32)],
        compiler_params=_params(2),
        name="qkv",
    )(x, g, w_qkv, gq_pair, gk_pair)
    return [outs[3 * di:3 * di + 3] for di in range(len(DILATIONS))]


def _attn_kernel(q_ref, k_ref, v_ref, kh_ref, vh_ref, bias_ref, o_ref, m_ref, l_ref):
    nb = ATT_BLOCK
    n_res, rows, _ = q_ref.shape
    first_step = pl.program_id(2) == 0
    first_table = jnp.where(first_step, 1, 0)

    lane = lax.broadcasted_iota(jnp.int32, (nb, LANES), 1)
    lo = lane < HEAD_DIM

    for res in range(n_res):
        for u in range(rows // nb):
            cur = slice(u * nb, (u + 1) * nb)
            m_ref[res, cur, :] = jnp.zeros((nb, LANES), F32)
            l_ref[res, cur, :] = jnp.ones((nb, LANES), F32)
            for hp in range(N_HEADS // 2):
                cols = slice(hp * LANES, (hp + 1) * LANES)
                q_pair = q_ref[res, cur, cols]
                if u == 0:
                    k_win = jnp.concatenate([kh_ref[res, :, cols], k_ref[res, cur, cols]], axis=0)
                    v_win = jnp.concatenate([vh_ref[res, :, cols], v_ref[res, cur, cols]], axis=0)
                    table = first_table
                else:
                    win = slice((u - 1) * nb, (u + 1) * nb)
                    k_win = k_ref[res, win, cols]
                    v_win = v_ref[res, win, cols]
                    table = 0
                outs = []
                for sub in range(2):
                    h = 2 * hp + sub
                    q_h = jnp.where(lo if sub == 0 else jnp.logical_not(lo), q_pair,
                                    jnp.zeros_like(q_pair))
                    sc = lax.dot_general(q_h, k_win, (((1,), (1,)), ((), ())),
                                         preferred_element_type=F32)
                    sc = sc + bias_ref[table, h]
                    m = jnp.max(sc, axis=-1, keepdims=True)
                    p = jnp.exp2(sc - m)
                    den = jnp.sum(p, axis=-1, keepdims=True)
                    pv = jnp.dot(p.astype(BF16), v_win, preferred_element_type=F32)
                    outs.append(pv / den)
                    m_ref[res, cur, h:h + 1] = m
                    l_ref[res, cur, h:h + 1] = den
                o_ref[res, cur, cols] = jnp.where(lo, outs[0], outs[1]).astype(BF16)


def _attn_branch(q, k, v, bias):
    b, dil, length, d = q.shape
    nb = ATT_BLOCK
    units = min(ATT_UNITS_PER_STEP, length // nb)
    n_res = min(dil, ATT_UNITS_PER_STEP // units)
    rows = units * nb
    main = pl.BlockSpec((None, n_res, rows, d), lambda bi, r, i: (bi, r, i, 0))
    halo = pl.BlockSpec((None, n_res, nb, d),
                        lambda bi, r, i: (bi, r, jnp.maximum(i * units - 1, 0), 0))
    stat = pl.BlockSpec((None, n_res, rows, LANES), lambda bi, r, i: (bi, r, i, 0))
    stat_shape = jax.ShapeDtypeStruct((b, dil, length, LANES), F32)
    return pl.pallas_call(
        _attn_kernel,
        grid=(b, dil // n_res, length // rows),
        in_specs=[main, main, main, halo, halo, _const_spec(bias.shape)],
        out_specs=[main, stat, stat],
        out_shape=[jax.ShapeDtypeStruct(q.shape, BF16), stat_shape, stat_shape],
        compiler_params=_params(3),
        name=f"attn_d{dil}",
    )(q, k, v, k, v, bias)


def _merge_kernel(x_ref, *refs):
    n_dil = len(DILATIONS)
    branch_refs = [refs[3 * di:3 * di + 3] for di in range(n_dil)]
    e_ref, wo_ref, out_ref, o_scr, st_scr = refs[3 * n_dil:]
    ts = x_ref.shape[1]
    n_slabs = D_MODEL // LANES

    def token_order(ref, dil, scr, slot, cols=slice(None)):
        if dil == 1:
            return ref[0, :, cols].astype(F32)
        for r in range(dil):
            scr[slot, pl.ds(r, ts // dil, stride=dil), :] = ref[r, :, cols].astype(F32)
        return scr[slot]

    ms, ls = [], []
    for di, dil in enumerate(DILATIONS):
        ms.append(token_order(branch_refs[di][1], dil, st_scr, 2 * di))
        ls.append(token_order(branch_refs[di][2], dil, st_scr, 2 * di + 1))
    m_max = functools.reduce(jnp.maximum, ms)
    ws = [l * jnp.exp2(m - m_max) for m, l in zip(ms, ls)]
    total = functools.reduce(jnp.add, ws)
    c_wides = []
    for w in ws:
        c = w / total
        c_hi = c.astype(BF16)
        c_lo = (c - c_hi.astype(F32)).astype(BF16)
        c_wides.append(jnp.dot(jnp.concatenate([c_hi, c_lo], axis=-1), e_ref[...],
                               preferred_element_type=F32))
    slabs = []
    for hp in range(n_slabs):
        cols = slice(hp * LANES, (hp + 1) * LANES)
        merged = None
        for di, dil in enumerate(DILATIONS):
            o = token_order(branch_refs[di][0], dil, o_scr, di * n_slabs + hp, cols)
            term = c_wides[di][:, cols] * o
            merged = term if merged is None else merged + term
        slabs.append(merged.astype(BF16))
    merged = jnp.concatenate(slabs, axis=-1)
    out_ref[0] = x_ref[0] + jnp.dot(merged, wo_ref[...], preferred_element_type=F32)


def _merge_out_proj(x, branches, expand, w_o):
    b, s, d = x.shape
    ts = ROW_TILE
    row_spec = pl.BlockSpec((1, ts, d), lambda i, j: (i, j, 0))
    in_specs, args = [row_spec], [x]
    for dil, (o, m, l) in zip(DILATIONS, branches):
        in_specs.append(pl.BlockSpec((None, dil, ts // dil, d), lambda i, j: (i, 0, j, 0)))
        in_specs += [pl.BlockSpec((None, dil, ts // dil, LANES), lambda i, j: (i, 0, j, 0))] * 2
        args += [o, m, l]
    n_dil = len(DILATIONS)
    return pl.pallas_call(
        _merge_kernel,
        grid=(b, s // ts),
        in_specs=in_specs + [_const_spec(expand.shape), _const_spec((d, d))],
        out_specs=row_spec,
        out_shape=jax.ShapeDtypeStruct(x.shape, F32),
        scratch_shapes=[pltpu.VMEM((n_dil * d // LANES, ts, LANES), F32),
                        pltpu.VMEM((2 * n_dil, ts, LANES), F32)],
        compiler_params=_params(2),
        name="merge_out_proj",
    )(x, *args[1:], expand, w_o)


def _t5_causal_bucket(dist):
    max_exact = N_REL_BUCKETS // 2
    dd = jnp.maximum(dist, 1).astype(F32)
    large = max_exact + (jnp.log(dd / max_exact) / math.log(REL_MAX_DISTANCE / max_exact)
                         * (N_REL_BUCKETS - max_exact)).astype(jnp.int32)
    large = jnp.minimum(large, N_REL_BUCKETS - 1)
    return jnp.where(dist < max_exact, dist, large)


def _bias_tables(rel_table, dil):
    n = ATT_BLOCK
    n_heads = rel_table.shape[1]
    by_offset = LOG2E * rel_table.astype(F32)[_t5_causal_bucket((n - jnp.arange(n + 1)) * dil)].T
    period = 3 * n + 1
    row = jnp.concatenate(
        [by_offset, jnp.full((n_heads, period - (n + 1)), MASK_VALUE, F32)], axis=1)
    band = jnp.tile(row, (1, n))[:, :n * (period - 1)].reshape(n_heads, n, period - 1)
    regular = band[:, :, :2 * n]
    start = jnp.where(jnp.arange(2 * n) >= n, regular, MASK_VALUE)
    return jnp.stack([regular, start])


def _head_expand_matrix():
    row = jnp.arange(2 * LANES)[:, None] % LANES
    col = jnp.arange(D_MODEL)[None, :] // HEAD_DIM
    return (row == col).astype(BF16)


def kernel(x, rel_bias, even_norm, even_w_in, even_conv_w, even_pool_w, even_pool_scale,
           even_w_out, odd_norm, odd_w_qkv, odd_q_norm, odd_k_norm, odd_w_o, ffn_norm,
           ffn_w_up, ffn_conv_w, ffn_conv_b, ffn_w_down):
    def ffn(t, layer):
        return _ffn(t, ffn_norm[layer][None], ffn_w_up[layer].astype(BF16), ffn_conv_w[layer],
                    ffn_conv_b[layer][None], ffn_w_down[layer].astype(BF16))

    x = _mixer(x, even_norm[0][None], even_w_in[0].astype(BF16), even_conv_w[0],
               even_pool_w[0].astype(BF16), even_pool_scale[0][None],
               even_w_out[0].astype(BF16))
    x = ffn(x, 0)

    gq_pair = jnp.tile(odd_q_norm[0], 2)[None]
    gk_pair = jnp.tile(odd_k_norm[0], 2)[None]
    qkv = _qkv(x, odd_norm[0][None], odd_w_qkv[0].astype(BF16), gq_pair, gk_pair)
    branches = [_attn_branch(q, k, v, _bias_tables(rel_bias, dil))
                for dil, (q, k, v) in zip(DILATIONS, qkv)]
    x = _merge_out_proj(x, branches, _head_expand_matrix(), odd_w_o[0].astype(BF16))
    x = ffn(x, 1)
    return x
```

```python
import functools
import math

import jax
import jax.numpy as jnp
from jax import lax
from jax.experimental import pallas as pl
from jax.experimental.pallas import tpu as pltpu

D_MODEL = 1024
CONV_WIDTH = 3
A_WIDTH = 512
B_WIDTH = 512
POOL_WINDOWS = (2, 4, 8, 16)
POOL_GROUP = 128
EVEN_IN = 3 * A_WIDTH + B_WIDTH
HEAD_DIM = 64
N_HEADS = 16
DILATED_PAIRS = ((128, 1), (512, 4), (2048, 16))
DILATIONS = tuple(dil for _, dil in DILATED_PAIRS)
N_REL_BUCKETS = 32
REL_MAX_DISTANCE = 2048
D_FF = 2816
EPS = 1e-6
MASK_VALUE = -1e30

BF16 = jnp.bfloat16
F32 = jnp.float32

LANES = 128
ATT_BLOCK = 128
ATT_UNITS_PER_STEP = 8
LOG2E = math.log2(math.e)
ROW_TILE = 512
FFN_ROW_TILE = 1024
POOL_HALO = 16
CONV_HALO = 8
FF_CHUNK = 256
QKV_CHUNK = 256
MIX_CHUNK = 256
SHIFT_BASE = 8
VMEM_LIMIT = 56 * 1024 * 1024


def _const_spec(shape):
    nd = len(shape)
    return pl.BlockSpec(shape, lambda *_: (0,) * nd, pipeline_mode=pl.Buffered(1))


def _params(n_axes):
    return pltpu.CompilerParams(
        dimension_semantics=("arbitrary",) * n_axes, vmem_limit_bytes=VMEM_LIMIT)


def _rmsnorm(x, g):
    ms = jnp.mean(x * x, axis=-1, keepdims=True)
    return x * lax.rsqrt(ms + EPS) * g


def _stage_rows(scr, slot, base, tile, prev):
    n_prev = prev.shape[0]
    scr[slot, pl.ds(base - 2 * n_prev, n_prev, stride=2), :] = prev
    scr[slot, pl.ds(base, tile.shape[0], stride=2), :] = tile


def _rows_back(scr, slot, base, n_rows, k):
    return scr[slot, pl.ds(base - 2 * k, n_rows, stride=2), :]


def _mixer_kernel(x_ref, g_ref, win_ref, cw_ref, pw_ref, ps_ref, wout_ref, o_ref,
                  y_buf, carry_a, carry_p, scr):
    ts = x_ref.shape[1]
    s = pl.program_id(1)
    base = 2 * POOL_HALO

    @pl.when(s == 0)
    def _():
        carry_a[...] = jnp.zeros(carry_a.shape, F32)
        carry_p[...] = jnp.zeros(carry_p.shape, F32)

    x = x_ref[0]
    xn = _rmsnorm(x, g_ref[...]).astype(BF16)
    slabs = MIX_CHUNK // LANES

    def proj(col0):
        return jnp.dot(xn, win_ref[:, col0:col0 + MIX_CHUNK], preferred_element_type=F32)

    for c in range(A_WIDTH // MIX_CHUNK):
        h = proj(c * MIX_CHUNK)
        gate_b = proj(A_WIDTH + c * MIX_CHUNK)
        gate_c = proj(2 * A_WIDTH + c * MIX_CHUNK)
        ch = gate_c * h
        for sl in range(slabs):
            slot = c * slabs + sl
            cols = slice(slot * LANES, (slot + 1) * LANES)
            ch_sl = ch[:, sl * LANES:(sl + 1) * LANES]
            _stage_rows(scr, slot, base, ch_sl, carry_a[CONV_HALO - 2:, cols])
            carry_a[:, cols] = ch_sl[ts - CONV_HALO:, :]
            cw = cw_ref[:, cols]
            conv = (cw[0:1] * _rows_back(scr, slot, base, ts, 2)
                    + cw[1:2] * _rows_back(scr, slot, base, ts, 1) + cw[2:3] * ch_sl)
            y_buf[:, cols] = (gate_b[:, sl * LANES:(sl + 1) * LANES] * conv).astype(BF16)

    pos = s * ts + lax.broadcasted_iota(jnp.int32, (ts, 1), 0)
    for c in range(B_WIDTH // MIX_CHUNK):
        pin = proj(3 * A_WIDTH + c * MIX_CHUNK)
        for sl in range(slabs):
            g = c * slabs + sl
            k = POOL_WINDOWS[g]
            slot = A_WIDTH // LANES + g
            cols = slice(g * POOL_GROUP, (g + 1) * POOL_GROUP)
            cur = pin[:, sl * LANES:(sl + 1) * LANES]
            _stage_rows(scr, slot, base, cur, carry_p[:, cols])
            carry_p[:, cols] = cur[ts - POOL_HALO:, :]
            acc = cur
            for j in range(1, k):
                acc = acc + _rows_back(scr, slot, base, ts, j)
            cnt = jnp.minimum(pos + 1, k).astype(F32)
            pooled = acc / cnt - cur
            yb = jnp.dot(pooled.astype(BF16), pw_ref[g], preferred_element_type=F32)
            y_buf[:, A_WIDTH + g * POOL_GROUP:A_WIDTH + (g + 1) * POOL_GROUP] = (
                yb * ps_ref[:, cols]).astype(BF16)

    o_ref[0] = x + jnp.dot(y_buf[...], wout_ref[...], preferred_element_type=F32)


def _mixer(x, g, w_in, conv_w, pool_w, pool_scale, w_out):
    b, s, d = x.shape
    ts = FFN_ROW_TILE
    row_spec = pl.BlockSpec((1, ts, d), lambda i, j: (i, j, 0))
    return pl.pallas_call(
        _mixer_kernel,
        grid=(b, s // ts),
        in_specs=[row_spec, _const_spec((1, d)), _const_spec((d, EVEN_IN)),
                  _const_spec((CONV_WIDTH, A_WIDTH)),
                  _const_spec((len(POOL_WINDOWS), POOL_GROUP, POOL_GROUP)),
                  _const_spec((1, B_WIDTH)), _const_spec((d, d))],
        out_specs=row_spec,
        out_shape=jax.ShapeDtypeStruct(x.shape, F32),
        scratch_shapes=[pltpu.VMEM((ts, d), BF16),
                        pltpu.VMEM((CONV_HALO, A_WIDTH), F32),
                        pltpu.VMEM((POOL_HALO, B_WIDTH), F32),
                        pltpu.VMEM(((A_WIDTH + B_WIDTH) // LANES, 2 * (POOL_HALO + ts), LANES),
                                   F32)],
        compiler_params=_params(2),
        name="mixer",
    )(x, g, w_in, conv_w, pool_w, pool_scale, w_out)


def _ffn_kernel(x_ref, g_ref, wup_ref, cw_ref, cb_ref, wdn_ref, o_ref, h_buf, carry, u_scr):
    ts = x_ref.shape[1]
    s = pl.program_id(1)

    @pl.when(s == 0)
    def _():
        carry[...] = jnp.zeros(carry.shape, F32)

    x = x_ref[0]
    xn = _rmsnorm(x, g_ref[...]).astype(BF16)
    slabs = FF_CHUNK // LANES
    for j in range(D_FF // FF_CHUNK):
        halves = []
        for half in range(2):
            c0 = half * D_FF + j * FF_CHUNK
            u = jnp.dot(xn, wup_ref[:, c0:c0 + FF_CHUNK], preferred_element_type=F32)
            parts = []
            for sl in range(slabs):
                cols = slice(c0 + sl * LANES, c0 + (sl + 1) * LANES)
                slot = ((j % 2) * 2 + half) * slabs + sl
                u_sl = u[:, sl * LANES:(sl + 1) * LANES]
                _stage_rows(u_scr, slot, SHIFT_BASE, u_sl, carry[CONV_HALO - 2:, cols])
                carry[:, cols] = u_sl[ts - CONV_HALO:ts, :]
                w = cw_ref[:, cols]
                parts.append(w[0:1] * _rows_back(u_scr, slot, SHIFT_BASE, ts, 2)
                             + w[1:2] * _rows_back(u_scr, slot, SHIFT_BASE, ts, 1)
                             + w[2:3] * u_sl + cb_ref[:, cols])
            halves.append(jnp.concatenate(parts, axis=-1))
        gate, up = halves
        h_buf[:, j * FF_CHUNK:(j + 1) * FF_CHUNK] = (gate * jax.nn.sigmoid(gate) * up).astype(BF16)
    o_ref[0] = x + jnp.dot(h_buf[...], wdn_ref[...], preferred_element_type=F32)


def _ffn(x, g, w_up, conv_w, conv_b, w_down):
    b, s, d = x.shape
    ts = FFN_ROW_TILE
    row_spec = pl.BlockSpec((1, ts, d), lambda i, j: (i, j, 0))
    return pl.pallas_call(
        _ffn_kernel,
        grid=(b, s // ts),
        in_specs=[row_spec, _const_spec((1, d)), _const_spec((d, 2 * D_FF)),
                  _const_spec((CONV_WIDTH, 2 * D_FF)), _const_spec((1, 2 * D_FF)),
                  _const_spec((D_FF, d))],
        out_specs=row_spec,
        out_shape=jax.ShapeDtypeStruct(x.shape, F32),
        scratch_shapes=[pltpu.VMEM((ts, D_FF), BF16),
                        pltpu.VMEM((CONV_HALO, 2 * D_FF), F32),
                        pltpu.VMEM((4 * FF_CHUNK // LANES, SHIFT_BASE + 2 * ts, LANES), F32)],
        compiler_params=_params(2),
        name="ffn",
    )(x, g, w_up, conv_w, conv_b, w_down)


def _head_pair_norm(t, g_pair, scale):
    lane = lax.broadcasted_iota(jnp.int32, t.shape, 1)
    lo = lane < HEAD_DIM
    sq = t * t
    s_lo = jnp.sum(jnp.where(lo, sq, 0.0), axis=-1, keepdims=True)
    s_hi = jnp.sum(jnp.where(lo, 0.0, sq), axis=-1, keepdims=True)
    r = lax.rsqrt(jnp.where(lo, s_lo, s_hi) * (1.0 / HEAD_DIM) + EPS)
    return t * r * (g_pair * scale)


def _qkv_kernel(x_ref, g_ref, w_ref, gq_ref, gk_ref, *refs):
    assert DILATIONS == (1, 4, 16)
    n_dil = len(DILATIONS)
    out1, out4, out16 = [refs[3 * di:3 * di + 3] for di in range(n_dil)]
    stage1, stage4 = refs[3 * n_dil:]
    d = D_MODEL
    ts = x_ref.shape[1]
    x = x_ref[0]
    xn = _rmsnorm(x, g_ref[...]).astype(BF16)
    slabs = QKV_CHUNK // LANES
    for c in range(3 * d // QKV_CHUNK):
        res = jnp.dot(xn, w_ref[:, c * QKV_CHUNK:(c + 1) * QKV_CHUNK],
                      preferred_element_type=F32)
        for sl in range(slabs):
            slot = c * slabs + sl
            which, hp = divmod(slot, d // LANES)
            cols = slice(hp * LANES, (hp + 1) * LANES)
            slab = res[:, sl * LANES:(sl + 1) * LANES]
            if which == 0:
                slab = _head_pair_norm(slab, gq_ref[...], HEAD_DIM ** -0.5 * LOG2E)
            elif which == 1:
                slab = _head_pair_norm(slab, gk_ref[...], 1.0)
            out1[which][0, :, cols] = slab.astype(BF16)
            stage1[slot] = slab
            for r4 in range(4):
                p4 = stage1[slot, pl.ds(r4, ts // 4, stride=4), :]
                out4[which][r4, :, cols] = p4.astype(BF16)
                stage4[4 * slot + r4] = p4
                for c4 in range(4):
                    p16 = stage4[4 * slot + r4, pl.ds(c4, ts // 16, stride=4), :]
                    out16[which][r4 + 4 * c4, :, cols] = p16.astype(BF16)


def _qkv(x, g, w_qkv, gq_pair, gk_pair):
    b, s, d = x.shape
    ts = ROW_TILE
    row_spec = pl.BlockSpec((1, ts, d), lambda i, j: (i, j, 0))
    out_specs, out_shapes = [], []
    for dil in DILATIONS:
        out_specs += [pl.BlockSpec((None, dil, ts // dil, d), lambda i, j: (i, 0, j, 0))] * 3
        out_shapes += [jax.ShapeDtypeStruct((b, dil, s // dil, d), BF16)] * 3
    outs = pl.pallas_call(
        _qkv_kernel,
        grid=(b, s // ts),
        in_specs=[row_spec, _const_spec((1, d)), _const_spec((d, 3 * d)),
                  _const_spec((1, LANES)), _const_spec((1, LANES))],
        out_specs=out_specs,
        out_shape=out_shapes,
        scratch_shapes=[pltpu.VMEM((3 * d // LANES, ts, LANES), F32),
                        pltpu.VMEM((4 * 3 * d // LANES, ts // 4, LANES), F32)],
        compiler_params=_params(2),
        name="qkv",
    )(x, g, w_qkv, gq_pair, gk_pair)
    return [outs[3 * di:3 * di + 3] for di in range(len(DILATIONS))]


def _attn_kernel(q_ref, k_ref, v_ref, kh_ref, vh_ref, bias_ref, o_ref, m_ref, l_ref):
    nb = ATT_BLOCK
    n_res, rows, _ = q_ref.shape
    first_step = pl.program_id(2) == 0
    first_table = jnp.where(first_step, 1, 0)

    lane = lax.broadcasted_iota(jnp.int32, (nb, LANES), 1)
    lo = lane < HEAD_DIM

    for res in range(n_res):
        for u in range(rows // nb):
            cur = slice(u * nb, (u + 1) * nb)
            m_ref[res, cur, :] = jnp.zeros((nb, LANES), F32)
            l_ref[res, cur, :] = jnp.ones((nb, LANES), F32)
            for hp in range(N_HEADS // 2):
                cols = slice(hp * LANES, (hp + 1) * LANES)
                q_pair = q_ref[res, cur, cols]
                if u == 0:
                    k_win = jnp.concatenate([kh_ref[res, :, cols], k_ref[res, cur, cols]], axis=0)
                    v_win = jnp.concatenate([vh_ref[res, :, cols], v_ref[res, cur, cols]], axis=0)
                    table = first_table
                else:
                    win = slice((u - 1) * nb, (u + 1) * nb)
                    k_win = k_ref[res, win, cols]
                    v_win = v_ref[res, win, cols]
                    table = 0
                outs = []
                for sub in range(2):
                    h = 2 * hp + sub
                    q_h = jnp.where(lo if sub == 0 else jnp.logical_not(lo), q_pair,
                                    jnp.zeros_like(q_pair))
                    sc = lax.dot_general(q_h, k_win, (((1,), (1,)), ((), ())),
                                         preferred_element_type=F32)
                    sc = sc + bias_ref[table, h]
                    m = jnp.max(sc, axis=-1, keepdims=True)
                    p = jnp.exp2(sc - m)
                    den = jnp.sum(p, axis=-1, keepdims=True)
                    pv = jnp.dot(p.astype(BF16), v_win, preferred_element_type=F32)
                    outs.append(pv / den)
                    m_ref[res, cur, h:h + 1] = m
                    l_ref[res, cur, h:h + 1] = den
                o_ref[res, cur, cols] = jnp.where(lo, outs[0], outs[1]).astype(BF16)


def _attn_branch(q, k, v, bias):
    b, dil, length, d = q.shape
    nb = ATT_BLOCK
    units = min(ATT_UNITS_PER_STEP, length // nb)
    n_res = min(dil, ATT_UNITS_PER_STEP // units)
    rows = units * nb
    main = pl.BlockSpec((None, n_res, rows, d), lambda bi, r, i: (bi, r, i, 0))
    halo = pl.BlockSpec((None, n_res, nb, d),
                        lambda bi, r, i: (bi, r, jnp.maximum(i * units - 1, 0), 0))
    stat = pl.BlockSpec((None, n_res, rows, LANES), lambda bi, r, i: (bi, r, i, 0))
    stat_shape = jax.ShapeDtypeStruct((b, dil, length, LANES), F32)
    return pl.pallas_call(
        _attn_kernel,
        grid=(b, dil // n_res, length // rows),
        in_specs=[main, main, main, halo, halo, _const_spec(bias.shape)],
        out_specs=[main, stat, stat],
        out_shape=[jax.ShapeDtypeStruct(q.shape, BF16), stat_shape, stat_shape],
        compiler_params=_params(3),
        name=f"attn_d{dil}",
    )(q, k, v, k, v, bias)


def _merge_kernel(x_ref, *refs):
    n_dil = len(DILATIONS)
    branch_refs = [refs[3 * di:3 * di + 3] for di in range(n_dil)]
    e_ref, wo_ref, out_ref, o_scr, st_scr = refs[3 * n_dil:]
    ts = x_ref.shape[1]
    n_slabs = D_MODEL // LANES

    def token_order(ref, dil, scr, slot, cols=slice(None)):
        if dil == 1:
            return ref[0, :, cols].astype(F32)
        for r in range(dil):
            scr[slot, pl.ds(r, ts // dil, stride=dil), :] = ref[r, :, cols].astype(F32)
        return scr[slot]

    ms, ls = [], []
    for di, dil in enumerate(DILATIONS):
        ms.append(token_order(branch_refs[di][1], dil, st_scr, 2 * di))
        ls.append(token_order(branch_refs[di][2], dil, st_scr, 2 * di + 1))
    m_max = functools.reduce(jnp.maximum, ms)
    ws = [l * jnp.exp2(m - m_max) for m, l in zip(ms, ls)]
    total = functools.reduce(jnp.add, ws)
    c_wides = []
    for w in ws:
        c = w / total
        c_hi = c.astype(BF16)
        c_lo = (c - c_hi.astype(F32)).astype(BF16)
        c_wides.append(jnp.dot(jnp.concatenate([c_hi, c_lo], axis=-1), e_ref[...],
                               preferred_element_type=F32))
    slabs = []
    for hp in range(n_slabs):
        cols = slice(hp * LANES, (hp + 1) * LANES)
        merged = None
        for di, dil in enumerate(DILATIONS):
            o = token_order(branch_refs[di][0], dil, o_scr, di * n_slabs + hp, cols)
            term = c_wides[di][:, cols] * o
            merged = term if merged is None else merged + term
        slabs.append(merged.astype(BF16))
    merged = jnp.concatenate(slabs, axis=-1)
    out_ref[0] = x_ref[0] + jnp.dot(merged, wo_ref[...], preferred_element_type=F32)


def _merge_out_proj(x, branches, expand, w_o):
    b, s, d = x.shape
    ts = ROW_TILE
    row_spec = pl.BlockSpec((1, ts, d), lambda i, j: (i, j, 0))
    in_specs, args = [row_spec], [x]
    for dil, (o, m, l) in zip(DILATIONS, branches):
        in_specs.append(pl.BlockSpec((None, dil, ts // dil, d), lambda i, j: (i, 0, j, 0)))
        in_specs += [pl.BlockSpec((None, dil, ts // dil, LANES), lambda i, j: (i, 0, j, 0))] * 2
        args += [o, m, l]
    n_dil = len(DILATIONS)
    return pl.pallas_call(
        _merge_kernel,
        grid=(b, s // ts),
        in_specs=in_specs + [_const_spec(expand.shape), _const_spec((d, d))],
        out_specs=row_spec,
        out_shape=jax.ShapeDtypeStruct(x.shape, F32),
        scratch_shapes=[pltpu.VMEM((n_dil * d // LANES, ts, LANES), F32),
                        pltpu.VMEM((2 * n_dil, ts, LANES), F32)],
        compiler_params=_params(2),
        name="merge_out_proj",
    )(x, *args[1:], expand, w_o)


def _t5_causal_bucket(dist):
    max_exact = N_REL_BUCKETS // 2
    dd = jnp.maximum(dist, 1).astype(F32)
    large = max_exact + (jnp.log(dd / max_exact) / math.log(REL_MAX_DISTANCE / max_exact)
                         * (N_REL_BUCKETS - max_exact)).astype(jnp.int32)
    large = jnp.minimum(large, N_REL_BUCKETS - 1)
    return jnp.where(dist < max_exact, dist, large)


def _bias_tables(rel_table, dil):
    n = ATT_BLOCK
    n_heads = rel_table.shape[1]
    by_offset = LOG2E * rel_table.astype(F32)[_t5_causal_bucket((n - jnp.arange(n + 1)) * dil)].T
    period = 3 * n + 1
    row = jnp.concatenate(
        [by_offset, jnp.full((n_heads, period - (n + 1)), MASK_VALUE, F32)], axis=1)
    band = jnp.tile(row, (1, n))[:, :n * (period - 1)].reshape(n_heads, n, period - 1)
    regular = band[:, :, :2 * n]
    start = jnp.where(jnp.arange(2 * n) >= n, regular, MASK_VALUE)
    return jnp.stack([regular, start])


def _head_expand_matrix():
    row = jnp.arange(2 * LANES)[:, None] % LANES
    col = jnp.arange(D_MODEL)[None, :] // HEAD_DIM
    return (row == col).astype(BF16)


def kernel(x, rel_bias, even_norm, even_w_in, even_conv_w, even_pool_w, even_pool_scale,
           even_w_out, odd_norm, odd_w_qkv, odd_q_norm, odd_k_norm, odd_w_o, ffn_norm,
           ffn_w_up, ffn_conv_w, ffn_conv_b, ffn_w_down):
    def ffn(t, layer):
        return _ffn(t, ffn_norm[layer][None], ffn_w_up[layer].astype(BF16), ffn_conv_w[layer],
                    ffn_conv_b[layer][None], ffn_w_down[layer].astype(BF16))

    x = _mixer(x, even_norm[0][None], even_w_in[0].astype(BF16), even_conv_w[0],
               even_pool_w[0].astype(BF16), even_pool_scale[0][None],
               even_w_out[0].astype(BF16))
    x = ffn(x, 0)

    gq_pair = jnp.tile(odd_q_norm[0], 2)[None]
    gk_pair = jnp.tile(odd_k_norm[0], 2)[None]
    qkv = _qkv(x, odd_norm[0][None], odd_w_qkv[0].astype(BF16), gq_pair, gk_pair)
    branches = [_attn_branch(q, k, v, _bias_tables(rel_bias, dil))
                for dil, (q, k, v) in zip(DILATIONS, qkv)]
    x = _merge_out_proj(x, branches, _head_expand_matrix(), odd_w_o[0].astype(BF16))
    x = ffn(x, 1)
    return x
```

```python
import functools
import math

import jax
import jax.numpy as jnp
from jax import lax
from jax.experimental import pallas as pl
from jax.experimental.pallas import tpu as pltpu

D_MODEL = 1024
CONV_WIDTH = 3
A_WIDTH = 512
B_WIDTH = 512
POOL_WINDOWS = (2, 4, 8, 16)
POOL_GROUP = 128
EVEN_IN = 3 * A_WIDTH + B_WIDTH
HEAD_DIM = 64
N_HEADS = 16
DILATED_PAIRS = ((128, 1), (512, 4), (2048, 16))
DILATIONS = tuple(dil for _, dil in DILATED_PAIRS)
N_REL_BUCKETS = 32
REL_MAX_DISTANCE = 2048
D_FF = 2816
EPS = 1e-6
MASK_VALUE = -1e30

BF16 = jnp.bfloat16
F32 = jnp.float32

LANES = 128
ATT_BLOCK = 128
ATT_UNITS_PER_STEP = 8
LOG2E = math.log2(math.e)
ROW_TILE = 512
FFN_ROW_TILE = 1024
POOL_HALO = 16
CONV_HALO = 8
FF_CHUNK = 256
QKV_CHUNK = 256
MIX_CHUNK = 256
SHIFT_BASE = 8
VMEM_LIMIT = 56 * 1024 * 1024


def _const_spec(shape):
    nd = len(shape)
    return pl.BlockSpec(shape, lambda *_: (0,) * nd, pipeline_mode=pl.Buffered(1))


def _layer_spec(shape, layer):
    nd = len(shape)
    return pl.BlockSpec((None,) + tuple(shape), lambda *_: (layer,) + (0,) * nd,
                        pipeline_mode=pl.Buffered(1))


def _params(n_axes):
    return pltpu.CompilerParams(
        dimension_semantics=("arbitrary",) * n_axes, vmem_limit_bytes=VMEM_LIMIT)


def _rmsnorm(x, g):
    ms = jnp.mean(x * x, axis=-1, keepdims=True)
    return x * lax.rsqrt(ms + EPS) * g


def _stage_rows(scr, slot, base, tile, prev):
    n_prev = prev.shape[0]
    scr[slot, pl.ds(base - 2 * n_prev, n_prev, stride=2), :] = prev
    scr[slot, pl.ds(base, tile.shape[0], stride=2), :] = tile


def _rows_back(scr, slot, base, n_rows, k):
    return scr[slot, pl.ds(base - 2 * k, n_rows, stride=2), :]


def _mixer_kernel(x_ref, g_ref, win_ref, cw_ref, pw_ref, ps_ref, wout_ref, o_ref,
                  y_buf, carry_a, carry_p, scr):
    ts = x_ref.shape[1]
    s = pl.program_id(1)
    base = 2 * POOL_HALO

    @pl.when(s == 0)
    def _():
        carry_a[...] = jnp.zeros(carry_a.shape, F32)
        carry_p[...] = jnp.zeros(carry_p.shape, F32)

    x = x_ref[0]
    xn = _rmsnorm(x, g_ref[...]).astype(BF16)
    slabs = MIX_CHUNK // LANES

    def proj(col0):
        return jnp.dot(xn, win_ref[:, col0:col0 + MIX_CHUNK], preferred_element_type=F32)

    for c in range(A_WIDTH // MIX_CHUNK):
        h = proj(c * MIX_CHUNK)
        gate_b = proj(A_WIDTH + c * MIX_CHUNK)
        gate_c = proj(2 * A_WIDTH + c * MIX_CHUNK)
        ch = gate_c * h
        for sl in range(slabs):
            slot = c * slabs + sl
            cols = slice(slot * LANES, (slot + 1) * LANES)
            ch_sl = ch[:, sl * LANES:(sl + 1) * LANES]
            _stage_rows(scr, slot, base, ch_sl, carry_a[CONV_HALO - 2:, cols])
            carry_a[:, cols] = ch_sl[ts - CONV_HALO:, :]
            cw = cw_ref[:, cols]
            conv = (cw[0:1] * _rows_back(scr, slot, base, ts, 2)
                    + cw[1:2] * _rows_back(scr, slot, base, ts, 1) + cw[2:3] * ch_sl)
            y_buf[:, cols] = (gate_b[:, sl * LANES:(sl + 1) * LANES] * conv).astype(BF16)

    pos = s * ts + lax.broadcasted_iota(jnp.int32, (ts, 1), 0)
    for c in range(B_WIDTH // MIX_CHUNK):
        pin = proj(3 * A_WIDTH + c * MIX_CHUNK)
        for sl in range(slabs):
            g = c * slabs + sl
            k = POOL_WINDOWS[g]
            slot = A_WIDTH // LANES + g
            cols = slice(g * POOL_GROUP, (g + 1) * POOL_GROUP)
            cur = pin[:, sl * LANES:(sl + 1) * LANES]
            _stage_rows(scr, slot, base, cur, carry_p[:, cols])
            carry_p[:, cols] = cur[ts - POOL_HALO:, :]
            acc = cur
            for j in range(1, k):
                acc = acc + _rows_back(scr, slot, base, ts, j)
            cnt = jnp.minimum(pos + 1, k).astype(F32)
            pooled = acc / cnt - cur
            yb = jnp.dot(pooled.astype(BF16), pw_ref[g], preferred_element_type=F32)
            y_buf[:, A_WIDTH + g * POOL_GROUP:A_WIDTH + (g + 1) * POOL_GROUP] = (
                yb * ps_ref[:, cols]).astype(BF16)

    o_ref[0] = x + jnp.dot(y_buf[...], wout_ref[...], preferred_element_type=F32)


def _mixer(x, g, w_in, conv_w, pool_w, pool_scale, w_out):
    b, s, d = x.shape
    ts = FFN_ROW_TILE
    row_spec = pl.BlockSpec((1, ts, d), lambda i, j: (i, j, 0))
    return pl.pallas_call(
        _mixer_kernel,
        grid=(b, s // ts),
        in_specs=[row_spec, _const_spec((1, d)), _const_spec((d, EVEN_IN)),
                  _const_spec((CONV_WIDTH, A_WIDTH)),
                  _const_spec((len(POOL_WINDOWS), POOL_GROUP, POOL_GROUP)),
                  _const_spec((1, B_WIDTH)), _const_spec((d, d))],
        out_specs=row_spec,
        out_shape=jax.ShapeDtypeStruct(x.shape, F32),
        scratch_shapes=[pltpu.VMEM((ts, d), BF16),
                        pltpu.VMEM((CONV_HALO, A_WIDTH), F32),
                        pltpu.VMEM((POOL_HALO, B_WIDTH), F32),
                        pltpu.VMEM(((A_WIDTH + B_WIDTH) // LANES, 2 * (POOL_HALO + ts), LANES),
                                   F32)],
        compiler_params=_params(2),
        name="mixer",
    )(x, g, w_in, conv_w, pool_w, pool_scale, w_out)


def _ffn_kernel(x_ref, g_ref, wup_ref, cw_ref, cb_ref, wdn_ref, o_ref, h_buf, carry, u_scr):
    ts = x_ref.shape[1]
    s = pl.program_id(1)

    @pl.when(s == 0)
    def _():
        carry[...] = jnp.zeros(carry.shape, F32)

    x = x_ref[0]
    xn = _rmsnorm(x, g_ref[...]).astype(BF16)
    slabs = FF_CHUNK // LANES
    for j in range(D_FF // FF_CHUNK):
        halves = []
        for half in range(2):
            c0 = half * D_FF + j * FF_CHUNK
            u = jnp.dot(xn, wup_ref[:, c0:c0 + FF_CHUNK], preferred_element_type=F32)
            parts = []
            for sl in range(slabs):
                cols = slice(c0 + sl * LANES, c0 + (sl + 1) * LANES)
                slot = ((j % 2) * 2 + half) * slabs + sl
                u_sl = u[:, sl * LANES:(sl + 1) * LANES]
                _stage_rows(u_scr, slot, SHIFT_BASE, u_sl, carry[CONV_HALO - 2:, cols])
                carry[:, cols] = u_sl[ts - CONV_HALO:ts, :]
                w = cw_ref[:, cols]
                parts.append(w[0:1] * _rows_back(u_scr, slot, SHIFT_BASE, ts, 2)
                             + w[1:2] * _rows_back(u_scr, slot, SHIFT_BASE, ts, 1)
                             + w[2:3] * u_sl + cb_ref[:, cols])
            halves.append(jnp.concatenate(parts, axis=-1))
        gate, up = halves
        h_buf[:, j * FF_CHUNK:(j + 1) * FF_CHUNK] = (gate * jax.nn.sigmoid(gate) * up).astype(BF16)
    o_ref[0] = x + jnp.dot(h_buf[...], wdn_ref[...], preferred_element_type=F32)


def _ffn(x, layer, g, w_up, conv_w, conv_b, w_down):
    b, s, d = x.shape
    ts = FFN_ROW_TILE
    row_spec = pl.BlockSpec((1, ts, d), lambda i, j: (i, j, 0))
    return pl.pallas_call(
        _ffn_kernel,
        grid=(b, s // ts),
        in_specs=[row_spec, _layer_spec((1, d), layer), _layer_spec((d, 2 * D_FF), layer),
                  _layer_spec((CONV_WIDTH, 2 * D_FF), layer),
                  _layer_spec((1, 2 * D_FF), layer), _layer_spec((D_FF, d), layer)],
        out_specs=row_spec,
        out_shape=jax.ShapeDtypeStruct(x.shape, F32),
        scratch_shapes=[pltpu.VMEM((ts, D_FF), BF16),
                        pltpu.VMEM((CONV_HALO, 2 * D_FF), F32),
                        pltpu.VMEM((4 * FF_CHUNK // LANES, SHIFT_BASE + 2 * ts, LANES), F32)],
        compiler_params=_params(2),
        name="ffn",
    )(x, g, w_up, conv_w, conv_b, w_down)


def _head_pair_norm(t, g_pair, scale):
    lane = lax.broadcasted_iota(jnp.int32, t.shape, 1)
    lo = lane < HEAD_DIM
    sq = t * t
    s_lo = jnp.sum(jnp.where(lo, sq, 0.0), axis=-1, keepdims=True)
    s_hi = jnp.sum(jnp.where(lo, 0.0, sq), axis=-1, keepdims=True)
    r = lax.rsqrt(jnp.where(lo, s_lo, s_hi) * (1.0 / HEAD_DIM) + EPS)
    return t * r * (g_pair * scale)


def _qkv_kernel(x_ref, g_ref, w_ref, gq_ref, gk_ref, *refs):
    assert DILATIONS == (1, 4, 16)
    n_dil = len(DILATIONS)
    out1, out4, out16 = [refs[3 * di:3 * di + 3] for di in range(n_dil)]
    stage1, stage4 = refs[3 * n_dil:]
    d = D_MODEL
    ts = x_ref.shape[1]
    x = x_ref[0]
    xn = _rmsnorm(x, g_ref[...]).astype(BF16)
    slabs = QKV_CHUNK // LANES
    for c in range(3 * d // QKV_CHUNK):
        res = jnp.dot(xn, w_ref[:, c * QKV_CHUNK:(c + 1) * QKV_CHUNK],
                      preferred_element_type=F32)
        for sl in range(slabs):
            slot = c * slabs + sl
            which, hp = divmod(slot, d // LANES)
            cols = slice(hp * LANES, (hp + 1) * LANES)
            slab = res[:, sl * LANES:(sl + 1) * LANES]
            if which == 0:
                slab = _head_pair_norm(slab, gq_ref[...], HEAD_DIM ** -0.5 * LOG2E)
            elif which == 1:
                slab = _head_pair_norm(slab, gk_ref[...], 1.0)
            out1[which][0, :, cols] = slab.astype(BF16)
            stage1[slot] = slab
            for r4 in range(4):
                p4 = stage1[slot, pl.ds(r4, ts // 4, stride=4), :]
                out4[which][r4, :, cols] = p4.astype(BF16)
                stage4[4 * slot + r4] = p4
                for c4 in range(4):
                    p16 = stage4[4 * slot + r4, pl.ds(c4, ts // 16, stride=4), :]
                    out16[which][r4 + 4 * c4, :, cols] = p16.astype(BF16)


def _qkv(x, g, w_qkv, gq_pair, gk_pair):
    b, s, d = x.shape
    ts = ROW_TILE
    row_spec = pl.BlockSpec((1, ts, d), lambda i, j: (i, j, 0))
    out_specs, out_shapes = [], []
    for dil in DILATIONS:
        out_specs += [pl.BlockSpec((None, dil, ts // dil, d), lambda i, j: (i, 0, j, 0))] * 3
        out_shapes += [jax.ShapeDtypeStruct((b, dil, s // dil, d), BF16)] * 3
    outs = pl.pallas_call(
        _qkv_kernel,
        grid=(b, s // ts),
        in_specs=[row_spec, _const_spec((1, d)), _const_spec((d, 3 * d)),
                  _const_spec((1, LANES)), _const_spec((1, LANES))],
        out_specs=out_specs,
        out_shape=out_shapes,
        scratch_shapes=[pltpu.VMEM((3 * d // LANES, ts, LANES), F32),
                        pltpu.VMEM((4 * 3 * d // LANES, ts // 4, LANES), F32)],
        compiler_params=_params(2),
        name="qkv",
    )(x, g, w_qkv, gq_pair, gk_pair)
    return [outs[3 * di:3 * di + 3] for di in range(len(DILATIONS))]


def _stat_lane(h):
    return HEAD_DIM + h // 2 if h % 2 == 0 else h // 2


def _attn_kernel(q_ref, k_ref, v_ref, kh_ref, vh_ref, bias_ref, o_ref, m_ref, l_ref):
    nb = ATT_BLOCK
    n_res, rows, _ = q_ref.shape
    first_step = pl.program_id(2) == 0
    first_table = jnp.where(first_step, 1, 0)

    lo = lax.broadcasted_iota(jnp.int32, (nb, LANES), 1) < HEAD_DIM
    lo_kv = lax.broadcasted_iota(jnp.int32, (2 * nb, LANES), 1) < HEAD_DIM

    for res in range(n_res):
        for u in range(rows // nb):
            cur = slice(u * nb, (u + 1) * nb)
            m_ref[res, cur, :] = jnp.zeros((nb, LANES), F32)
            l_ref[res, cur, :] = jnp.ones((nb, LANES), F32)
            for hp in range(N_HEADS // 2):
                cols = slice(hp * LANES, (hp + 1) * LANES)
                q_pair = q_ref[res, cur, cols]
                if u == 0:
                    k_win = jnp.concatenate([kh_ref[res, :, cols], k_ref[res, cur, cols]], axis=0)
                    v_win = jnp.concatenate([vh_ref[res, :, cols], v_ref[res, cur, cols]], axis=0)
                    table = first_table
                else:
                    win = slice((u - 1) * nb, (u + 1) * nb)
                    k_win = k_ref[res, win, cols]
                    v_win = v_ref[res, win, cols]
                    table = 0
                outs = []
                for sub in range(2):
                    h = 2 * hp + sub
                    own = lo if sub == 0 else jnp.logical_not(lo)
                    own_kv = lo_kv if sub == 0 else jnp.logical_not(lo_kv)
                    q_h = jnp.where(own, q_pair, jnp.zeros_like(q_pair))
                    sc = lax.dot_general(q_h, k_win, (((1,), (1,)), ((), ())),
                                         preferred_element_type=F32)
                    sc = sc + bias_ref[table, h]
                    m = jnp.max(sc, axis=-1, keepdims=True)
                    p = jnp.exp2(sc - m).astype(BF16)
                    v_h = jnp.where(own_kv, v_win, jnp.ones_like(v_win))
                    pv = jnp.dot(p, v_h, preferred_element_type=F32)
                    outs.append(pv)
                    sl = _stat_lane(h)
                    m_ref[res, cur, sl:sl + 1] = m
                    l_ref[res, cur, sl:sl + 1] = pv[:, sl:sl + 1]
                o_ref[res, cur, cols] = jnp.where(lo, outs[0], outs[1]).astype(BF16)


def _attn_branch(q, k, v, bias):
    b, dil, length, d = q.shape
    nb = ATT_BLOCK
    units = min(ATT_UNITS_PER_STEP, length // nb)
    n_res = min(dil, ATT_UNITS_PER_STEP // units)
    rows = units * nb
    main = pl.BlockSpec((None, n_res, rows, d), lambda bi, r, i: (bi, r, i, 0))
    halo = pl.BlockSpec((None, n_res, nb, d),
                        lambda bi, r, i: (bi, r, jnp.maximum(i * units - 1, 0), 0))
    stat = pl.BlockSpec((None, n_res, rows, LANES), lambda bi, r, i: (bi, r, i, 0))
    stat_shape = jax.ShapeDtypeStruct((b, dil, length, LANES), F32)
    return pl.pallas_call(
        _attn_kernel,
        grid=(b, dil // n_res, length // rows),
        in_specs=[main, main, main, halo, halo, _const_spec(bias.shape)],
        out_specs=[main, stat, stat],
        out_shape=[jax.ShapeDtypeStruct(q.shape, BF16), stat_shape, stat_shape],
        compiler_params=_params(3),
        name=f"attn_d{dil}",
    )(q, k, v, k, v, bias)


def _merge_kernel(x_ref, *refs):
    n_dil = len(DILATIONS)
    branch_refs = [refs[3 * di:3 * di + 3] for di in range(n_dil)]
    e_ref, wo_ref, out_ref, o_scr, st_scr = refs[3 * n_dil:]
    ts = x_ref.shape[1]
    n_slabs = D_MODEL // LANES

    def token_order(ref, dil, scr, slot, cols=slice(None)):
        if dil == 1:
            return ref[0, :, cols].astype(F32)
        for r in range(dil):
            scr[slot, pl.ds(r, ts // dil, stride=dil), :] = ref[r, :, cols].astype(F32)
        return scr[slot]

    ms, ls = [], []
    for di, dil in enumerate(DILATIONS):
        ms.append(token_order(branch_refs[di][1], dil, st_scr, 2 * di))
        ls.append(token_order(branch_refs[di][2], dil, st_scr, 2 * di + 1))
    m_max = functools.reduce(jnp.maximum, ms)
    es = [jnp.exp2(m - m_max) for m in ms]
    total = functools.reduce(jnp.add, [e * l for e, l in zip(es, ls)])
    c_wides = []
    for e in es:
        c = e / total
        c_hi = c.astype(BF16)
        c_lo = (c - c_hi.astype(F32)).astype(BF16)
        c_wides.append(jnp.dot(jnp.concatenate([c_hi, c_lo], axis=-1), e_ref[...],
                               preferred_element_type=F32))
    slabs = []
    for hp in range(n_slabs):
        cols = slice(hp * LANES, (hp + 1) * LANES)
        merged = None
        for di, dil in enumerate(DILATIONS):
            o = token_order(branch_refs[di][0], dil, o_scr, di * n_slabs + hp, cols)
            term = c_wides[di][:, cols] * o
            merged = term if merged is None else merged + term
        slabs.append(merged.astype(BF16))
    merged = jnp.concatenate(slabs, axis=-1)
    out_ref[0] = x_ref[0] + jnp.dot(merged, wo_ref[...], preferred_element_type=F32)


def _merge_out_proj(x, branches, expand, w_o):
    b, s, d = x.shape
    ts = ROW_TILE
    row_spec = pl.BlockSpec((1, ts, d), lambda i, j: (i, j, 0))
    in_specs, args = [row_spec], [x]
    for dil, (o, m, l) in zip(DILATIONS, branches):
        in_specs.append(pl.BlockSpec((None, dil, ts // dil, d), lambda i, j: (i, 0, j, 0)))
        in_specs += [pl.BlockSpec((None, dil, ts // dil, LANES), lambda i, j: (i, 0, j, 0))] * 2
        args += [o, m, l]
    n_dil = len(DILATIONS)
    return pl.pallas_call(
        _merge_kernel,
        grid=(b, s // ts),
        in_specs=in_specs + [_const_spec(expand.shape), _const_spec((d, d))],
        out_specs=row_spec,
        out_shape=jax.ShapeDtypeStruct(x.shape, F32),
        scratch_shapes=[pltpu.VMEM((n_dil * d // LANES, ts, LANES), F32),
                        pltpu.VMEM((2 * n_dil, ts, LANES), F32)],
        compiler_params=_params(2),
        name="merge_out_proj",
    )(x, *args[1:], expand, w_o)


def _t5_causal_bucket(dist):
    max_exact = N_REL_BUCKETS // 2
    dd = jnp.maximum(dist, 1).astype(F32)
    large = max_exact + (jnp.log(dd / max_exact) / math.log(REL_MAX_DISTANCE / max_exact)
                         * (N_REL_BUCKETS - max_exact)).astype(jnp.int32)
    large = jnp.minimum(large, N_REL_BUCKETS - 1)
    return jnp.where(dist < max_exact, dist, large)


def _bias_tables(rel_table, dil):
    n = ATT_BLOCK
    n_heads = rel_table.shape[1]
    by_offset = LOG2E * rel_table.astype(F32)[_t5_causal_bucket((n - jnp.arange(n + 1)) * dil)].T
    period = 3 * n + 1
    row = jnp.concatenate(
        [by_offset, jnp.full((n_heads, period - (n + 1)), MASK_VALUE, F32)], axis=1)
    band = jnp.tile(row, (1, n))[:, :n * (period - 1)].reshape(n_heads, n, period - 1)
    regular = band[:, :, :2 * n]
    start = jnp.where(jnp.arange(2 * n) >= n, regular, MASK_VALUE)
    return jnp.stack([regular, start])


def _head_expand_matrix():
    row = jnp.arange(2 * LANES)[:, None] % LANES
    head = jnp.arange(D_MODEL)[None, :] // HEAD_DIM
    stat_lane = jnp.where(head % 2 == 0, HEAD_DIM + head // 2, head // 2)
    return (row == stat_lane).astype(BF16)


def kernel(x, rel_bias, even_norm, even_w_in, even_conv_w, even_pool_w, even_pool_scale,
           even_w_out, odd_norm, odd_w_qkv, odd_q_norm, odd_k_norm, odd_w_o, ffn_norm,
           ffn_w_up, ffn_conv_w, ffn_conv_b, ffn_w_down):
    ffn_params = (ffn_norm[:, None], ffn_w_up.astype(BF16), ffn_conv_w, ffn_conv_b[:, None],
                  ffn_w_down.astype(BF16))

    def ffn(t, layer):
        return _ffn(t, layer, *ffn_params)

    x = _mixer(x, even_norm[0][None], even_w_in[0].astype(BF16), even_conv_w[0],
               even_pool_w[0].astype(BF16), even_pool_scale[0][None],
               even_w_out[0].astype(BF16))
    x = ffn(x, 0)

    gq_pair = jnp.tile(odd_q_norm[0], 2)[None]
    gk_pair = jnp.tile(odd_k_norm[0], 2)[None]
    qkv = _qkv(x, odd_norm[0][None], odd_w_qkv[0].astype(BF16), gq_pair, gk_pair)
    branches = [_attn_branch(q, k, v, _bias_tables(rel_bias, dil))
                for dil, (q, k, v) in zip(DILATIONS, qkv)]
    x = _merge_out_proj(x, branches, _head_expand_matrix(), odd_w_o[0].astype(BF16))
    x = ffn(x, 1)
    return x
```

```python
import functools
import math

import jax
import jax.numpy as jnp
from jax import lax
from jax.experimental import pallas as pl
from jax.experimental.pallas import tpu as pltpu

D_MODEL = 1024
CONV_WIDTH = 3
A_WIDTH = 512
B_WIDTH = 512
POOL_WINDOWS = (2, 4, 8, 16)
POOL_GROUP = 128
EVEN_IN = 3 * A_WIDTH + B_WIDTH
HEAD_DIM = 64
N_HEADS = 16
DILATED_PAIRS = ((128, 1), (512, 4), (2048, 16))
DILATIONS = tuple(dil for _, dil in DILATED_PAIRS)
N_REL_BUCKETS = 32
REL_MAX_DISTANCE = 2048
D_FF = 2816
EPS = 1e-6
MASK_VALUE = -1e30

BF16 = jnp.bfloat16
F32 = jnp.float32

LANES = 128
ATT_BLOCK = 128
ATT_UNITS_PER_STEP = 8
LOG2E = math.log2(math.e)
ROW_TILE = 512
FFN_ROW_TILE = 1024
POOL_HALO = 16
CONV_HALO = 8
FF_CHUNK = 256
QKV_CHUNK = 256
MIX_CHUNK = 256
SHIFT_BASE = 8
VMEM_LIMIT = 56 * 1024 * 1024


def _const_spec(shape):
    nd = len(shape)
    return pl.BlockSpec(shape, lambda *_: (0,) * nd, pipeline_mode=pl.Buffered(1))


def _layer_spec(shape, layer):
    nd = len(shape)
    return pl.BlockSpec((None,) + tuple(shape), lambda *_: (layer,) + (0,) * nd,
                        pipeline_mode=pl.Buffered(1))


def _params(n_axes):
    return pltpu.CompilerParams(
        dimension_semantics=("arbitrary",) * n_axes, vmem_limit_bytes=VMEM_LIMIT)


def _rmsnorm(x, g):
    ms = jnp.mean(x * x, axis=-1, keepdims=True)
    return x * lax.rsqrt(ms + EPS) * g


def _stage_rows(scr, slot, base, tile, prev):
    n_prev = prev.shape[0]
    scr[slot, pl.ds(base - 2 * n_prev, n_prev, stride=2), :] = prev
    scr[slot, pl.ds(base, tile.shape[0], stride=2), :] = tile


def _rows_back(scr, slot, base, n_rows, k):
    return scr[slot, pl.ds(base - 2 * k, n_rows, stride=2), :]


def _mixer_kernel(x_ref, g_ref, win_ref, cw_ref, pw_ref, ps_ref, wout_ref, o_ref,
                  y_buf, carry_a, carry_p, scr):
    ts = x_ref.shape[1]
    s = pl.program_id(1)
    base = 2 * POOL_HALO

    @pl.when(s == 0)
    def _():
        carry_a[...] = jnp.zeros(carry_a.shape, F32)
        carry_p[...] = jnp.zeros(carry_p.shape, F32)

    x = x_ref[0]
    xn = _rmsnorm(x, g_ref[...]).astype(BF16)
    slabs = MIX_CHUNK // LANES

    def proj(col0):
        return jnp.dot(xn, win_ref[:, col0:col0 + MIX_CHUNK], preferred_element_type=F32)

    for c in range(A_WIDTH // MIX_CHUNK):
        h = proj(c * MIX_CHUNK)
        gate_b = proj(A_WIDTH + c * MIX_CHUNK)
        gate_c = proj(2 * A_WIDTH + c * MIX_CHUNK)
        ch = gate_c * h
        for sl in range(slabs):
            slot = c * slabs + sl
            cols = slice(slot * LANES, (slot + 1) * LANES)
            ch_sl = ch[:, sl * LANES:(sl + 1) * LANES]
            _stage_rows(scr, slot, base, ch_sl, carry_a[CONV_HALO - 2:, cols])
            carry_a[:, cols] = ch_sl[ts - CONV_HALO:, :]
            cw = cw_ref[:, cols]
            conv = (cw[0:1] * _rows_back(scr, slot, base, ts, 2)
                    + cw[1:2] * _rows_back(scr, slot, base, ts, 1) + cw[2:3] * ch_sl)
            y_buf[:, cols] = (gate_b[:, sl * LANES:(sl + 1) * LANES] * conv).astype(BF16)

    pos = s * ts + lax.broadcasted_iota(jnp.int32, (ts, 1), 0)
    for c in range(B_WIDTH // MIX_CHUNK):
        pin = proj(3 * A_WIDTH + c * MIX_CHUNK)
        for sl in range(slabs):
            g = c * slabs + sl
            k = POOL_WINDOWS[g]
            slot = A_WIDTH // LANES + g
            cols = slice(g * POOL_GROUP, (g + 1) * POOL_GROUP)
            cur = pin[:, sl * LANES:(sl + 1) * LANES]
            _stage_rows(scr, slot, base, cur, carry_p[:, cols])
            carry_p[:, cols] = cur[ts - POOL_HALO:, :]
            acc = cur
            for j in range(1, k):
                acc = acc + _rows_back(scr, slot, base, ts, j)
            cnt = jnp.minimum(pos + 1, k).astype(F32)
            pooled = acc / cnt - cur
            yb = jnp.dot(pooled.astype(BF16), pw_ref[g], preferred_element_type=F32)
            y_buf[:, A_WIDTH + g * POOL_GROUP:A_WIDTH + (g + 1) * POOL_GROUP] = (
                yb * ps_ref[:, cols]).astype(BF16)

    o_ref[0] = x + jnp.dot(y_buf[...], wout_ref[...], preferred_element_type=F32)


def _mixer(x, g, w_in, conv_w, pool_w, pool_scale, w_out):
    b, s, d = x.shape
    ts = FFN_ROW_TILE
    row_spec = pl.BlockSpec((1, ts, d), lambda i, j: (i, j, 0))
    return pl.pallas_call(
        _mixer_kernel,
        grid=(b, s // ts),
        in_specs=[row_spec, _const_spec((1, d)), _const_spec((d, EVEN_IN)),
                  _const_spec((CONV_WIDTH, A_WIDTH)),
                  _const_spec((len(POOL_WINDOWS), POOL_GROUP, POOL_GROUP)),
                  _const_spec((1, B_WIDTH)), _const_spec((d, d))],
        out_specs=row_spec,
        out_shape=jax.ShapeDtypeStruct(x.shape, F32),
        scratch_shapes=[pltpu.VMEM((ts, d), BF16),
                        pltpu.VMEM((CONV_HALO, A_WIDTH), F32),
                        pltpu.VMEM((POOL_HALO, B_WIDTH), F32),
                        pltpu.VMEM(((A_WIDTH + B_WIDTH) // LANES, 2 * (POOL_HALO + ts), LANES),
                                   F32)],
        compiler_params=_params(2),
        name="mixer",
    )(x, g, w_in, conv_w, pool_w, pool_scale, w_out)


def _ffn_kernel(x_ref, g_ref, wup_ref, cw_ref, cb_ref, wdn_ref, o_ref, h_buf, carry, u_scr):
    ts = x_ref.shape[1]
    s = pl.program_id(1)

    @pl.when(s == 0)
    def _():
        carry[...] = jnp.zeros(carry.shape, F32)

    x = x_ref[0]
    xn = _rmsnorm(x, g_ref[...]).astype(BF16)
    slabs = FF_CHUNK // LANES
    for j in range(D_FF // FF_CHUNK):
        halves = []
        for half in range(2):
            c0 = half * D_FF + j * FF_CHUNK
            u = jnp.dot(xn, wup_ref[:, c0:c0 + FF_CHUNK], preferred_element_type=F32)
            parts = []
            for sl in range(slabs):
                cols = slice(c0 + sl * LANES, c0 + (sl + 1) * LANES)
                slot = ((j % 2) * 2 + half) * slabs + sl
                u_sl = u[:, sl * LANES:(sl + 1) * LANES]
                _stage_rows(u_scr, slot, SHIFT_BASE, u_sl, carry[CONV_HALO - 2:, cols])
                carry[:, cols] = u_sl[ts - CONV_HALO:ts, :]
                w = cw_ref[:, cols]
                parts.append(w[0:1] * _rows_back(u_scr, slot, SHIFT_BASE, ts, 2)
                             + w[1:2] * _rows_back(u_scr, slot, SHIFT_BASE, ts, 1)
                             + w[2:3] * u_sl + cb_ref[:, cols])
            halves.append(jnp.concatenate(parts, axis=-1))
        gate, up = halves
        h_buf[:, j * FF_CHUNK:(j + 1) * FF_CHUNK] = (gate * jax.nn.sigmoid(gate) * up).astype(BF16)
    o_ref[0] = x + jnp.dot(h_buf[...], wdn_ref[...], preferred_element_type=F32)


def _ffn(x, layer, g, w_up, conv_w, conv_b, w_down):
    b, s, d = x.shape
    ts = FFN_ROW_TILE
    row_spec = pl.BlockSpec((1, ts, d), lambda i, j: (i, j, 0))
    return pl.pallas_call(
        _ffn_kernel,
        grid=(b, s // ts),
        in_specs=[row_spec, _layer_spec((1, d), layer), _layer_spec((d, 2 * D_FF), layer),
                  _layer_spec((CONV_WIDTH, 2 * D_FF), layer),
                  _layer_spec((1, 2 * D_FF), layer), _layer_spec((D_FF, d), layer)],
        out_specs=row_spec,
        out_shape=jax.ShapeDtypeStruct(x.shape, F32),
        scratch_shapes=[pltpu.VMEM((ts, D_FF), BF16),
                        pltpu.VMEM((CONV_HALO, 2 * D_FF), F32),
                        pltpu.VMEM((4 * FF_CHUNK // LANES, SHIFT_BASE + 2 * ts, LANES), F32)],
        compiler_params=_params(2),
        name="ffn",
    )(x, g, w_up, conv_w, conv_b, w_down)


def _head_pair_norm(t, g_pair, scale):
    lane = lax.broadcasted_iota(jnp.int32, t.shape, 1)
    lo = lane < HEAD_DIM
    sq = t * t
    s_lo = jnp.sum(jnp.where(lo, sq, 0.0), axis=-1, keepdims=True)
    s_hi = jnp.sum(jnp.where(lo, 0.0, sq), axis=-1, keepdims=True)
    r = lax.rsqrt(jnp.where(lo, s_lo, s_hi) * (1.0 / HEAD_DIM) + EPS)
    return t * r * (g_pair * scale)


def _qkv_kernel(x_ref, g_ref, w_ref, gq_ref, gk_ref, *refs):
    assert DILATIONS == (1, 4, 16)
    out4, out16 = refs[0:3], refs[3:6]
    stage1, stage4 = refs[6:]
    d = D_MODEL
    ts = x_ref.shape[1]
    x = x_ref[0]
    xn = _rmsnorm(x, g_ref[...]).astype(BF16)
    slabs = QKV_CHUNK // LANES
    for c in range(3 * d // QKV_CHUNK):
        res = jnp.dot(xn, w_ref[:, c * QKV_CHUNK:(c + 1) * QKV_CHUNK],
                      preferred_element_type=F32)
        for sl in range(slabs):
            slot = c * slabs + sl
            which, hp = divmod(slot, d // LANES)
            cols = slice(hp * LANES, (hp + 1) * LANES)
            slab = res[:, sl * LANES:(sl + 1) * LANES]
            if which == 0:
                slab = _head_pair_norm(slab, gq_ref[...], HEAD_DIM ** -0.5 * LOG2E)
            elif which == 1:
                slab = _head_pair_norm(slab, gk_ref[...], 1.0)
            stage1[slot] = slab
            for r4 in range(4):
                p4 = stage1[slot, pl.ds(r4, ts // 4, stride=4), :]
                out4[which][r4, :, cols] = p4.astype(BF16)
                stage4[4 * slot + r4] = p4
                for c4 in range(4):
                    p16 = stage4[4 * slot + r4, pl.ds(c4, ts // 16, stride=4), :]
                    out16[which][4 * r4 + c4, :, cols] = p16.astype(BF16)


def _qkv(x, g, w_qkv, gq_pair, gk_pair):
    b, s, d = x.shape
    ts = ROW_TILE
    row_spec = pl.BlockSpec((1, ts, d), lambda i, j: (i, j, 0))
    out_specs, out_shapes = [], []
    for planes in (4, 16):
        out_specs += [pl.BlockSpec((None, planes, ts // planes, d), lambda i, j: (i, 0, j, 0))] * 3
        out_shapes += [jax.ShapeDtypeStruct((b, planes, s // planes, d), BF16)] * 3
    outs = pl.pallas_call(
        _qkv_kernel,
        grid=(b, s // ts),
        in_specs=[row_spec, _const_spec((1, d)), _const_spec((d, 3 * d)),
                  _const_spec((1, LANES)), _const_spec((1, LANES))],
        out_specs=out_specs,
        out_shape=out_shapes,
        scratch_shapes=[pltpu.VMEM((3 * d // LANES, ts, LANES), F32),
                        pltpu.VMEM((4 * 3 * d // LANES, ts // 4, LANES), F32)],
        compiler_params=_params(2),
        name="qkv",
    )(x, g, w_qkv, gq_pair, gk_pair)
    return outs[0:3], outs[3:6]


def _stat_lane(h):
    return HEAD_DIM + h // 2 if h % 2 == 0 else h // 2


def _rows_cat(parts):
    return parts[0] if len(parts) == 1 else jnp.concatenate(parts, axis=0)


def _attn_kernel(q_ref, k_ref, v_ref, kh_ref, vh_ref, bias_ref, o_ref, m_ref, l_ref, *scratch,
                 split, interleave):
    nb = ATT_BLOCK
    n_planes, rows, _ = q_ref.shape
    piece = nb // split
    first_step = pl.program_id(2) == 0
    first_table = jnp.where(first_step, 1, 0)
    if interleave:
        o_stage, m_tiles, l_tiles = scratch

    lo = lax.broadcasted_iota(jnp.int32, (nb, LANES), 1) < HEAD_DIM
    lo_kv = lax.broadcasted_iota(jnp.int32, (2 * nb, LANES), 1) < HEAD_DIM

    for grp in range(n_planes // split):
        planes = range(grp * split, (grp + 1) * split)
        for u in range(rows // piece):
            cur = slice(u * piece, (u + 1) * piece)
            tile = grp * (rows // piece) + u
            out_rows = pl.ds(interleave * nb * u + grp, nb, stride=interleave) if interleave else None

            def block_rows(ref, cols):
                return _rows_cat([ref[p, cur, cols] for p in planes])

            def window_rows(ref, halo_ref, cols):
                parts = []
                for p in planes:
                    if u == 0:
                        parts += [halo_ref[p, :, cols], ref[p, cur, cols]]
                    else:
                        parts.append(ref[p, (u - 1) * piece:(u + 1) * piece, cols])
                return _rows_cat(parts)

            def put_stat(ref, tiles, lane_cols, value):
                if interleave:
                    tiles[tile, :, lane_cols] = value
                else:
                    for n, p in enumerate(planes):
                        ref[p, cur, lane_cols] = value[n * piece:(n + 1) * piece]

            put_stat(m_ref, m_tiles if interleave else None, slice(None), jnp.zeros((nb, LANES), F32))
            put_stat(l_ref, l_tiles if interleave else None, slice(None), jnp.ones((nb, LANES), F32))
            for hp in range(N_HEADS // 2):
                cols = slice(hp * LANES, (hp + 1) * LANES)
                q_pair = block_rows(q_ref, cols)
                k_win = window_rows(k_ref, kh_ref, cols)
                v_win = window_rows(v_ref, vh_ref, cols)
                table = first_table if u == 0 else 0
                outs = []
                for sub in range(2):
                    h = 2 * hp + sub
                    own = lo if sub == 0 else jnp.logical_not(lo)
                    own_kv = lo_kv if sub == 0 else jnp.logical_not(lo_kv)
                    q_h = jnp.where(own, q_pair, jnp.zeros_like(q_pair))
                    sc = lax.dot_general(q_h, k_win, (((1,), (1,)), ((), ())),
                                         preferred_element_type=F32)
                    sc = sc + bias_ref[table, h]
                    m = jnp.max(sc, axis=-1, keepdims=True)
                    p = jnp.exp2(sc - m).astype(BF16)
                    v_h = jnp.where(own_kv, v_win, jnp.ones_like(v_win))
                    pv = jnp.dot(p, v_h, preferred_element_type=F32)
                    outs.append(pv)
                    sl = slice(_stat_lane(h), _stat_lane(h) + 1)
                    put_stat(m_ref, m_tiles if interleave else None, sl, m)
                    put_stat(l_ref, l_tiles if interleave else None, sl, pv[:, sl])
                o_pair = jnp.where(lo, outs[0], outs[1])
                if interleave:
                    o_stage[hp, out_rows, :] = o_pair
                else:
                    for n, p in enumerate(planes):
                        o_ref[p, cur, cols] = o_pair[n * piece:(n + 1) * piece].astype(BF16)
            if interleave:
                m_ref[out_rows, :] = m_tiles[tile]
                l_ref[out_rows, :] = l_tiles[tile]
    if interleave:
        for hp in range(N_HEADS // 2):
            o_ref[:, hp * LANES:(hp + 1) * LANES] = o_stage[hp].astype(BF16)


def _attn_branch(q, k, v, bias, *, dil, split=1, interleave=0):
    b, n_planes, length, d = q.shape
    piece = ATT_BLOCK // split
    units = min(ATT_UNITS_PER_STEP, length // piece)
    step_planes = split * min(n_planes // split, ATT_UNITS_PER_STEP // units)
    rows = units * piece
    main = pl.BlockSpec((None, step_planes, rows, d), lambda bi, r, i: (bi, r, i, 0))
    halo = pl.BlockSpec((None, step_planes, piece, d),
                        lambda bi, r, i: (bi, r, jnp.maximum(i * units - 1, 0), 0))
    scratch = []
    if interleave:
        assert interleave == step_planes and rows == length and split == 1
        out_planes, out_len = n_planes // interleave, interleave * length
        o_spec = pl.BlockSpec((None, None, out_len, d), lambda bi, r, i: (bi, r, 0, 0))
        stat = pl.BlockSpec((None, None, out_len, LANES), lambda bi, r, i: (bi, r, 0, 0))
        n_tiles = step_planes * units
        scratch = [pltpu.VMEM((d // LANES, out_len, LANES), F32),
                   pltpu.VMEM((n_tiles, ATT_BLOCK, LANES), F32),
                   pltpu.VMEM((n_tiles, ATT_BLOCK, LANES), F32)]
    else:
        out_planes, out_len = n_planes, length
        o_spec = main
        stat = pl.BlockSpec((None, step_planes, rows, LANES), lambda bi, r, i: (bi, r, i, 0))
    stat_shape = jax.ShapeDtypeStruct((b, out_planes, out_len, LANES), F32)
    return pl.pallas_call(
        functools.partial(_attn_kernel, split=split, interleave=interleave),
        grid=(b, n_planes // step_planes, length // rows),
        in_specs=[main, main, main, halo, halo, _const_spec(bias.shape)],
        out_specs=[o_spec, stat, stat],
        out_shape=[jax.ShapeDtypeStruct((b, out_planes, out_len, d), BF16), stat_shape, stat_shape],
        scratch_shapes=scratch,
        compiler_params=_params(3),
        name=f"attn_d{dil}",
    )(q, k, v, k, v, bias)


def _merge_kernel(x_ref, *refs):
    n_br = len(DILATIONS)
    branch_refs = [refs[3 * i:3 * i + 3] for i in range(n_br)]
    e_ref, wo_ref, out_ref, y_scr = refs[3 * n_br:]
    planes, tj, _ = branch_refs[0][0].shape

    def plane_major(ref):
        return jnp.concatenate([ref[p] for p in range(planes)], axis=0)

    ms = [plane_major(br[1]) for br in branch_refs]
    ls = [plane_major(br[2]) for br in branch_refs]
    m_max = functools.reduce(jnp.maximum, ms)
    es = [jnp.exp2(m - m_max) for m in ms]
    total = functools.reduce(jnp.add, [e * l for e, l in zip(es, ls)])
    merged = None
    for e, br in zip(es, branch_refs):
        c = e / total
        c_hi = c.astype(BF16)
        c_lo = (c - c_hi.astype(F32)).astype(BF16)
        c_wide = jnp.dot(jnp.concatenate([c_hi, c_lo], axis=-1), e_ref[...],
                         preferred_element_type=F32)
        term = c_wide * plane_major(br[0]).astype(F32)
        merged = term if merged is None else merged + term
    y = jnp.dot(merged.astype(BF16), wo_ref[...], preferred_element_type=F32)
    for slab in range(D_MODEL // LANES):
        cols = slice(slab * LANES, (slab + 1) * LANES)
        for p in range(planes):
            y_scr[slab, pl.ds(p, tj, stride=planes), :] = y[p * tj:(p + 1) * tj, cols]
        out_ref[0, :, cols] = x_ref[0, :, cols] + y_scr[slab]


def _merge_out_proj(x, branches, expand, w_o):
    b, s, d = x.shape
    ts = ROW_TILE
    planes = branches[0][0].shape[1]
    row_spec = pl.BlockSpec((1, ts, d), lambda i, j: (i, j, 0))
    o_spec = pl.BlockSpec((None, planes, ts // planes, d), lambda i, j: (i, 0, j, 0))
    stat_spec = pl.BlockSpec((None, planes, ts // planes, LANES), lambda i, j: (i, 0, j, 0))
    args = [t for br in branches for t in br]
    return pl.pallas_call(
        _merge_kernel,
        grid=(b, s // ts),
        in_specs=[row_spec] + [o_spec, stat_spec, stat_spec] * len(branches)
                 + [_const_spec(expand.shape), _const_spec((d, d))],
        out_specs=row_spec,
        out_shape=jax.ShapeDtypeStruct(x.shape, F32),
        scratch_shapes=[pltpu.VMEM((d // LANES, ts, LANES), F32)],
        compiler_params=_params(2),
        name="merge_out_proj",
    )(x, *args, expand, w_o)


def _t5_causal_bucket(dist):
    max_exact = N_REL_BUCKETS // 2
    dd = jnp.maximum(dist, 1).astype(F32)
    large = max_exact + (jnp.log(dd / max_exact) / math.log(REL_MAX_DISTANCE / max_exact)
                         * (N_REL_BUCKETS - max_exact)).astype(jnp.int32)
    large = jnp.minimum(large, N_REL_BUCKETS - 1)
    return jnp.where(dist < max_exact, dist, large)


def _bias_tables(rel_table, dil, split=1):
    n = ATT_BLOCK
    n_heads = rel_table.shape[1]
    by_offset = LOG2E * rel_table.astype(F32)[_t5_causal_bucket((n - jnp.arange(n + 1)) * dil)].T
    period = 3 * n + 1
    row = jnp.concatenate(
        [by_offset, jnp.full((n_heads, period - (n + 1)), MASK_VALUE, F32)], axis=1)
    band = jnp.tile(row, (1, n))[:, :n * (period - 1)].reshape(n_heads, n, period - 1)
    regular = band[:, :, :2 * n]
    start = jnp.where(jnp.arange(2 * n) >= n, regular, MASK_VALUE)
    tables = jnp.stack([regular, start])
    if split > 1:
        tables = tables.reshape(2, n_heads, n // split, split, 2 * n // split, split)
        tables = tables.transpose(0, 1, 3, 2, 5, 4).reshape(2, n_heads, n, 2 * n)
    return tables


def _head_expand_matrix():
    row = jnp.arange(2 * LANES)[:, None] % LANES
    head = jnp.arange(D_MODEL)[None, :] // HEAD_DIM
    stat_lane = jnp.where(head % 2 == 0, HEAD_DIM + head // 2, head // 2)
    return (row == stat_lane).astype(BF16)


def kernel(x, rel_bias, even_norm, even_w_in, even_conv_w, even_pool_w, even_pool_scale,
           even_w_out, odd_norm, odd_w_qkv, odd_q_norm, odd_k_norm, odd_w_o, ffn_norm,
           ffn_w_up, ffn_conv_w, ffn_conv_b, ffn_w_down):
    ffn_params = (ffn_norm[:, None], ffn_w_up.astype(BF16), ffn_conv_w, ffn_conv_b[:, None],
                  ffn_w_down.astype(BF16))

    def ffn(t, layer):
        return _ffn(t, layer, *ffn_params)

    x = _mixer(x, even_norm[0][None], even_w_in[0].astype(BF16), even_conv_w[0],
               even_pool_w[0].astype(BF16), even_pool_scale[0][None],
               even_w_out[0].astype(BF16))
    x = ffn(x, 0)

    gq_pair = jnp.tile(odd_q_norm[0], 2)[None]
    gk_pair = jnp.tile(odd_k_norm[0], 2)[None]
    qkv4, qkv16 = _qkv(x, odd_norm[0][None], odd_w_qkv[0].astype(BF16), gq_pair, gk_pair)
    branches = [
        _attn_branch(*qkv4, _bias_tables(rel_bias, 1, split=4), dil=1, split=4),
        _attn_branch(*qkv4, _bias_tables(rel_bias, 4), dil=4),
        _attn_branch(*qkv16, _bias_tables(rel_bias, 16), dil=16, interleave=4),
    ]
    x = _merge_out_proj(x, branches, _head_expand_matrix(), odd_w_o[0].astype(BF16))
    x = ffn(x, 1)
    return x
```

```python
import functools
import math

import jax
import jax.numpy as jnp
import numpy as np
from jax import lax
from jax.experimental import pallas as pl
from jax.experimental.pallas import tpu as pltpu

D_MODEL = 1024
CONV_WIDTH = 3
A_WIDTH = 512
B_WIDTH = 512
POOL_WINDOWS = (2, 4, 8, 16)
POOL_GROUP = 128
EVEN_IN = 3 * A_WIDTH + B_WIDTH
HEAD_DIM = 64
N_HEADS = 16
DILATED_PAIRS = ((128, 1), (512, 4), (2048, 16))
DILATIONS = tuple(dil for _, dil in DILATED_PAIRS)
N_REL_BUCKETS = 32
REL_MAX_DISTANCE = 2048
D_FF = 2816
EPS = 1e-6
MASK_VALUE = -1e30

BF16 = jnp.bfloat16
F32 = jnp.float32

LANES = 128
ATT_BLOCK = 128
ATT_UNITS_PER_STEP = 8
LOG2E = math.log2(math.e)
ROW_TILE = 512
FFN_ROW_TILE = 1024
POOL_HALO = 16
CONV_HALO = 8
FF_CHUNK = 256
QKV_CHUNK = 256
MIX_CHUNK = 256
SHIFT_BASE = 8
VMEM_LIMIT = 56 * 1024 * 1024


def _const_spec(shape):
    nd = len(shape)
    return pl.BlockSpec(shape, lambda *_: (0,) * nd, pipeline_mode=pl.Buffered(1))


def _layer_spec(shape, layer):
    nd = len(shape)
    return pl.BlockSpec((None,) + tuple(shape), lambda *_: (layer,) + (0,) * nd,
                        pipeline_mode=pl.Buffered(1))


def _params(n_axes):
    return pltpu.CompilerParams(
        dimension_semantics=("arbitrary",) * n_axes, vmem_limit_bytes=VMEM_LIMIT)


def _rmsnorm(x, g):
    ms = jnp.mean(x * x, axis=-1, keepdims=True)
    return x * lax.rsqrt(ms + EPS) * g


def _stage_rows(scr, slot, base, tile, prev):
    n_prev = prev.shape[0]
    scr[slot, pl.ds(base - 2 * n_prev, n_prev, stride=2), :] = prev
    scr[slot, pl.ds(base, tile.shape[0], stride=2), :] = tile


def _rows_back(scr, slot, base, n_rows, k):
    return scr[slot, pl.ds(base - 2 * k, n_rows, stride=2), :]


def _mixer_kernel(x_ref, g_ref, win_ref, cw_ref, pw_ref, ps_ref, wout_ref, o_ref,
                  y_buf, carry_a, carry_p, scr):
    ts = x_ref.shape[1]
    s = pl.program_id(1)
    base = 2 * POOL_HALO

    @pl.when(s == 0)
    def _():
        carry_a[...] = jnp.zeros(carry_a.shape, F32)
        carry_p[...] = jnp.zeros(carry_p.shape, F32)

    x = x_ref[0]
    xn = _rmsnorm(x, g_ref[...]).astype(BF16)
    slabs = MIX_CHUNK // LANES

    def proj(col0):
        return jnp.dot(xn, win_ref[:, col0:col0 + MIX_CHUNK], preferred_element_type=F32)

    for c in range(A_WIDTH // MIX_CHUNK):
        h = proj(c * MIX_CHUNK)
        gate_b = proj(A_WIDTH + c * MIX_CHUNK)
        gate_c = proj(2 * A_WIDTH + c * MIX_CHUNK)
        ch = gate_c * h
        for sl in range(slabs):
            slot = c * slabs + sl
            cols = slice(slot * LANES, (slot + 1) * LANES)
            ch_sl = ch[:, sl * LANES:(sl + 1) * LANES]
            _stage_rows(scr, slot, base, ch_sl, carry_a[CONV_HALO - 2:, cols])
            carry_a[:, cols] = ch_sl[ts - CONV_HALO:, :]
            cw = cw_ref[:, cols]
            conv = (cw[0:1] * _rows_back(scr, slot, base, ts, 2)
                    + cw[1:2] * _rows_back(scr, slot, base, ts, 1) + cw[2:3] * ch_sl)
            y_buf[:, cols] = (gate_b[:, sl * LANES:(sl + 1) * LANES] * conv).astype(BF16)

    pos = s * ts + lax.broadcasted_iota(jnp.int32, (ts, 1), 0)
    for c in range(B_WIDTH // MIX_CHUNK):
        pin = proj(3 * A_WIDTH + c * MIX_CHUNK)
        for sl in range(slabs):
            g = c * slabs + sl
            k = POOL_WINDOWS[g]
            slot = A_WIDTH // LANES + g
            cols = slice(g * POOL_GROUP, (g + 1) * POOL_GROUP)
            cur = pin[:, sl * LANES:(sl + 1) * LANES]
            _stage_rows(scr, slot, base, cur, carry_p[:, cols])
            carry_p[:, cols] = cur[ts - POOL_HALO:, :]
            acc = cur
            for j in range(1, k):
                acc = acc + _rows_back(scr, slot, base, ts, j)
            cnt = jnp.minimum(pos + 1, k).astype(F32)
            pooled = acc / cnt - cur
            yb = jnp.dot(pooled.astype(BF16), pw_ref[g], preferred_element_type=F32)
            y_buf[:, A_WIDTH + g * POOL_GROUP:A_WIDTH + (g + 1) * POOL_GROUP] = (
                yb * ps_ref[:, cols]).astype(BF16)

    o_ref[0] = x + jnp.dot(y_buf[...], wout_ref[...], preferred_element_type=F32)


def _mixer(x, g, w_in, conv_w, pool_w, pool_scale, w_out):
    b, s, d = x.shape
    ts = FFN_ROW_TILE
    row_spec = pl.BlockSpec((1, ts, d), lambda i, j: (i, j, 0))
    return pl.pallas_call(
        _mixer_kernel,
        grid=(b, s // ts),
        in_specs=[row_spec, _const_spec((1, d)), _const_spec((d, EVEN_IN)),
                  _const_spec((CONV_WIDTH, A_WIDTH)),
                  _const_spec((len(POOL_WINDOWS), POOL_GROUP, POOL_GROUP)),
                  _const_spec((1, B_WIDTH)), _const_spec((d, d))],
        out_specs=row_spec,
        out_shape=jax.ShapeDtypeStruct(x.shape, F32),
        scratch_shapes=[pltpu.VMEM((ts, d), BF16),
                        pltpu.VMEM((CONV_HALO, A_WIDTH), F32),
                        pltpu.VMEM((POOL_HALO, B_WIDTH), F32),
                        pltpu.VMEM(((A_WIDTH + B_WIDTH) // LANES, 2 * (POOL_HALO + ts), LANES),
                                   F32)],
        compiler_params=_params(2),
        name="mixer",
    )(x, g, w_in, conv_w, pool_w, pool_scale, w_out)


def _ffn_kernel(x_ref, g_ref, wup_ref, cw_ref, cb_ref, wdn_ref, o_ref, h_buf, carry, u_scr):
    ts = x_ref.shape[1]
    s = pl.program_id(1)

    @pl.when(s == 0)
    def _():
        carry[...] = jnp.zeros(carry.shape, F32)

    x = x_ref[0]
    xn = _rmsnorm(x, g_ref[...]).astype(BF16)
    slabs = FF_CHUNK // LANES
    for j in range(D_FF // FF_CHUNK):
        halves = []
        for half in range(2):
            c0 = half * D_FF + j * FF_CHUNK
            u = jnp.dot(xn, wup_ref[:, c0:c0 + FF_CHUNK], preferred_element_type=F32)
            parts = []
            for sl in range(slabs):
                cols = slice(c0 + sl * LANES, c0 + (sl + 1) * LANES)
                slot = ((j % 2) * 2 + half) * slabs + sl
                u_sl = u[:, sl * LANES:(sl + 1) * LANES]
                _stage_rows(u_scr, slot, SHIFT_BASE, u_sl, carry[CONV_HALO - 2:, cols])
                carry[:, cols] = u_sl[ts - CONV_HALO:ts, :]
                w = cw_ref[:, cols]
                parts.append(w[0:1] * _rows_back(u_scr, slot, SHIFT_BASE, ts, 2)
                             + w[1:2] * _rows_back(u_scr, slot, SHIFT_BASE, ts, 1)
                             + w[2:3] * u_sl + cb_ref[:, cols])
            halves.append(jnp.concatenate(parts, axis=-1))
        gate, up = halves
        h_buf[:, j * FF_CHUNK:(j + 1) * FF_CHUNK] = (gate * jax.nn.sigmoid(gate) * up).astype(BF16)
    o_ref[0] = x + jnp.dot(h_buf[...], wdn_ref[...], preferred_element_type=F32)


def _ffn(x, layer, g, w_up, conv_w, conv_b, w_down):
    b, s, d = x.shape
    ts = FFN_ROW_TILE
    row_spec = pl.BlockSpec((1, ts, d), lambda i, j: (i, j, 0))
    return pl.pallas_call(
        _ffn_kernel,
        grid=(b, s // ts),
        in_specs=[row_spec, _layer_spec((1, d), layer), _layer_spec((d, 2 * D_FF), layer),
                  _layer_spec((CONV_WIDTH, 2 * D_FF), layer),
                  _layer_spec((1, 2 * D_FF), layer), _layer_spec((D_FF, d), layer)],
        out_specs=row_spec,
        out_shape=jax.ShapeDtypeStruct(x.shape, F32),
        scratch_shapes=[pltpu.VMEM((ts, D_FF), BF16),
                        pltpu.VMEM((CONV_HALO, 2 * D_FF), F32),
                        pltpu.VMEM((4 * FF_CHUNK // LANES, SHIFT_BASE + 2 * ts, LANES), F32)],
        compiler_params=_params(2),
        name="ffn",
    )(x, g, w_up, conv_w, conv_b, w_down)


def _head_pair_norm(t, g_pair, scale):
    lane = lax.broadcasted_iota(jnp.int32, t.shape, 1)
    lo = lane < HEAD_DIM
    sq = t * t
    s_lo = jnp.sum(jnp.where(lo, sq, 0.0), axis=-1, keepdims=True)
    s_hi = jnp.sum(jnp.where(lo, 0.0, sq), axis=-1, keepdims=True)
    r = lax.rsqrt(jnp.where(lo, s_lo, s_hi) * (1.0 / HEAD_DIM) + EPS)
    return t * r * (g_pair * scale)


def _qkv_kernel(x_ref, g_ref, w_ref, gq_ref, gk_ref, *refs):
    assert DILATIONS == (1, 4, 16)
    out4, out16 = refs[0:3], refs[3:6]
    x_stage, stage = refs[6:]
    d = D_MODEL
    ts = x_ref.shape[1]
    tj = ts // 4
    for sl in range(d // LANES):
        x_stage[sl] = x_ref[0, :, sl * LANES:(sl + 1) * LANES]
    x = jnp.concatenate(
        [jnp.concatenate([x_stage[sl, pl.ds(r4, tj, stride=4), :] for sl in range(d // LANES)],
                         axis=-1) for r4 in range(4)], axis=0)
    xn = _rmsnorm(x, g_ref[...]).astype(BF16)
    slabs = QKV_CHUNK // LANES
    for c in range(3 * d // QKV_CHUNK):
        res = jnp.dot(xn, w_ref[:, c * QKV_CHUNK:(c + 1) * QKV_CHUNK],
                      preferred_element_type=F32)
        for sl in range(slabs):
            slot = c * slabs + sl
            which, hp = divmod(slot, d // LANES)
            cols = slice(hp * LANES, (hp + 1) * LANES)
            slab = res[:, sl * LANES:(sl + 1) * LANES]
            if which == 0:
                slab = _head_pair_norm(slab, gq_ref[...], HEAD_DIM ** -0.5 * LOG2E)
            elif which == 1:
                slab = _head_pair_norm(slab, gk_ref[...], 1.0)
            stage[slot] = slab
            for r4 in range(4):
                out4[which][r4, :, cols] = slab[r4 * tj:(r4 + 1) * tj].astype(BF16)
                for c4 in range(4):
                    p16 = stage[slot, pl.ds(r4 * tj + c4, tj // 4, stride=4), :]
                    out16[which][4 * r4 + c4, :, cols] = p16.astype(BF16)


def _qkv(x, g, w_qkv, gq_pair, gk_pair):
    b, s, d = x.shape
    ts = ROW_TILE
    row_spec = pl.BlockSpec((1, ts, d), lambda i, j: (i, j, 0))
    out_specs, out_shapes = [], []
    for planes in (4, 16):
        out_specs += [pl.BlockSpec((None, planes, ts // planes, d), lambda i, j: (i, 0, j, 0))] * 3
        out_shapes += [jax.ShapeDtypeStruct((b, planes, s // planes, d), BF16)] * 3
    outs = pl.pallas_call(
        _qkv_kernel,
        grid=(b, s // ts),
        in_specs=[row_spec, _const_spec((1, d)), _const_spec((d, 3 * d)),
                  _const_spec((1, LANES)), _const_spec((1, LANES))],
        out_specs=out_specs,
        out_shape=out_shapes,
        scratch_shapes=[pltpu.VMEM((d // LANES, ts, LANES), F32),
                        pltpu.VMEM((3 * d // LANES, ts, LANES), F32)],
        compiler_params=_params(2),
        name="qkv",
    )(x, g, w_qkv, gq_pair, gk_pair)
    return outs[0:3], outs[3:6]


def _stat_lane(h):
    return HEAD_DIM + h // 2 if h % 2 == 0 else h // 2


def _rows_cat(parts):
    return parts[0] if len(parts) == 1 else jnp.concatenate(parts, axis=0)


def _attn_kernel(q_ref, k_ref, v_ref, kh_ref, vh_ref, bias_ref, o_ref, m_ref, l_ref, *scratch,
                 split, interleave):
    nb = ATT_BLOCK
    n_planes, rows, _ = q_ref.shape
    piece = nb // split
    first_step = pl.program_id(2) == 0
    first_table = jnp.where(first_step, 1, 0)
    if interleave:
        o_stage, m_tiles, l_tiles = scratch

    lo = lax.broadcasted_iota(jnp.int32, (nb, LANES), 1) < HEAD_DIM
    lo_kv = lax.broadcasted_iota(jnp.int32, (2 * nb, LANES), 1) < HEAD_DIM

    for grp in range(n_planes // split):
        planes = range(grp * split, (grp + 1) * split)
        for u in range(rows // piece):
            cur = slice(u * piece, (u + 1) * piece)
            tile = grp * (rows // piece) + u
            out_rows = pl.ds(interleave * nb * u + grp, nb, stride=interleave) if interleave else None

            def block_rows(ref, cols):
                return _rows_cat([ref[p, cur, cols] for p in planes])

            def window_rows(ref, halo_ref, cols):
                parts = []
                for p in planes:
                    if u == 0:
                        parts += [halo_ref[p, :, cols], ref[p, cur, cols]]
                    else:
                        parts.append(ref[p, (u - 1) * piece:(u + 1) * piece, cols])
                return _rows_cat(parts)

            def put_stat(ref, tiles, lane_cols, value):
                if interleave:
                    tiles[tile, :, lane_cols] = value
                else:
                    for n, p in enumerate(planes):
                        ref[p, cur, lane_cols] = value[n * piece:(n + 1) * piece]

            put_stat(m_ref, m_tiles if interleave else None, slice(None), jnp.zeros((nb, LANES), F32))
            put_stat(l_ref, l_tiles if interleave else None, slice(None), jnp.ones((nb, LANES), F32))
            for hp in range(N_HEADS // 2):
                cols = slice(hp * LANES, (hp + 1) * LANES)
                q_pair = block_rows(q_ref, cols)
                k_win = window_rows(k_ref, kh_ref, cols)
                v_win = window_rows(v_ref, vh_ref, cols)
                table = first_table if u == 0 else 0
                outs = []
                for sub in range(2):
                    h = 2 * hp + sub
                    own = lo if sub == 0 else jnp.logical_not(lo)
                    own_kv = lo_kv if sub == 0 else jnp.logical_not(lo_kv)
                    q_h = jnp.where(own, q_pair, jnp.zeros_like(q_pair))
                    sc = lax.dot_general(q_h, k_win, (((1,), (1,)), ((), ())),
                                         preferred_element_type=F32)
                    sc = sc + bias_ref[table, h]
                    m = jnp.max(sc, axis=-1, keepdims=True)
                    p = jnp.exp2(sc - m).astype(BF16)
                    v_h = jnp.where(own_kv, v_win, jnp.ones_like(v_win))
                    pv = jnp.dot(p, v_h, preferred_element_type=F32)
                    outs.append(pv)
                    sl = slice(_stat_lane(h), _stat_lane(h) + 1)
                    put_stat(m_ref, m_tiles if interleave else None, sl, m)
                    put_stat(l_ref, l_tiles if interleave else None, sl, pv[:, sl])
                o_pair = jnp.where(lo, outs[0], outs[1])
                if interleave:
                    o_stage[hp, out_rows, :] = o_pair
                else:
                    for n, p in enumerate(planes):
                        o_ref[p, cur, cols] = o_pair[n * piece:(n + 1) * piece].astype(BF16)
            if interleave:
                m_ref[out_rows, :] = m_tiles[tile]
                l_ref[out_rows, :] = l_tiles[tile]
    if interleave:
        for hp in range(N_HEADS // 2):
            o_ref[:, hp * LANES:(hp + 1) * LANES] = o_stage[hp].astype(BF16)


def _attn_branch(q, k, v, bias, *, branch, split=1, interleave=0):
    b, n_planes, length, d = q.shape
    piece = ATT_BLOCK // split
    units = min(ATT_UNITS_PER_STEP, length // piece)
    step_planes = split * min(n_planes // split, ATT_UNITS_PER_STEP // units)
    rows = units * piece
    main = pl.BlockSpec((None, step_planes, rows, d), lambda bi, r, i: (bi, r, i, 0))
    halo = pl.BlockSpec((None, step_planes, piece, d),
                        lambda bi, r, i: (bi, r, jnp.maximum(i * units - 1, 0), 0))
    scratch = []
    if interleave:
        assert interleave == step_planes and rows == length and split == 1
        out_planes, out_len = n_planes // interleave, interleave * length
        o_spec = pl.BlockSpec((None, None, out_len, d), lambda bi, r, i: (bi, r, 0, 0))
        stat = pl.BlockSpec((None, None, out_len, LANES), lambda bi, r, i: (bi, r, 0, 0))
        n_tiles = step_planes * units
        scratch = [pltpu.VMEM((d // LANES, out_len, LANES), F32),
                   pltpu.VMEM((n_tiles, ATT_BLOCK, LANES), F32),
                   pltpu.VMEM((n_tiles, ATT_BLOCK, LANES), F32)]
    else:
        out_planes, out_len = n_planes, length
        o_spec = main
        stat = pl.BlockSpec((None, step_planes, rows, LANES), lambda bi, r, i: (bi, r, i, 0))
    stat_shape = jax.ShapeDtypeStruct((b, out_planes, out_len, LANES), F32)
    return pl.pallas_call(
        functools.partial(_attn_kernel, split=split, interleave=interleave),
        grid=(b, n_planes // step_planes, length // rows),
        in_specs=[main, main, main, halo, halo, _layer_spec(bias.shape[1:], branch)],
        out_specs=[o_spec, stat, stat],
        out_shape=[jax.ShapeDtypeStruct((b, out_planes, out_len, d), BF16), stat_shape, stat_shape],
        scratch_shapes=scratch,
        compiler_params=_params(3),
        name=f"attn_d{DILATIONS[branch]}",
    )(q, k, v, k, v, bias)


def _merge_kernel(x_ref, *refs):
    n_br = len(DILATIONS)
    branch_refs = [refs[3 * i:3 * i + 3] for i in range(n_br)]
    e_ref, wo_ref, out_ref, y_scr = refs[3 * n_br:]
    planes, tj, _ = branch_refs[0][0].shape

    def plane_major(ref):
        return jnp.concatenate([ref[p] for p in range(planes)], axis=0)

    ms = [plane_major(br[1]) for br in branch_refs]
    ls = [plane_major(br[2]) for br in branch_refs]
    m_max = functools.reduce(jnp.maximum, ms)
    es = [jnp.exp2(m - m_max) for m in ms]
    total = functools.reduce(jnp.add, [e * l for e, l in zip(es, ls)])
    merged = None
    for e, br in zip(es, branch_refs):
        c = e / total
        c_hi = c.astype(BF16)
        c_lo = (c - c_hi.astype(F32)).astype(BF16)
        c_wide = jnp.dot(jnp.concatenate([c_hi, c_lo], axis=-1), e_ref[...],
                         preferred_element_type=F32)
        term = c_wide * plane_major(br[0]).astype(F32)
        merged = term if merged is None else merged + term
    y = jnp.dot(merged.astype(BF16), wo_ref[...], preferred_element_type=F32)
    for slab in range(D_MODEL // LANES):
        cols = slice(slab * LANES, (slab + 1) * LANES)
        for p in range(planes):
            y_scr[slab, pl.ds(p, tj, stride=planes), :] = y[p * tj:(p + 1) * tj, cols]
        out_ref[0, :, cols] = x_ref[0, :, cols] + y_scr[slab]


def _merge_out_proj(x, branches, expand, w_o):
    b, s, d = x.shape
    ts = ROW_TILE
    planes = branches[0][0].shape[1]
    row_spec = pl.BlockSpec((1, ts, d), lambda i, j: (i, j, 0))
    o_spec = pl.BlockSpec((None, planes, ts // planes, d), lambda i, j: (i, 0, j, 0))
    stat_spec = pl.BlockSpec((None, planes, ts // planes, LANES), lambda i, j: (i, 0, j, 0))
    args = [t for br in branches for t in br]
    return pl.pallas_call(
        _merge_kernel,
        grid=(b, s // ts),
        in_specs=[row_spec] + [o_spec, stat_spec, stat_spec] * len(branches)
                 + [_const_spec(expand.shape), _const_spec((d, d))],
        out_specs=row_spec,
        out_shape=jax.ShapeDtypeStruct(x.shape, F32),
        scratch_shapes=[pltpu.VMEM((d // LANES, ts, LANES), F32)],
        compiler_params=_params(2),
        name="merge_out_proj",
    )(x, *args, expand, w_o)


def _t5_causal_bucket(dist):
    max_exact = N_REL_BUCKETS // 2
    dd = jnp.maximum(dist, 1).astype(F32)
    large = max_exact + (jnp.log(dd / max_exact) / math.log(REL_MAX_DISTANCE / max_exact)
                         * (N_REL_BUCKETS - max_exact)).astype(jnp.int32)
    large = jnp.minimum(large, N_REL_BUCKETS - 1)
    return jnp.where(dist < max_exact, dist, large)


def _bias_tables(rel_table, splits):
    n = ATT_BLOCK
    n_br = len(DILATIONS)
    n_heads = rel_table.shape[1]
    dist = (n - jnp.arange(n + 1))[None, :] * jnp.asarray(DILATIONS)[:, None]
    by_offset = LOG2E * jnp.swapaxes(rel_table.astype(F32)[_t5_causal_bucket(dist)], 1, 2)
    period = 3 * n + 1
    row = jnp.concatenate(
        [by_offset, jnp.full((n_br, n_heads, period - (n + 1)), MASK_VALUE, F32)], axis=-1)
    band = jnp.tile(row, (1, 1, n))[..., :n * (period - 1)].reshape(n_br, n_heads, n, period - 1)
    band = band[..., :2 * n]
    regular, prev_cols = [], []
    for br, split in enumerate(splits):
        tab = band[br]
        col = np.arange(2 * n)
        if split > 1:
            tab = tab.reshape(n_heads, n // split, split, 2 * n).transpose(0, 2, 1, 3)
            src = split * (col % (2 * n // split)) + col // (2 * n // split)
            perm = np.zeros((2 * n, 2 * n), np.float32)
            perm[src, col] = 1.0
            tab = jnp.einsum("hac,ck->hak", tab.reshape(n_heads, n, 2 * n), perm,
                             precision=lax.Precision.HIGHEST)
        regular.append(tab)
        prev_cols.append(col % (2 * n // split) < n // split)
    regular = jnp.stack(regular)
    start = jnp.where(jnp.asarray(np.stack(prev_cols))[:, None, None, :], MASK_VALUE, regular)
    return jnp.stack([regular, start], axis=1)


def _head_expand_matrix():
    row = jnp.arange(2 * LANES)[:, None] % LANES
    head = jnp.arange(D_MODEL)[None, :] // HEAD_DIM
    stat_lane = jnp.where(head % 2 == 0, HEAD_DIM + head // 2, head // 2)
    return (row == stat_lane).astype(BF16)


def kernel(x, rel_bias, even_norm, even_w_in, even_conv_w, even_pool_w, even_pool_scale,
           even_w_out, odd_norm, odd_w_qkv, odd_q_norm, odd_k_norm, odd_w_o, ffn_norm,
           ffn_w_up, ffn_conv_w, ffn_conv_b, ffn_w_down):
    ffn_params = (ffn_norm[:, None], ffn_w_up.astype(BF16), ffn_conv_w, ffn_conv_b[:, None],
                  ffn_w_down.astype(BF16))

    def ffn(t, layer):
        return _ffn(t, layer, *ffn_params)

    x = _mixer(x, even_norm[0][None], even_w_in[0].astype(BF16), even_conv_w[0],
               even_pool_w[0].astype(BF16), even_pool_scale[0][None],
               even_w_out[0].astype(BF16))
    x = ffn(x, 0)

    gq_pair = jnp.tile(odd_q_norm[0], 2)[None]
    gk_pair = jnp.tile(odd_k_norm[0], 2)[None]
    qkv4, qkv16 = _qkv(x, odd_norm[0][None], odd_w_qkv[0].astype(BF16), gq_pair, gk_pair)
    bias = _bias_tables(rel_bias, splits=(4, 1, 1))
    branches = [
        _attn_branch(*qkv4, bias, branch=0, split=4),
        _attn_branch(*qkv4, bias, branch=1),
        _attn_branch(*qkv16, bias, branch=2, interleave=4),
    ]
    x = _merge_out_proj(x, branches, _head_expand_matrix(), odd_w_o[0].astype(BF16))
    x = ffn(x, 1)
    return x
```

```python
import functools
import math

import jax
import jax.numpy as jnp
import numpy as np
from jax import lax
from jax.experimental import pallas as pl
from jax.experimental.pallas import tpu as pltpu

D_MODEL = 1024
CONV_WIDTH = 3
A_WIDTH = 512
B_WIDTH = 512
POOL_WINDOWS = (2, 4, 8, 16)
POOL_GROUP = 128
EVEN_IN = 3 * A_WIDTH + B_WIDTH
HEAD_DIM = 64
N_HEADS = 16
DILATED_PAIRS = ((128, 1), (512, 4), (2048, 16))
DILATIONS = tuple(dil for _, dil in DILATED_PAIRS)
N_REL_BUCKETS = 32
REL_MAX_DISTANCE = 2048
D_FF = 2816
EPS = 1e-6
MASK_VALUE = -1e30

BF16 = jnp.bfloat16
F32 = jnp.float32

LANES = 128
ATT_BLOCK = 128
ATT_UNITS_PER_STEP = 8
LOG2E = math.log2(math.e)
ROW_TILE = 512
FFN_ROW_TILE = 1024
POOL_HALO = 16
CONV_HALO = 8
FF_CHUNK = 256
QKV_CHUNK = 256
MIX_CHUNK = 256
SHIFT_BASE = 8
VMEM_LIMIT = 56 * 1024 * 1024


def _const_spec(shape):
    nd = len(shape)
    return pl.BlockSpec(shape, lambda *_: (0,) * nd, pipeline_mode=pl.Buffered(1))


def _layer_spec(shape, layer):
    nd = len(shape)
    return pl.BlockSpec((None,) + tuple(shape), lambda *_: (layer,) + (0,) * nd,
                        pipeline_mode=pl.Buffered(1))


def _params(n_axes):
    return pltpu.CompilerParams(
        dimension_semantics=("arbitrary",) * n_axes, vmem_limit_bytes=VMEM_LIMIT)


def _rmsnorm(x, g):
    ms = jnp.mean(x * x, axis=-1, keepdims=True)
    return x * lax.rsqrt(ms + EPS) * g


def _stage_rows(scr, slot, base, tile, prev):
    n_prev = prev.shape[0]
    scr[slot, pl.ds(base - 2 * n_prev, n_prev, stride=2), :] = prev
    scr[slot, pl.ds(base, tile.shape[0], stride=2), :] = tile


def _rows_back(scr, slot, base, n_rows, k):
    return scr[slot, pl.ds(base - 2 * k, n_rows, stride=2), :]


def _mixer_kernel(x_ref, g_ref, win_ref, cw_ref, pw_ref, ps_ref, wout_ref, o_ref,
                  y_buf, carry_a, carry_p, scr):
    ts = x_ref.shape[1]
    s = pl.program_id(1)
    base = 2 * POOL_HALO

    @pl.when(s == 0)
    def _():
        carry_a[...] = jnp.zeros(carry_a.shape, F32)
        carry_p[...] = jnp.zeros(carry_p.shape, F32)

    x = x_ref[0]
    xn = _rmsnorm(x, g_ref[...]).astype(BF16)
    slabs = MIX_CHUNK // LANES

    def proj(col0):
        return jnp.dot(xn, win_ref[:, col0:col0 + MIX_CHUNK], preferred_element_type=F32)

    for c in range(A_WIDTH // MIX_CHUNK):
        h = proj(c * MIX_CHUNK)
        gate_b = proj(A_WIDTH + c * MIX_CHUNK)
        gate_c = proj(2 * A_WIDTH + c * MIX_CHUNK)
        ch = gate_c * h
        for sl in range(slabs):
            slot = c * slabs + sl
            cols = slice(slot * LANES, (slot + 1) * LANES)
            ch_sl = ch[:, sl * LANES:(sl + 1) * LANES]
            _stage_rows(scr, slot, base, ch_sl, carry_a[CONV_HALO - 2:, cols])
            carry_a[:, cols] = ch_sl[ts - CONV_HALO:, :]
            cw = cw_ref[:, cols]
            conv = (cw[0:1] * _rows_back(scr, slot, base, ts, 2)
                    + cw[1:2] * _rows_back(scr, slot, base, ts, 1) + cw[2:3] * ch_sl)
            y_buf[:, cols] = (gate_b[:, sl * LANES:(sl + 1) * LANES] * conv).astype(BF16)

    pos = s * ts + lax.broadcasted_iota(jnp.int32, (ts, 1), 0)
    for c in range(B_WIDTH // MIX_CHUNK):
        pin = proj(3 * A_WIDTH + c * MIX_CHUNK)
        for sl in range(slabs):
            g = c * slabs + sl
            k = POOL_WINDOWS[g]
            slot = A_WIDTH // LANES + g
            cols = slice(g * POOL_GROUP, (g + 1) * POOL_GROUP)
            cur = pin[:, sl * LANES:(sl + 1) * LANES]
            _stage_rows(scr, slot, base, cur, carry_p[:, cols])
            carry_p[:, cols] = cur[ts - POOL_HALO:, :]
            acc = cur
            for j in range(1, k):
                acc = acc + _rows_back(scr, slot, base, ts, j)
            cnt = jnp.minimum(pos + 1, k).astype(F32)
            pooled = acc / cnt - cur
            yb = jnp.dot(pooled.astype(BF16), pw_ref[g], preferred_element_type=F32)
            y_buf[:, A_WIDTH + g * POOL_GROUP:A_WIDTH + (g + 1) * POOL_GROUP] = (
                yb * ps_ref[:, cols]).astype(BF16)

    o_ref[0] = x + jnp.dot(y_buf[...], wout_ref[...], preferred_element_type=F32)


def _mixer(x, g, w_in, conv_w, pool_w, pool_scale, w_out):
    b, s, d = x.shape
    ts = FFN_ROW_TILE
    row_spec = pl.BlockSpec((1, ts, d), lambda i, j: (i, j, 0))
    return pl.pallas_call(
        _mixer_kernel,
        grid=(b, s // ts),
        in_specs=[row_spec, _const_spec((1, d)), _const_spec((d, EVEN_IN)),
                  _const_spec((CONV_WIDTH, A_WIDTH)),
                  _const_spec((len(POOL_WINDOWS), POOL_GROUP, POOL_GROUP)),
                  _const_spec((1, B_WIDTH)), _const_spec((d, d))],
        out_specs=row_spec,
        out_shape=jax.ShapeDtypeStruct(x.shape, F32),
        scratch_shapes=[pltpu.VMEM((ts, d), BF16),
                        pltpu.VMEM((CONV_HALO, A_WIDTH), F32),
                        pltpu.VMEM((POOL_HALO, B_WIDTH), F32),
                        pltpu.VMEM(((A_WIDTH + B_WIDTH) // LANES, 2 * (POOL_HALO + ts), LANES),
                                   F32)],
        compiler_params=_params(2),
        name="mixer",
    )(x, g, w_in, conv_w, pool_w, pool_scale, w_out)


def _ffn_kernel(x_ref, g_ref, wup_ref, cw_ref, cb_ref, wdn_ref, o_ref, h_buf, carry, u_scr):
    ts = x_ref.shape[1]
    s = pl.program_id(1)

    @pl.when(s == 0)
    def _():
        carry[...] = jnp.zeros(carry.shape, F32)

    x = x_ref[0]
    xn = _rmsnorm(x, g_ref[...]).astype(BF16)
    slabs = FF_CHUNK // LANES
    for j in range(D_FF // FF_CHUNK):
        halves = []
        for half in range(2):
            c0 = half * D_FF + j * FF_CHUNK
            u = jnp.dot(xn, wup_ref[:, c0:c0 + FF_CHUNK], preferred_element_type=F32)
            parts = []
            for sl in range(slabs):
                cols = slice(c0 + sl * LANES, c0 + (sl + 1) * LANES)
                slot = ((j % 2) * 2 + half) * slabs + sl
                u_sl = u[:, sl * LANES:(sl + 1) * LANES]
                _stage_rows(u_scr, slot, SHIFT_BASE, u_sl, carry[CONV_HALO - 2:, cols])
                carry[:, cols] = u_sl[ts - CONV_HALO:ts, :]
                w = cw_ref[:, cols]
                parts.append(w[0:1] * _rows_back(u_scr, slot, SHIFT_BASE, ts, 2)
                             + w[1:2] * _rows_back(u_scr, slot, SHIFT_BASE, ts, 1)
                             + w[2:3] * u_sl + cb_ref[:, cols])
            halves.append(jnp.concatenate(parts, axis=-1))
        gate, up = halves
        h_buf[:, j * FF_CHUNK:(j + 1) * FF_CHUNK] = (gate * jax.nn.sigmoid(gate) * up).astype(BF16)
    o_ref[0] = x + jnp.dot(h_buf[...], wdn_ref[...], preferred_element_type=F32)


def _ffn(x, layer, g, w_up, conv_w, conv_b, w_down):
    b, s, d = x.shape
    ts = FFN_ROW_TILE
    row_spec = pl.BlockSpec((1, ts, d), lambda i, j: (i, j, 0))
    return pl.pallas_call(
        _ffn_kernel,
        grid=(b, s // ts),
        in_specs=[row_spec, _layer_spec((1, d), layer), _layer_spec((d, 2 * D_FF), layer),
                  _layer_spec((CONV_WIDTH, 2 * D_FF), layer),
                  _layer_spec((1, 2 * D_FF), layer), _layer_spec((D_FF, d), layer)],
        out_specs=row_spec,
        out_shape=jax.ShapeDtypeStruct(x.shape, F32),
        scratch_shapes=[pltpu.VMEM((ts, D_FF), BF16),
                        pltpu.VMEM((CONV_HALO, 2 * D_FF), F32),
                        pltpu.VMEM((4 * FF_CHUNK // LANES, SHIFT_BASE + 2 * ts, LANES), F32)],
        compiler_params=_params(2),
        name="ffn",
    )(x, g, w_up, conv_w, conv_b, w_down)


def _head_pair_norm(t, g_pair, scale):
    lane = lax.broadcasted_iota(jnp.int32, t.shape, 1)
    lo = lane < HEAD_DIM
    sq = t * t
    s_lo = jnp.sum(jnp.where(lo, sq, 0.0), axis=-1, keepdims=True)
    s_hi = jnp.sum(jnp.where(lo, 0.0, sq), axis=-1, keepdims=True)
    r = lax.rsqrt(jnp.where(lo, s_lo, s_hi) * (1.0 / HEAD_DIM) + EPS)
    return t * r * (g_pair * scale)


def _qkv_kernel(x_ref, g_ref, w_ref, gq_ref, gk_ref, *refs):
    assert DILATIONS == (1, 4, 16)
    out4, out16 = refs[0:3], refs[3:6]
    x_stage, stage = refs[6:]
    d = D_MODEL
    ts = x_ref.shape[1]
    tj = ts // 4
    for sl in range(d // LANES):
        x_stage[sl] = x_ref[0, :, sl * LANES:(sl + 1) * LANES]
    x = jnp.concatenate(
        [jnp.concatenate([x_stage[sl, pl.ds(r4, tj, stride=4), :] for sl in range(d // LANES)],
                         axis=-1) for r4 in range(4)], axis=0)
    xn = _rmsnorm(x, g_ref[...]).astype(BF16)
    slabs = QKV_CHUNK // LANES
    for c in range(3 * d // QKV_CHUNK):
        res = jnp.dot(xn, w_ref[:, c * QKV_CHUNK:(c + 1) * QKV_CHUNK],
                      preferred_element_type=F32)
        for sl in range(slabs):
            slot = c * slabs + sl
            which, hp = divmod(slot, d // LANES)
            cols = slice(hp * LANES, (hp + 1) * LANES)
            slab = res[:, sl * LANES:(sl + 1) * LANES]
            if which == 0:
                slab = _head_pair_norm(slab, gq_ref[...], HEAD_DIM ** -0.5 * LOG2E)
            elif which == 1:
                slab = _head_pair_norm(slab, gk_ref[...], 1.0)
            stage[slot] = slab
            for r4 in range(4):
                out4[which][r4, :, cols] = slab[r4 * tj:(r4 + 1) * tj].astype(BF16)
                for c4 in range(4):
                    p16 = stage[slot, pl.ds(r4 * tj + c4, tj // 4, stride=4), :]
                    out16[which][4 * r4 + c4, :, cols] = p16.astype(BF16)


def _qkv(x, g, w_qkv, gq_pair, gk_pair):
    b, s, d = x.shape
    ts = ROW_TILE
    row_spec = pl.BlockSpec((1, ts, d), lambda i, j: (i, j, 0))
    out_specs, out_shapes = [], []
    for planes in (4, 16):
        out_specs += [pl.BlockSpec((None, planes, ts // planes, d), lambda i, j: (i, 0, j, 0))] * 3
        out_shapes += [jax.ShapeDtypeStruct((b, planes, s // planes, d), BF16)] * 3
    outs = pl.pallas_call(
        _qkv_kernel,
        grid=(b, s // ts),
        in_specs=[row_spec, _const_spec((1, d)), _const_spec((d, 3 * d)),
                  _const_spec((1, LANES)), _const_spec((1, LANES))],
        out_specs=out_specs,
        out_shape=out_shapes,
        scratch_shapes=[pltpu.VMEM((d // LANES, ts, LANES), F32),
                        pltpu.VMEM((3 * d // LANES, ts, LANES), F32)],
        compiler_params=_params(2),
        name="qkv",
    )(x, g, w_qkv, gq_pair, gk_pair)
    return outs[0:3], outs[3:6]


def _stat_lane(h):
    return HEAD_DIM + h // 2 if h % 2 == 0 else h // 2


def _rows_cat(parts):
    return parts[0] if len(parts) == 1 else jnp.concatenate(parts, axis=0)


def _attn_kernel(q_ref, k_ref, v_ref, kh_ref, vh_ref, bias_ref, o_ref, m_ref, l_ref, *scratch,
                 split, interleave):
    nb = ATT_BLOCK
    n_planes, rows, _ = q_ref.shape
    piece = nb // split
    first_step = pl.program_id(2) == 0
    first_table = jnp.where(first_step, 1, 0)
    if interleave:
        o_stage, m_tiles, l_tiles = scratch

    lo = lax.broadcasted_iota(jnp.int32, (nb, LANES), 1) < HEAD_DIM
    lo_kv = lax.broadcasted_iota(jnp.int32, (2 * nb, LANES), 1) < HEAD_DIM

    for grp in range(n_planes // split):
        planes = range(grp * split, (grp + 1) * split)
        for u in range(rows // piece):
            cur = slice(u * piece, (u + 1) * piece)
            tile = grp * (rows // piece) + u
            out_rows = pl.ds(interleave * nb * u + grp, nb, stride=interleave) if interleave else None

            def block_rows(ref, cols):
                return _rows_cat([ref[p, cur, cols] for p in planes])

            def window_rows(ref, halo_ref, cols):
                parts = []
                for p in planes:
                    if u == 0:
                        parts += [halo_ref[p, :, cols], ref[p, cur, cols]]
                    else:
                        parts.append(ref[p, (u - 1) * piece:(u + 1) * piece, cols])
                return _rows_cat(parts)

            def put_stat(ref, tiles, lane_cols, value):
                if interleave:
                    tiles[tile, :, lane_cols] = value
                else:
                    for n, p in enumerate(planes):
                        ref[p, cur, lane_cols] = value[n * piece:(n + 1) * piece]

            put_stat(m_ref, m_tiles if interleave else None, slice(None), jnp.zeros((nb, LANES), F32))
            put_stat(l_ref, l_tiles if interleave else None, slice(None), jnp.ones((nb, LANES), F32))
            for hp in range(N_HEADS // 2):
                cols = slice(hp * LANES, (hp + 1) * LANES)
                q_pair = block_rows(q_ref, cols)
                k_win = window_rows(k_ref, kh_ref, cols)
                v_win = window_rows(v_ref, vh_ref, cols)
                table = first_table if u == 0 else 0
                outs = []
                for sub in range(2):
                    h = 2 * hp + sub
                    own = lo if sub == 0 else jnp.logical_not(lo)
                    own_kv = lo_kv if sub == 0 else jnp.logical_not(lo_kv)
                    q_h = jnp.where(own, q_pair, jnp.zeros_like(q_pair))
                    sc = lax.dot_general(q_h, k_win, (((1,), (1,)), ((), ())),
                                         preferred_element_type=F32)
                    sc = sc + bias_ref[table, h]
                    m = jnp.max(sc, axis=-1, keepdims=True)
                    p = jnp.exp2(sc - m).astype(BF16)
                    v_h = jnp.where(own_kv, v_win, jnp.ones_like(v_win))
                    pv = jnp.dot(p, v_h, preferred_element_type=F32)
                    outs.append(pv)
                    sl = slice(_stat_lane(h), _stat_lane(h) + 1)
                    put_stat(m_ref, m_tiles if interleave else None, sl, m)
                    put_stat(l_ref, l_tiles if interleave else None, sl, pv[:, sl])
                o_pair = jnp.where(lo, outs[0], outs[1])
                if interleave:
                    o_stage[hp, out_rows, :] = o_pair
                else:
                    for n, p in enumerate(planes):
                        o_ref[p, cur, cols] = o_pair[n * piece:(n + 1) * piece].astype(BF16)
            if interleave:
                m_ref[out_rows, :] = m_tiles[tile]
                l_ref[out_rows, :] = l_tiles[tile]
    if interleave:
        for hp in range(N_HEADS // 2):
            o_ref[:, hp * LANES:(hp + 1) * LANES] = o_stage[hp].astype(BF16)


def _attn_branch(q, k, v, bias, *, branch, split=1, interleave=0):
    b, n_planes, length, d = q.shape
    piece = ATT_BLOCK // split
    units = min(ATT_UNITS_PER_STEP, length // piece)
    step_planes = split * min(n_planes // split, ATT_UNITS_PER_STEP // units)
    rows = units * piece
    main = pl.BlockSpec((None, step_planes, rows, d), lambda bi, r, i: (bi, r, i, 0))
    halo = pl.BlockSpec((None, step_planes, piece, d),
                        lambda bi, r, i: (bi, r, jnp.maximum(i * units - 1, 0), 0))
    scratch = []
    if interleave:
        assert interleave == step_planes and rows == length and split == 1
        out_planes, out_len = n_planes // interleave, interleave * length
        o_spec = pl.BlockSpec((None, None, out_len, d), lambda bi, r, i: (bi, r, 0, 0))
        stat = pl.BlockSpec((None, None, out_len, LANES), lambda bi, r, i: (bi, r, 0, 0))
        n_tiles = step_planes * units
        scratch = [pltpu.VMEM((d // LANES, out_len, LANES), F32),
                   pltpu.VMEM((n_tiles, ATT_BLOCK, LANES), F32),
                   pltpu.VMEM((n_tiles, ATT_BLOCK, LANES), F32)]
    else:
        out_planes, out_len = n_planes, length
        o_spec = main
        stat = pl.BlockSpec((None, step_planes, rows, LANES), lambda bi, r, i: (bi, r, i, 0))
    stat_shape = jax.ShapeDtypeStruct((b, out_planes, out_len, LANES), F32)
    return pl.pallas_call(
        functools.partial(_attn_kernel, split=split, interleave=interleave),
        grid=(b, n_planes // step_planes, length // rows),
        in_specs=[main, main, main, halo, halo, _layer_spec(bias.shape[1:], branch)],
        out_specs=[o_spec, stat, stat],
        out_shape=[jax.ShapeDtypeStruct((b, out_planes, out_len, d), BF16), stat_shape, stat_shape],
        scratch_shapes=scratch,
        compiler_params=_params(3),
        name=f"attn_d{DILATIONS[branch]}",
    )(q, k, v, k, v, bias)


def _merge_kernel(x_ref, *refs):
    n_br = len(DILATIONS)
    branch_refs = [refs[3 * i:3 * i + 3] for i in range(n_br)]
    e_ref, wo_ref, out_ref, y_scr = refs[3 * n_br:]
    planes, tj, _ = branch_refs[0][0].shape

    def plane_major(ref):
        return jnp.concatenate([ref[p] for p in range(planes)], axis=0)

    ms = [plane_major(br[1]) for br in branch_refs]
    ls = [plane_major(br[2]) for br in branch_refs]
    m_max = functools.reduce(jnp.maximum, ms)
    es = [jnp.exp2(m - m_max) for m in ms]
    total = functools.reduce(jnp.add, [e * l for e, l in zip(es, ls)])
    merged = None
    for e, br in zip(es, branch_refs):
        c = e / total
        c_hi = c.astype(BF16)
        c_lo = (c - c_hi.astype(F32)).astype(BF16)
        c_wide = jnp.dot(jnp.concatenate([c_hi, c_lo], axis=-1), e_ref[...],
                         preferred_element_type=F32)
        term = c_wide * plane_major(br[0]).astype(F32)
        merged = term if merged is None else merged + term
    y = jnp.dot(merged.astype(BF16), wo_ref[...], preferred_element_type=F32)
    for slab in range(D_MODEL // LANES):
        cols = slice(slab * LANES, (slab + 1) * LANES)
        for p in range(planes):
            y_scr[slab, pl.ds(p, tj, stride=planes), :] = y[p * tj:(p + 1) * tj, cols]
        out_ref[0, :, cols] = x_ref[0, :, cols] + y_scr[slab]


def _merge_out_proj(x, branches, expand, w_o):
    b, s, d = x.shape
    ts = ROW_TILE
    planes = branches[0][0].shape[1]
    row_spec = pl.BlockSpec((1, ts, d), lambda i, j: (i, j, 0))
    o_spec = pl.BlockSpec((None, planes, ts // planes, d), lambda i, j: (i, 0, j, 0))
    stat_spec = pl.BlockSpec((None, planes, ts // planes, LANES), lambda i, j: (i, 0, j, 0))
    args = [t for br in branches for t in br]
    return pl.pallas_call(
        _merge_kernel,
        grid=(b, s // ts),
        in_specs=[row_spec] + [o_spec, stat_spec, stat_spec] * len(branches)
                 + [_const_spec(expand.shape), _const_spec((d, d))],
        out_specs=row_spec,
        out_shape=jax.ShapeDtypeStruct(x.shape, F32),
        scratch_shapes=[pltpu.VMEM((d // LANES, ts, LANES), F32)],
        compiler_params=_params(2),
        name="merge_out_proj",
    )(x, *args, expand, w_o)


def _t5_causal_bucket(dist):
    max_exact = N_REL_BUCKETS // 2
    dd = jnp.maximum(dist, 1).astype(F32)
    large = max_exact + (jnp.log(dd / max_exact) / math.log(REL_MAX_DISTANCE / max_exact)
                         * (N_REL_BUCKETS - max_exact)).astype(jnp.int32)
    large = jnp.minimum(large, N_REL_BUCKETS - 1)
    return jnp.where(dist < max_exact, dist, large)


def _bias_tables(rel_table, splits):
    n = ATT_BLOCK
    n_br = len(DILATIONS)
    n_heads = rel_table.shape[1]
    assert splits[0] in (1, 4) and all(s == 1 for s in splits[1:])
    dist = (n - jnp.arange(n + 1))[None, :] * jnp.asarray(DILATIONS)[:, None]
    by_offset = LOG2E * jnp.swapaxes(rel_table.astype(F32)[_t5_causal_bucket(dist)], 1, 2)

    def mask_fill(count):
        return jnp.full((n_heads, count), MASK_VALUE, F32)

    split = splits[0]
    rows = []
    for br in range(n_br):
        w = by_offset[br]
        if br > 0 or split == 1:
            rows.append(jnp.stack([jnp.concatenate([w, mask_fill(4 * n - (n + 1))], axis=1)] * 4, 1))
            continue
        groups = []
        for r in range(split):
            parts = []
            for rp in range(split):
                first = 0 if rp >= r else 1
                vals = w[:, split * first + rp - r::split]
                parts += [mask_fill(first), vals, mask_fill(n - first - vals.shape[1])]
            groups.append(jnp.concatenate(parts, axis=1))
        rows.append(jnp.stack(groups, axis=1))
    rows = jnp.stack(rows)
    return pl.pallas_call(
        functools.partial(_bias_kernel, split=split),
        out_shape=jax.ShapeDtypeStruct((n_br, 2, n_heads, n, 2 * n), F32),
        name="bias_tables",
    )(rows[:, :, :, None, :])


def _bias_kernel(rows_ref, out_ref, *, split):
    n = ATT_BLOCK
    n_br, _, n_heads = out_ref.shape[:3]
    col = lax.broadcasted_iota(jnp.int32, (n, 2 * n), 1)
    for br in range(n_br):
        sp = split if br == 0 else 1
        prev_cols = col % (2 * n // sp) < n // sp
        for h in range(n_heads):
            if sp == 1:
                wide = jnp.broadcast_to(rows_ref[br, h, 0], (n, 4 * n))
                tab = pltpu.roll(wide, 0, 1, stride=1, stride_axis=0)[:, :2 * n]
            else:
                groups = []
                for r in range(sp):
                    wide = jnp.broadcast_to(rows_ref[br, h, r], (n // sp, 4 * n))
                    rolled = pltpu.roll(wide, 0, 1, stride=1, stride_axis=0)
                    groups.append(jnp.concatenate(
                        [rolled[:, rp * n:rp * n + 2 * n // sp] for rp in range(sp)], axis=1))
                tab = jnp.concatenate(groups, axis=0)
            out_ref[br, 0, h] = tab
            out_ref[br, 1, h] = jnp.where(prev_cols, MASK_VALUE, tab)


def _head_expand_matrix():
    row = jnp.arange(2 * LANES)[:, None] % LANES
    head = jnp.arange(D_MODEL)[None, :] // HEAD_DIM
    stat_lane = jnp.where(head % 2 == 0, HEAD_DIM + head // 2, head // 2)
    return (row == stat_lane).astype(BF16)


def kernel(x, rel_bias, even_norm, even_w_in, even_conv_w, even_pool_w, even_pool_scale,
           even_w_out, odd_norm, odd_w_qkv, odd_q_norm, odd_k_norm, odd_w_o, ffn_norm,
           ffn_w_up, ffn_conv_w, ffn_conv_b, ffn_w_down):
    ffn_params = (ffn_norm[:, None], ffn_w_up.astype(BF16), ffn_conv_w, ffn_conv_b[:, None],
                  ffn_w_down.astype(BF16))

    def ffn(t, layer):
        return _ffn(t, layer, *ffn_params)

    x = _mixer(x, even_norm[0][None], even_w_in[0].astype(BF16), even_conv_w[0],
               even_pool_w[0].astype(BF16), even_pool_scale[0][None],
               even_w_out[0].astype(BF16))
    x = ffn(x, 0)

    gq_pair = jnp.tile(odd_q_norm[0], 2)[None]
    gk_pair = jnp.tile(odd_k_norm[0], 2)[None]
    qkv4, qkv16 = _qkv(x, odd_norm[0][None], odd_w_qkv[0].astype(BF16), gq_pair, gk_pair)
    bias = _bias_tables(rel_bias, splits=(4, 1, 1))
    branches = [
        _attn_branch(*qkv4, bias, branch=0, split=4),
        _attn_branch(*qkv4, bias, branch=1),
        _attn_branch(*qkv16, bias, branch=2, interleave=4),
    ]
    x = _merge_out_proj(x, branches, _head_expand_matrix(), odd_w_o[0].astype(BF16))
    x = ffn(x, 1)
    return x
```

```python
import functools
import math

import jax
import jax.numpy as jnp
import numpy as np
from jax import lax
from jax.experimental import pallas as pl
from jax.experimental.pallas import tpu as pltpu

D_MODEL = 1024
CONV_WIDTH = 3
A_WIDTH = 512
B_WIDTH = 512
POOL_WINDOWS = (2, 4, 8, 16)
POOL_GROUP = 128
EVEN_IN = 3 * A_WIDTH + B_WIDTH
HEAD_DIM = 64
N_HEADS = 16
DILATED_PAIRS = ((128, 1), (512, 4), (2048, 16))
DILATIONS = tuple(dil for _, dil in DILATED_PAIRS)
N_REL_BUCKETS = 32
REL_MAX_DISTANCE = 2048
D_FF = 2816
EPS = 1e-6
MASK_VALUE = -1e30

BF16 = jnp.bfloat16
F32 = jnp.float32

LANES = 128
ATT_BLOCK = 128
ATT_UNITS_PER_STEP = 8
LOG2E = math.log2(math.e)
ROW_TILE = 512
FFN_ROW_TILE = 1024
POOL_HALO = 16
CONV_HALO = 8
FF_CHUNK = 256
QKV_CHUNK = 256
MIX_CHUNK = 256
SHIFT_BASE = 8
VMEM_LIMIT = 56 * 1024 * 1024


def _const_spec(shape):
    nd = len(shape)
    return pl.BlockSpec(shape, lambda *_: (0,) * nd, pipeline_mode=pl.Buffered(1))


def _layer_spec(shape, layer):
    nd = len(shape)
    return pl.BlockSpec((None,) + tuple(shape), lambda *_: (layer,) + (0,) * nd,
                        pipeline_mode=pl.Buffered(1))


def _params(n_axes):
    return pltpu.CompilerParams(
        dimension_semantics=("arbitrary",) * n_axes, vmem_limit_bytes=VMEM_LIMIT)


def _rmsnorm(x, g):
    ms = jnp.mean(x * x, axis=-1, keepdims=True)
    return x * lax.rsqrt(ms + EPS) * g


def _stage_rows(scr, slot, base, tile, prev):
    n_prev = prev.shape[0]
    scr[slot, pl.ds(base - 2 * n_prev, n_prev, stride=2), :] = prev
    scr[slot, pl.ds(base, tile.shape[0], stride=2), :] = tile


def _rows_back(scr, slot, base, n_rows, k):
    return scr[slot, pl.ds(base - 2 * k, n_rows, stride=2), :]


def _mixer_kernel(x_ref, g_ref, win_ref, cw_ref, pw_ref, ps_ref, wout_ref, *refs):
    n_cast = (len(refs) - 5) // 2
    cast_src, o_ref, cast_dst = refs[:n_cast], refs[n_cast], refs[n_cast + 1:2 * n_cast + 1]
    y_buf, carry_a, carry_p, scr = refs[2 * n_cast + 1:]
    for src, dst in zip(cast_src, cast_dst):
        dst[...] = src[...].astype(BF16)
    ts = x_ref.shape[1]
    s = pl.program_id(1)
    base = 2 * POOL_HALO

    @pl.when(s == 0)
    def _():
        carry_a[...] = jnp.zeros(carry_a.shape, F32)
        carry_p[...] = jnp.zeros(carry_p.shape, F32)

    x = x_ref[0]
    xn = _rmsnorm(x, g_ref[...]).astype(BF16)
    slabs = MIX_CHUNK // LANES

    def proj(col0):
        return jnp.dot(xn, win_ref[:, col0:col0 + MIX_CHUNK], preferred_element_type=F32)

    for c in range(A_WIDTH // MIX_CHUNK):
        h = proj(c * MIX_CHUNK)
        gate_b = proj(A_WIDTH + c * MIX_CHUNK)
        gate_c = proj(2 * A_WIDTH + c * MIX_CHUNK)
        ch = gate_c * h
        for sl in range(slabs):
            slot = c * slabs + sl
            cols = slice(slot * LANES, (slot + 1) * LANES)
            ch_sl = ch[:, sl * LANES:(sl + 1) * LANES]
            _stage_rows(scr, slot, base, ch_sl, carry_a[CONV_HALO - 2:, cols])
            carry_a[:, cols] = ch_sl[ts - CONV_HALO:, :]
            cw = cw_ref[:, cols]
            conv = (cw[0:1] * _rows_back(scr, slot, base, ts, 2)
                    + cw[1:2] * _rows_back(scr, slot, base, ts, 1) + cw[2:3] * ch_sl)
            y_buf[:, cols] = (gate_b[:, sl * LANES:(sl + 1) * LANES] * conv).astype(BF16)

    pos = s * ts + lax.broadcasted_iota(jnp.int32, (ts, 1), 0)
    for c in range(B_WIDTH // MIX_CHUNK):
        pin = proj(3 * A_WIDTH + c * MIX_CHUNK)
        for sl in range(slabs):
            g = c * slabs + sl
            k = POOL_WINDOWS[g]
            slot = A_WIDTH // LANES + g
            cols = slice(g * POOL_GROUP, (g + 1) * POOL_GROUP)
            cur = pin[:, sl * LANES:(sl + 1) * LANES]
            _stage_rows(scr, slot, base, cur, carry_p[:, cols])
            carry_p[:, cols] = cur[ts - POOL_HALO:, :]
            acc = cur
            for j in range(1, k):
                acc = acc + _rows_back(scr, slot, base, ts, j)
            cnt = jnp.minimum(pos + 1, k).astype(F32)
            pooled = acc / cnt - cur
            yb = jnp.dot(pooled.astype(BF16), pw_ref[g], preferred_element_type=F32)
            y_buf[:, A_WIDTH + g * POOL_GROUP:A_WIDTH + (g + 1) * POOL_GROUP] = (
                yb * ps_ref[:, cols]).astype(BF16)

    o_ref[0] = x + jnp.dot(y_buf[...], wout_ref[...], preferred_element_type=F32)


def _mixer(x, g, w_in, conv_w, pool_w, pool_scale, w_out, to_cast):
    b, s, d = x.shape
    ts = FFN_ROW_TILE
    n_steps = b * (s // ts)
    row_spec = pl.BlockSpec((1, ts, d), lambda i, j: (i, j, 0))
    cast_specs = [pl.BlockSpec((w.shape[0] // n_steps, w.shape[1]),
                               lambda i, j: (i * (s // ts) + j, 0)) for w in to_cast]
    return pl.pallas_call(
        _mixer_kernel,
        grid=(b, s // ts),
        in_specs=[row_spec, _const_spec((1, d)), _const_spec((d, EVEN_IN)),
                  _const_spec((CONV_WIDTH, A_WIDTH)),
                  _const_spec((len(POOL_WINDOWS), POOL_GROUP, POOL_GROUP)),
                  _const_spec((1, B_WIDTH)), _const_spec((d, d))] + cast_specs,
        out_specs=[row_spec] + cast_specs,
        out_shape=[jax.ShapeDtypeStruct(x.shape, F32)]
                  + [jax.ShapeDtypeStruct(w.shape, BF16) for w in to_cast],
        scratch_shapes=[pltpu.VMEM((ts, d), BF16),
                        pltpu.VMEM((CONV_HALO, A_WIDTH), F32),
                        pltpu.VMEM((POOL_HALO, B_WIDTH), F32),
                        pltpu.VMEM(((A_WIDTH + B_WIDTH) // LANES, 2 * (POOL_HALO + ts), LANES),
                                   F32)],
        compiler_params=_params(2),
        name="mixer",
    )(x, g, w_in, conv_w, pool_w, pool_scale, w_out, *to_cast)


def _ffn_kernel(x_ref, g_ref, wup_ref, cw_ref, cb_ref, wdn_ref, o_ref, h_buf, carry, u_scr):
    ts = x_ref.shape[1]
    s = pl.program_id(1)

    @pl.when(s == 0)
    def _():
        carry[...] = jnp.zeros(carry.shape, F32)

    x = x_ref[0]
    xn = _rmsnorm(x, g_ref[...]).astype(BF16)
    slabs = FF_CHUNK // LANES
    for j in range(D_FF // FF_CHUNK):
        halves = []
        for half in range(2):
            c0 = half * D_FF + j * FF_CHUNK
            u = jnp.dot(xn, wup_ref[:, c0:c0 + FF_CHUNK], preferred_element_type=F32)
            parts = []
            for sl in range(slabs):
                cols = slice(c0 + sl * LANES, c0 + (sl + 1) * LANES)
                slot = ((j % 2) * 2 + half) * slabs + sl
                u_sl = u[:, sl * LANES:(sl + 1) * LANES]
                _stage_rows(u_scr, slot, SHIFT_BASE, u_sl, carry[CONV_HALO - 2:, cols])
                carry[:, cols] = u_sl[ts - CONV_HALO:ts, :]
                w = cw_ref[:, cols]
                parts.append(w[0:1] * _rows_back(u_scr, slot, SHIFT_BASE, ts, 2)
                             + w[1:2] * _rows_back(u_scr, slot, SHIFT_BASE, ts, 1)
                             + w[2:3] * u_sl + cb_ref[:, cols])
            halves.append(jnp.concatenate(parts, axis=-1))
        gate, up = halves
        h_buf[:, j * FF_CHUNK:(j + 1) * FF_CHUNK] = (gate * jax.nn.sigmoid(gate) * up).astype(BF16)
    o_ref[0] = x + jnp.dot(h_buf[...], wdn_ref[...], preferred_element_type=F32)


def _ffn(x, layer, g, w_up, conv_w, conv_b, w_down):
    b, s, d = x.shape
    ts = FFN_ROW_TILE
    row_spec = pl.BlockSpec((1, ts, d), lambda i, j: (i, j, 0))
    return pl.pallas_call(
        _ffn_kernel,
        grid=(b, s // ts),
        in_specs=[row_spec, _layer_spec((1, d), layer), _layer_spec((d, 2 * D_FF), layer),
                  _layer_spec((CONV_WIDTH, 2 * D_FF), layer),
                  _layer_spec((1, 2 * D_FF), layer), _layer_spec((D_FF, d), layer)],
        out_specs=row_spec,
        out_shape=jax.ShapeDtypeStruct(x.shape, F32),
        scratch_shapes=[pltpu.VMEM((ts, D_FF), BF16),
                        pltpu.VMEM((CONV_HALO, 2 * D_FF), F32),
                        pltpu.VMEM((4 * FF_CHUNK // LANES, SHIFT_BASE + 2 * ts, LANES), F32)],
        compiler_params=_params(2),
        name="ffn",
    )(x, g, w_up, conv_w, conv_b, w_down)


def _head_pair_norm(t, g_pair, scale):
    lane = lax.broadcasted_iota(jnp.int32, t.shape, 1)
    lo = lane < HEAD_DIM
    sq = t * t
    s_lo = jnp.sum(jnp.where(lo, sq, 0.0), axis=-1, keepdims=True)
    s_hi = jnp.sum(jnp.where(lo, 0.0, sq), axis=-1, keepdims=True)
    r = lax.rsqrt(jnp.where(lo, s_lo, s_hi) * (1.0 / HEAD_DIM) + EPS)
    return t * r * (g_pair * scale)


def _qkv_kernel(x_ref, g_ref, w_ref, gq_ref, gk_ref, *refs):
    assert DILATIONS == (1, 4, 16)
    out4, out16 = refs[0:3], refs[3:6]
    x_stage, stage = refs[6:]
    d = D_MODEL
    ts = x_ref.shape[1]
    tj = ts // 4
    for sl in range(d // LANES):
        x_stage[sl] = x_ref[0, :, sl * LANES:(sl + 1) * LANES]
    x = jnp.concatenate(
        [jnp.concatenate([x_stage[sl, pl.ds(r4, tj, stride=4), :] for sl in range(d // LANES)],
                         axis=-1) for r4 in range(4)], axis=0)
    xn = _rmsnorm(x, g_ref[...]).astype(BF16)
    slabs = QKV_CHUNK // LANES
    for c in range(3 * d // QKV_CHUNK):
        res = jnp.dot(xn, w_ref[:, c * QKV_CHUNK:(c + 1) * QKV_CHUNK],
                      preferred_element_type=F32)
        for sl in range(slabs):
            slot = c * slabs + sl
            which, hp = divmod(slot, d // LANES)
            cols = slice(hp * LANES, (hp + 1) * LANES)
            slab = res[:, sl * LANES:(sl + 1) * LANES]
            if which == 0:
                slab = _head_pair_norm(slab, gq_ref[...], HEAD_DIM ** -0.5 * LOG2E)
            elif which == 1:
                slab = _head_pair_norm(slab, gk_ref[...], 1.0)
            stage[slot] = slab
            for r4 in range(4):
                out4[which][r4, :, cols] = slab[r4 * tj:(r4 + 1) * tj].astype(BF16)
                for c4 in range(4):
                    p16 = stage[slot, pl.ds(r4 * tj + c4, tj // 4, stride=4), :]
                    out16[which][4 * r4 + c4, :, cols] = p16.astype(BF16)


def _qkv(x, g, w_qkv, gq_pair, gk_pair):
    b, s, d = x.shape
    ts = ROW_TILE
    row_spec = pl.BlockSpec((1, ts, d), lambda i, j: (i, j, 0))
    out_specs, out_shapes = [], []
    for planes in (4, 16):
        out_specs += [pl.BlockSpec((None, planes, ts // planes, d), lambda i, j: (i, 0, j, 0))] * 3
        out_shapes += [jax.ShapeDtypeStruct((b, planes, s // planes, d), BF16)] * 3
    outs = pl.pallas_call(
        _qkv_kernel,
        grid=(b, s // ts),
        in_specs=[row_spec, _const_spec((1, d)), _const_spec((d, 3 * d)),
                  _const_spec((1, LANES)), _const_spec((1, LANES))],
        out_specs=out_specs,
        out_shape=out_shapes,
        scratch_shapes=[pltpu.VMEM((d // LANES, ts, LANES), F32),
                        pltpu.VMEM((3 * d // LANES, ts, LANES), F32)],
        compiler_params=_params(2),
        name="qkv",
    )(x, g, w_qkv, gq_pair, gk_pair)
    return outs[0:3], outs[3:6]


def _stat_lane(h):
    return HEAD_DIM + h // 2 if h % 2 == 0 else h // 2


def _rows_cat(parts):
    return parts[0] if len(parts) == 1 else jnp.concatenate(parts, axis=0)


def _attn_kernel(q_ref, k_ref, v_ref, kh_ref, vh_ref, bias_ref, o_ref, m_ref, l_ref, *scratch,
                 split, interleave):
    nb = ATT_BLOCK
    n_planes, rows, _ = q_ref.shape
    piece = nb // split
    first_step = pl.program_id(2) == 0
    first_table = jnp.where(first_step, 1, 0)
    if interleave:
        o_stage, m_tiles, l_tiles = scratch

    lo = lax.broadcasted_iota(jnp.int32, (nb, LANES), 1) < HEAD_DIM
    lo_kv = lax.broadcasted_iota(jnp.int32, (2 * nb, LANES), 1) < HEAD_DIM

    for grp in range(n_planes // split):
        planes = range(grp * split, (grp + 1) * split)
        for u in range(rows // piece):
            cur = slice(u * piece, (u + 1) * piece)
            tile = grp * (rows // piece) + u
            out_rows = pl.ds(interleave * nb * u + grp, nb, stride=interleave) if interleave else None

            def block_rows(ref, cols):
                return _rows_cat([ref[p, cur, cols] for p in planes])

            def window_rows(ref, halo_ref, cols):
                parts = []
                for p in planes:
                    if u == 0:
                        parts += [halo_ref[p, :, cols], ref[p, cur, cols]]
                    else:
                        parts.append(ref[p, (u - 1) * piece:(u + 1) * piece, cols])
                return _rows_cat(parts)

            def put_stat(ref, tiles, lane_cols, value):
                if interleave:
                    tiles[tile, :, lane_cols] = value
                else:
                    for n, p in enumerate(planes):
                        ref[p, cur, lane_cols] = value[n * piece:(n + 1) * piece]

            put_stat(m_ref, m_tiles if interleave else None, slice(None), jnp.zeros((nb, LANES), F32))
            put_stat(l_ref, l_tiles if interleave else None, slice(None), jnp.ones((nb, LANES), F32))
            for hp in range(N_HEADS // 2):
                cols = slice(hp * LANES, (hp + 1) * LANES)
                q_pair = block_rows(q_ref, cols)
                k_win = window_rows(k_ref, kh_ref, cols)
                v_win = window_rows(v_ref, vh_ref, cols)
                table = first_table if u == 0 else 0
                outs = []
                for sub in range(2):
                    h = 2 * hp + sub
                    own = lo if sub == 0 else jnp.logical_not(lo)
                    own_kv = lo_kv if sub == 0 else jnp.logical_not(lo_kv)
                    q_h = jnp.where(own, q_pair, jnp.zeros_like(q_pair))
                    sc = lax.dot_general(q_h, k_win, (((1,), (1,)), ((), ())),
                                         preferred_element_type=F32)
                    sc = sc + bias_ref[table, h]
                    m = jnp.max(sc, axis=-1, keepdims=True)
                    p = jnp.exp2(sc - m).astype(BF16)
                    v_h = jnp.where(own_kv, v_win, jnp.ones_like(v_win))
                    pv = jnp.dot(p, v_h, preferred_element_type=F32)
                    outs.append(pv)
                    sl = slice(_stat_lane(h), _stat_lane(h) + 1)
                    put_stat(m_ref, m_tiles if interleave else None, sl, m)
                    put_stat(l_ref, l_tiles if interleave else None, sl, pv[:, sl])
                o_pair = jnp.where(lo, outs[0], outs[1])
                if interleave:
                    o_stage[hp, out_rows, :] = o_pair
                else:
                    for n, p in enumerate(planes):
                        o_ref[p, cur, cols] = o_pair[n * piece:(n + 1) * piece].astype(BF16)
            if interleave:
                m_ref[out_rows, :] = m_tiles[tile]
                l_ref[out_rows, :] = l_tiles[tile]
    if interleave:
        for hp in range(N_HEADS // 2):
            o_ref[:, hp * LANES:(hp + 1) * LANES] = o_stage[hp].astype(BF16)


def _attn_branch(q, k, v, bias, *, branch, split=1, interleave=0):
    b, n_planes, length, d = q.shape
    piece = ATT_BLOCK // split
    units = min(ATT_UNITS_PER_STEP, length // piece)
    step_planes = split * min(n_planes // split, ATT_UNITS_PER_STEP // units)
    rows = units * piece
    main = pl.BlockSpec((None, step_planes, rows, d), lambda bi, r, i: (bi, r, i, 0))
    halo = pl.BlockSpec((None, step_planes, piece, d),
                        lambda bi, r, i: (bi, r, jnp.maximum(i * units - 1, 0), 0))
    scratch = []
    if interleave:
        assert interleave == step_planes and rows == length and split == 1
        out_planes, out_len = n_planes // interleave, interleave * length
        o_spec = pl.BlockSpec((None, None, out_len, d), lambda bi, r, i: (bi, r, 0, 0))
        stat = pl.BlockSpec((None, None, out_len, LANES), lambda bi, r, i: (bi, r, 0, 0))
        n_tiles = step_planes * units
        scratch = [pltpu.VMEM((d // LANES, out_len, LANES), F32),
                   pltpu.VMEM((n_tiles, ATT_BLOCK, LANES), F32),
                   pltpu.VMEM((n_tiles, ATT_BLOCK, LANES), F32)]
    else:
        out_planes, out_len = n_planes, length
        o_spec = main
        stat = pl.BlockSpec((None, step_planes, rows, LANES), lambda bi, r, i: (bi, r, i, 0))
    stat_shape = jax.ShapeDtypeStruct((b, out_planes, out_len, LANES), F32)
    return pl.pallas_call(
        functools.partial(_attn_kernel, split=split, interleave=interleave),
        grid=(b, n_planes // step_planes, length // rows),
        in_specs=[main, main, main, halo, halo, _layer_spec(bias.shape[1:], branch)],
        out_specs=[o_spec, stat, stat],
        out_shape=[jax.ShapeDtypeStruct((b, out_planes, out_len, d), BF16), stat_shape, stat_shape],
        scratch_shapes=scratch,
        compiler_params=_params(3),
        name=f"attn_d{DILATIONS[branch]}",
    )(q, k, v, k, v, bias)


def _merge_kernel(x_ref, *refs):
    n_br = len(DILATIONS)
    branch_refs = [refs[3 * i:3 * i + 3] for i in range(n_br)]
    e_ref, wo_ref, out_ref, y_scr = refs[3 * n_br:]
    planes, tj, _ = branch_refs[0][0].shape

    def plane_major(ref):
        return jnp.concatenate([ref[p] for p in range(planes)], axis=0)

    ms = [plane_major(br[1]) for br in branch_refs]
    ls = [plane_major(br[2]) for br in branch_refs]
    m_max = functools.reduce(jnp.maximum, ms)
    es = [jnp.exp2(m - m_max) for m in ms]
    total = functools.reduce(jnp.add, [e * l for e, l in zip(es, ls)])
    merged = None
    for e, br in zip(es, branch_refs):
        c = e / total
        c_hi = c.astype(BF16)
        c_lo = (c - c_hi.astype(F32)).astype(BF16)
        c_wide = jnp.dot(jnp.concatenate([c_hi, c_lo], axis=-1), e_ref[...],
                         preferred_element_type=F32)
        term = c_wide * plane_major(br[0]).astype(F32)
        merged = term if merged is None else merged + term
    y = jnp.dot(merged.astype(BF16), wo_ref[...], preferred_element_type=F32)
    for slab in range(D_MODEL // LANES):
        cols = slice(slab * LANES, (slab + 1) * LANES)
        for p in range(planes):
            y_scr[slab, pl.ds(p, tj, stride=planes), :] = y[p * tj:(p + 1) * tj, cols]
        out_ref[0, :, cols] = x_ref[0, :, cols] + y_scr[slab]


def _merge_out_proj(x, branches, expand, w_o):
    b, s, d = x.shape
    ts = ROW_TILE
    planes = branches[0][0].shape[1]
    row_spec = pl.BlockSpec((1, ts, d), lambda i, j: (i, j, 0))
    o_spec = pl.BlockSpec((None, planes, ts // planes, d), lambda i, j: (i, 0, j, 0))
    stat_spec = pl.BlockSpec((None, planes, ts // planes, LANES), lambda i, j: (i, 0, j, 0))
    args = [t for br in branches for t in br]
    return pl.pallas_call(
        _merge_kernel,
        grid=(b, s // ts),
        in_specs=[row_spec] + [o_spec, stat_spec, stat_spec] * len(branches)
                 + [_const_spec(expand.shape), _const_spec((d, d))],
        out_specs=row_spec,
        out_shape=jax.ShapeDtypeStruct(x.shape, F32),
        scratch_shapes=[pltpu.VMEM((d // LANES, ts, LANES), F32)],
        compiler_params=_params(2),
        name="merge_out_proj",
    )(x, *args, expand, w_o)


def _t5_causal_bucket(dist):
    max_exact = N_REL_BUCKETS // 2
    dd = jnp.maximum(dist, 1).astype(F32)
    large = max_exact + (jnp.log(dd / max_exact) / math.log(REL_MAX_DISTANCE / max_exact)
                         * (N_REL_BUCKETS - max_exact)).astype(jnp.int32)
    large = jnp.minimum(large, N_REL_BUCKETS - 1)
    return jnp.where(dist < max_exact, dist, large)


def _bias_tables(rel_table, splits):
    n = ATT_BLOCK
    n_br = len(DILATIONS)
    n_heads = rel_table.shape[1]
    assert splits[0] in (1, 4) and all(s == 1 for s in splits[1:])
    dist = (n - jnp.arange(n + 1))[None, :] * jnp.asarray(DILATIONS)[:, None]
    by_offset = LOG2E * jnp.swapaxes(rel_table.astype(F32)[_t5_causal_bucket(dist)], 1, 2)

    def mask_fill(count):
        return jnp.full((n_heads, count), MASK_VALUE, F32)

    split = splits[0]
    rows = []
    for br in range(n_br):
        w = by_offset[br]
        if br > 0 or split == 1:
            rows.append(jnp.stack([jnp.concatenate([w, mask_fill(4 * n - (n + 1))], axis=1)] * 4, 1))
            continue
        groups = []
        for r in range(split):
            parts = []
            for rp in range(split):
                first = 0 if rp >= r else 1
                vals = w[:, split * first + rp - r::split]
                parts += [mask_fill(first), vals, mask_fill(n - first - vals.shape[1])]
            groups.append(jnp.concatenate(parts, axis=1))
        rows.append(jnp.stack(groups, axis=1))
    rows = jnp.stack(rows)
    return pl.pallas_call(
        functools.partial(_bias_kernel, split=split),
        out_shape=jax.ShapeDtypeStruct((n_br, 2, n_heads, n, 2 * n), F32),
        name="bias_tables",
    )(rows[:, :, :, None, :])


def _bias_kernel(rows_ref, out_ref, *, split):
    n = ATT_BLOCK
    n_br, _, n_heads = out_ref.shape[:3]
    col = lax.broadcasted_iota(jnp.int32, (n, 2 * n), 1)
    for br in range(n_br):
        sp = split if br == 0 else 1
        prev_cols = col % (2 * n // sp) < n // sp
        for h in range(n_heads):
            if sp == 1:
                wide = jnp.broadcast_to(rows_ref[br, h, 0], (n, 4 * n))
                tab = pltpu.roll(wide, 0, 1, stride=1, stride_axis=0)[:, :2 * n]
            else:
                groups = []
                for r in range(sp):
                    wide = jnp.broadcast_to(rows_ref[br, h, r], (n // sp, 4 * n))
                    rolled = pltpu.roll(wide, 0, 1, stride=1, stride_axis=0)
                    groups.append(jnp.concatenate(
                        [rolled[:, rp * n:rp * n + 2 * n // sp] for rp in range(sp)], axis=1))
                tab = jnp.concatenate(groups, axis=0)
            out_ref[br, 0, h] = tab
            out_ref[br, 1, h] = jnp.where(prev_cols, MASK_VALUE, tab)


def _head_expand_matrix():
    row = jnp.arange(2 * LANES)[:, None] % LANES
    head = jnp.arange(D_MODEL)[None, :] // HEAD_DIM
    stat_lane = jnp.where(head % 2 == 0, HEAD_DIM + head // 2, head // 2)
    return (row == stat_lane).astype(BF16)


def kernel(x, rel_bias, even_norm, even_w_in, even_conv_w, even_pool_w, even_pool_scale,
           even_w_out, odd_norm, odd_w_qkv, odd_q_norm, odd_k_norm, odd_w_o, ffn_norm,
           ffn_w_up, ffn_conv_w, ffn_conv_b, ffn_w_down):
    depth, d, d_ff2 = ffn_w_up.shape
    x, w_up, w_down, w_qkv, w_o = _mixer(
        x, even_norm[0][None], even_w_in[0].astype(BF16), even_conv_w[0],
        even_pool_w[0].astype(BF16), even_pool_scale[0][None], even_w_out[0].astype(BF16),
        to_cast=[ffn_w_up.reshape(depth * d, d_ff2), ffn_w_down.reshape(depth * D_FF, d),
                 odd_w_qkv[0], odd_w_o[0]])
    ffn_params = (ffn_norm[:, None], w_up.reshape(depth, d, d_ff2), ffn_conv_w,
                  ffn_conv_b[:, None], w_down.reshape(depth, D_FF, d))

    def ffn(t, layer):
        return _ffn(t, layer, *ffn_params)

    x = ffn(x, 0)

    gq_pair = jnp.tile(odd_q_norm[0], 2)[None]
    gk_pair = jnp.tile(odd_k_norm[0], 2)[None]
    qkv4, qkv16 = _qkv(x, odd_norm[0][None], w_qkv, gq_pair, gk_pair)
    bias = _bias_tables(rel_bias, splits=(4, 1, 1))
    branches = [
        _attn_branch(*qkv4, bias, branch=0, split=4),
        _attn_branch(*qkv4, bias, branch=1),
        _attn_branch(*qkv16, bias, branch=2, interleave=4),
    ]
    x = _merge_out_proj(x, branches, _head_expand_matrix(), w_o)
    x = ffn(x, 1)
    return x
```

```python
import functools
import math

import jax
import jax.numpy as jnp
import numpy as np
from jax import lax
from jax.experimental import pallas as pl
from jax.experimental.pallas import tpu as pltpu

D_MODEL = 1024
CONV_WIDTH = 3
A_WIDTH = 512
B_WIDTH = 512
POOL_WINDOWS = (2, 4, 8, 16)
POOL_GROUP = 128
EVEN_IN = 3 * A_WIDTH + B_WIDTH
HEAD_DIM = 64
N_HEADS = 16
DILATED_PAIRS = ((128, 1), (512, 4), (2048, 16))
DILATIONS = tuple(dil for _, dil in DILATED_PAIRS)
N_REL_BUCKETS = 32
REL_MAX_DISTANCE = 2048
D_FF = 2816
EPS = 1e-6
MASK_VALUE = -1e30

BF16 = jnp.bfloat16
F32 = jnp.float32

LANES = 128
ATT_BLOCK = 128
ATT_UNITS_PER_STEP = 16
LOG2E = math.log2(math.e)
ROW_TILE = 512
FFN_ROW_TILE = 1024
POOL_HALO = 16
CONV_HALO = 8
FF_CHUNK = 256
QKV_CHUNK = 256
MIX_CHUNK = 256
SHIFT_BASE = 8
VMEM_LIMIT = 56 * 1024 * 1024


def _const_spec(shape):
    nd = len(shape)
    return pl.BlockSpec(shape, lambda *_: (0,) * nd, pipeline_mode=pl.Buffered(1))


def _layer_spec(shape, layer):
    nd = len(shape)
    return pl.BlockSpec((None,) + tuple(shape), lambda *_: (layer,) + (0,) * nd,
                        pipeline_mode=pl.Buffered(1))


def _params(n_axes):
    return pltpu.CompilerParams(
        dimension_semantics=("arbitrary",) * n_axes, vmem_limit_bytes=VMEM_LIMIT)


def _rmsnorm(x, g):
    ms = jnp.mean(x * x, axis=-1, keepdims=True)
    return x * lax.rsqrt(ms + EPS) * g


def _stage_rows(scr, slot, base, tile, prev):
    n_prev = prev.shape[0]
    scr[slot, pl.ds(base - 2 * n_prev, n_prev, stride=2), :] = prev
    scr[slot, pl.ds(base, tile.shape[0], stride=2), :] = tile


def _rows_back(scr, slot, base, n_rows, k):
    return scr[slot, pl.ds(base - 2 * k, n_rows, stride=2), :]


def _mixer_kernel(x_ref, g_ref, win_ref, cw_ref, pw_ref, ps_ref, wout_ref, *refs):
    n_cast = (len(refs) - 5) // 2
    cast_src, o_ref, cast_dst = refs[:n_cast], refs[n_cast], refs[n_cast + 1:2 * n_cast + 1]
    y_buf, carry_a, carry_p, scr = refs[2 * n_cast + 1:]
    for src, dst in zip(cast_src, cast_dst):
        dst[...] = src[...].astype(BF16)
    ts = x_ref.shape[1]
    s = pl.program_id(1)
    base = 2 * POOL_HALO

    @pl.when(s == 0)
    def _():
        carry_a[...] = jnp.zeros(carry_a.shape, F32)
        carry_p[...] = jnp.zeros(carry_p.shape, F32)

    x = x_ref[0]
    xn = _rmsnorm(x, g_ref[...]).astype(BF16)
    slabs = MIX_CHUNK // LANES

    def proj(col0):
        return jnp.dot(xn, win_ref[:, col0:col0 + MIX_CHUNK], preferred_element_type=F32)

    for c in range(A_WIDTH // MIX_CHUNK):
        h = proj(c * MIX_CHUNK)
        gate_b = proj(A_WIDTH + c * MIX_CHUNK)
        gate_c = proj(2 * A_WIDTH + c * MIX_CHUNK)
        ch = gate_c * h
        for sl in range(slabs):
            slot = c * slabs + sl
            cols = slice(slot * LANES, (slot + 1) * LANES)
            ch_sl = ch[:, sl * LANES:(sl + 1) * LANES]
            _stage_rows(scr, slot, base, ch_sl, carry_a[CONV_HALO - 2:, cols])
            carry_a[:, cols] = ch_sl[ts - CONV_HALO:, :]
            cw = cw_ref[:, cols]
            conv = (cw[0:1] * _rows_back(scr, slot, base, ts, 2)
                    + cw[1:2] * _rows_back(scr, slot, base, ts, 1) + cw[2:3] * ch_sl)
            y_buf[:, cols] = (gate_b[:, sl * LANES:(sl + 1) * LANES] * conv).astype(BF16)

    pos = s * ts + lax.broadcasted_iota(jnp.int32, (ts, 1), 0)
    for c in range(B_WIDTH // MIX_CHUNK):
        pin = proj(3 * A_WIDTH + c * MIX_CHUNK)
        for sl in range(slabs):
            g = c * slabs + sl
            k = POOL_WINDOWS[g]
            slot = A_WIDTH // LANES + g
            cols = slice(g * POOL_GROUP, (g + 1) * POOL_GROUP)
            cur = pin[:, sl * LANES:(sl + 1) * LANES]
            _stage_rows(scr, slot, base, cur, carry_p[:, cols])
            carry_p[:, cols] = cur[ts - POOL_HALO:, :]
            acc = cur
            for j in range(1, k):
                acc = acc + _rows_back(scr, slot, base, ts, j)
            cnt = jnp.minimum(pos + 1, k).astype(F32)
            pooled = acc / cnt - cur
            yb = jnp.dot(pooled.astype(BF16), pw_ref[g], preferred_element_type=F32)
            y_buf[:, A_WIDTH + g * POOL_GROUP:A_WIDTH + (g + 1) * POOL_GROUP] = (
                yb * ps_ref[:, cols]).astype(BF16)

    o_ref[0] = x + jnp.dot(y_buf[...], wout_ref[...], preferred_element_type=F32)


def _mixer(x, g, w_in, conv_w, pool_w, pool_scale, w_out, to_cast):
    b, s, d = x.shape
    ts = FFN_ROW_TILE
    n_steps = b * (s // ts)
    row_spec = pl.BlockSpec((1, ts, d), lambda i, j: (i, j, 0))
    cast_specs = [pl.BlockSpec((w.shape[0] // n_steps, w.shape[1]),
                               lambda i, j: (i * (s // ts) + j, 0)) for w in to_cast]
    return pl.pallas_call(
        _mixer_kernel,
        grid=(b, s // ts),
        in_specs=[row_spec, _const_spec((1, d)), _const_spec((d, EVEN_IN)),
                  _const_spec((CONV_WIDTH, A_WIDTH)),
                  _const_spec((len(POOL_WINDOWS), POOL_GROUP, POOL_GROUP)),
                  _const_spec((1, B_WIDTH)), _const_spec((d, d))] + cast_specs,
        out_specs=[row_spec] + cast_specs,
        out_shape=[jax.ShapeDtypeStruct(x.shape, F32)]
                  + [jax.ShapeDtypeStruct(w.shape, BF16) for w in to_cast],
        scratch_shapes=[pltpu.VMEM((ts, d), BF16),
                        pltpu.VMEM((CONV_HALO, A_WIDTH), F32),
                        pltpu.VMEM((POOL_HALO, B_WIDTH), F32),
                        pltpu.VMEM(((A_WIDTH + B_WIDTH) // LANES, 2 * (POOL_HALO + ts), LANES),
                                   F32)],
        compiler_params=_params(2),
        name="mixer",
    )(x, g, w_in, conv_w, pool_w, pool_scale, w_out, *to_cast)


def _ffn_kernel(x_ref, g_ref, wup_ref, cw_ref, cb_ref, wdn_ref, o_ref, h_buf, carry, u_scr):
    ts = x_ref.shape[1]
    s = pl.program_id(1)

    @pl.when(s == 0)
    def _():
        carry[...] = jnp.zeros(carry.shape, F32)

    x = x_ref[0]
    xn = _rmsnorm(x, g_ref[...]).astype(BF16)
    slabs = FF_CHUNK // LANES
    for j in range(D_FF // FF_CHUNK):
        halves = []
        for half in range(2):
            c0 = half * D_FF + j * FF_CHUNK
            u = jnp.dot(xn, wup_ref[:, c0:c0 + FF_CHUNK], preferred_element_type=F32)
            parts = []
            for sl in range(slabs):
                cols = slice(c0 + sl * LANES, c0 + (sl + 1) * LANES)
                slot = ((j % 2) * 2 + half) * slabs + sl
                u_sl = u[:, sl * LANES:(sl + 1) * LANES]
                _stage_rows(u_scr, slot, SHIFT_BASE, u_sl, carry[CONV_HALO - 2:, cols])
                carry[:, cols] = u_sl[ts - CONV_HALO:ts, :]
                w = cw_ref[:, cols]
                parts.append(w[0:1] * _rows_back(u_scr, slot, SHIFT_BASE, ts, 2)
                             + w[1:2] * _rows_back(u_scr, slot, SHIFT_BASE, ts, 1)
                             + w[2:3] * u_sl + cb_ref[:, cols])
            halves.append(jnp.concatenate(parts, axis=-1))
        gate, up = halves
        h_buf[:, j * FF_CHUNK:(j + 1) * FF_CHUNK] = (gate * jax.nn.sigmoid(gate) * up).astype(BF16)
    o_ref[0] = x + jnp.dot(h_buf[...], wdn_ref[...], preferred_element_type=F32)


def _ffn(x, layer, g, w_up, conv_w, conv_b, w_down):
    b, s, d = x.shape
    ts = FFN_ROW_TILE
    row_spec = pl.BlockSpec((1, ts, d), lambda i, j: (i, j, 0))
    return pl.pallas_call(
        _ffn_kernel,
        grid=(b, s // ts),
        in_specs=[row_spec, _layer_spec((1, d), layer), _layer_spec((d, 2 * D_FF), layer),
                  _layer_spec((CONV_WIDTH, 2 * D_FF), layer),
                  _layer_spec((1, 2 * D_FF), layer), _layer_spec((D_FF, d), layer)],
        out_specs=row_spec,
        out_shape=jax.ShapeDtypeStruct(x.shape, F32),
        scratch_shapes=[pltpu.VMEM((ts, D_FF), BF16),
                        pltpu.VMEM((CONV_HALO, 2 * D_FF), F32),
                        pltpu.VMEM((4 * FF_CHUNK // LANES, SHIFT_BASE + 2 * ts, LANES), F32)],
        compiler_params=_params(2),
        name="ffn",
    )(x, g, w_up, conv_w, conv_b, w_down)


def _head_pair_norm(t, g_pair, scale):
    lane = lax.broadcasted_iota(jnp.int32, t.shape, 1)
    lo = lane < HEAD_DIM
    sq = t * t
    s_lo = jnp.sum(jnp.where(lo, sq, 0.0), axis=-1, keepdims=True)
    s_hi = jnp.sum(jnp.where(lo, 0.0, sq), axis=-1, keepdims=True)
    r = lax.rsqrt(jnp.where(lo, s_lo, s_hi) * (1.0 / HEAD_DIM) + EPS)
    return t * r * (g_pair * scale)


def _qkv_kernel(x_ref, g_ref, w_ref, gq_ref, gk_ref, *refs):
    assert DILATIONS == (1, 4, 16)
    out4, out16 = refs[0:3], refs[3:6]
    x_stage, stage = refs[6:]
    d = D_MODEL
    ts = x_ref.shape[1]
    tj = ts // 4
    for sl in range(d // LANES):
        x_stage[sl] = x_ref[0, :, sl * LANES:(sl + 1) * LANES]
    x = jnp.concatenate(
        [jnp.concatenate([x_stage[sl, pl.ds(r4, tj, stride=4), :] for sl in range(d // LANES)],
                         axis=-1) for r4 in range(4)], axis=0)
    xn = _rmsnorm(x, g_ref[...]).astype(BF16)
    slabs = QKV_CHUNK // LANES
    for c in range(3 * d // QKV_CHUNK):
        res = jnp.dot(xn, w_ref[:, c * QKV_CHUNK:(c + 1) * QKV_CHUNK],
                      preferred_element_type=F32)
        for sl in range(slabs):
            slot = c * slabs + sl
            which, hp = divmod(slot, d // LANES)
            cols = slice(hp * LANES, (hp + 1) * LANES)
            slab = res[:, sl * LANES:(sl + 1) * LANES]
            if which == 0:
                slab = _head_pair_norm(slab, gq_ref[...], HEAD_DIM ** -0.5 * LOG2E)
            elif which == 1:
                slab = _head_pair_norm(slab, gk_ref[...], 1.0)
            stage[slot] = slab
            for r4 in range(4):
                out4[which][r4, :, cols] = slab[r4 * tj:(r4 + 1) * tj].astype(BF16)
                for c4 in range(4):
                    p16 = stage[slot, pl.ds(r4 * tj + c4, tj // 4, stride=4), :]
                    out16[which][4 * r4 + c4, :, cols] = p16.astype(BF16)


def _qkv(x, g, w_qkv, gq_pair, gk_pair):
    b, s, d = x.shape
    ts = ROW_TILE
    row_spec = pl.BlockSpec((1, ts, d), lambda i, j: (i, j, 0))
    out_specs, out_shapes = [], []
    for planes in (4, 16):
        out_specs += [pl.BlockSpec((None, planes, ts // planes, d), lambda i, j: (i, 0, j, 0))] * 3
        out_shapes += [jax.ShapeDtypeStruct((b, planes, s // planes, d), BF16)] * 3
    outs = pl.pallas_call(
        _qkv_kernel,
        grid=(b, s // ts),
        in_specs=[row_spec, _const_spec((1, d)), _const_spec((d, 3 * d)),
                  _const_spec((1, LANES)), _const_spec((1, LANES))],
        out_specs=out_specs,
        out_shape=out_shapes,
        scratch_shapes=[pltpu.VMEM((d // LANES, ts, LANES), F32),
                        pltpu.VMEM((3 * d // LANES, ts, LANES), F32)],
        compiler_params=_params(2),
        name="qkv",
    )(x, g, w_qkv, gq_pair, gk_pair)
    return outs[0:3], outs[3:6]


def _stat_lane(h):
    return HEAD_DIM + h // 2 if h % 2 == 0 else h // 2


def _rows_cat(parts):
    return parts[0] if len(parts) == 1 else jnp.concatenate(parts, axis=0)


def _attn_kernel(q_ref, k_ref, v_ref, kh_ref, vh_ref, bias_ref, o_ref, m_ref, l_ref, *scratch,
                 split, interleave):
    nb = ATT_BLOCK
    n_planes, rows, _ = q_ref.shape
    piece = nb // split
    first_step = pl.program_id(2) == 0
    first_table = jnp.where(first_step, 1, 0)
    if interleave:
        o_stage, m_tiles, l_tiles = scratch

    lo = lax.broadcasted_iota(jnp.int32, (nb, LANES), 1) < HEAD_DIM
    lo_kv = lax.broadcasted_iota(jnp.int32, (2 * nb, LANES), 1) < HEAD_DIM

    for grp in range(n_planes // split):
        planes = range(grp * split, (grp + 1) * split)
        for u in range(rows // piece):
            cur = slice(u * piece, (u + 1) * piece)
            tile = grp * (rows // piece) + u
            out_rows = pl.ds(interleave * nb * u + grp, nb, stride=interleave) if interleave else None

            def block_rows(ref, cols):
                return _rows_cat([ref[p, cur, cols] for p in planes])

            def window_rows(ref, halo_ref, cols):
                parts = []
                for p in planes:
                    if u == 0:
                        parts += [halo_ref[p, :, cols], ref[p, cur, cols]]
                    else:
                        parts.append(ref[p, (u - 1) * piece:(u + 1) * piece, cols])
                return _rows_cat(parts)

            def put_stat(ref, tiles, lane_cols, value):
                if interleave:
                    tiles[tile, :, lane_cols] = value
                else:
                    for n, p in enumerate(planes):
                        ref[p, cur, lane_cols] = value[n * piece:(n + 1) * piece]

            put_stat(m_ref, m_tiles if interleave else None, slice(None), jnp.zeros((nb, LANES), F32))
            put_stat(l_ref, l_tiles if interleave else None, slice(None), jnp.ones((nb, LANES), F32))
            for hp in range(N_HEADS // 2):
                cols = slice(hp * LANES, (hp + 1) * LANES)
                q_pair = block_rows(q_ref, cols)
                k_win = window_rows(k_ref, kh_ref, cols)
                v_win = window_rows(v_ref, vh_ref, cols)
                table = first_table if u == 0 else 0
                outs = []
                for sub in range(2):
                    h = 2 * hp + sub
                    own = lo if sub == 0 else jnp.logical_not(lo)
                    own_kv = lo_kv if sub == 0 else jnp.logical_not(lo_kv)
                    q_h = jnp.where(own, q_pair, jnp.zeros_like(q_pair))
                    sc = lax.dot_general(q_h, k_win, (((1,), (1,)), ((), ())),
                                         preferred_element_type=F32)
                    sc = sc + bias_ref[table, h]
                    m = jnp.max(sc, axis=-1, keepdims=True)
                    p = jnp.exp2(sc - m).astype(BF16)
                    v_h = jnp.where(own_kv, v_win, jnp.ones_like(v_win))
                    pv = jnp.dot(p, v_h, preferred_element_type=F32)
                    outs.append(pv)
                    sl = slice(_stat_lane(h), _stat_lane(h) + 1)
                    put_stat(m_ref, m_tiles if interleave else None, sl, m)
                    put_stat(l_ref, l_tiles if interleave else None, sl, pv[:, sl])
                o_pair = jnp.where(lo, outs[0], outs[1])
                if interleave:
                    o_stage[hp, out_rows, :] = o_pair
                else:
                    for n, p in enumerate(planes):
                        o_ref[p, cur, cols] = o_pair[n * piece:(n + 1) * piece].astype(BF16)
            if interleave:
                m_ref[out_rows, :] = m_tiles[tile]
                l_ref[out_rows, :] = l_tiles[tile]
    if interleave:
        for hp in range(N_HEADS // 2):
            o_ref[:, hp * LANES:(hp + 1) * LANES] = o_stage[hp].astype(BF16)


def _attn_branch(q, k, v, bias, *, branch, split=1, interleave=0):
    b, n_planes, length, d = q.shape
    piece = ATT_BLOCK // split
    units = min(ATT_UNITS_PER_STEP, length // piece)
    step_planes = interleave or split * min(n_planes // split, ATT_UNITS_PER_STEP // units)
    rows = units * piece
    main = pl.BlockSpec((None, step_planes, rows, d), lambda bi, r, i: (bi, r, i, 0))
    halo = pl.BlockSpec((None, step_planes, piece, d),
                        lambda bi, r, i: (bi, r, jnp.maximum(i * units - 1, 0), 0))
    scratch = []
    if interleave:
        assert interleave == step_planes and rows == length and split == 1
        out_planes, out_len = n_planes // interleave, interleave * length
        o_spec = pl.BlockSpec((None, None, out_len, d), lambda bi, r, i: (bi, r, 0, 0))
        stat = pl.BlockSpec((None, None, out_len, LANES), lambda bi, r, i: (bi, r, 0, 0))
        n_tiles = step_planes * units
        scratch = [pltpu.VMEM((d // LANES, out_len, LANES), F32),
                   pltpu.VMEM((n_tiles, ATT_BLOCK, LANES), F32),
                   pltpu.VMEM((n_tiles, ATT_BLOCK, LANES), F32)]
    else:
        out_planes, out_len = n_planes, length
        o_spec = main
        stat = pl.BlockSpec((None, step_planes, rows, LANES), lambda bi, r, i: (bi, r, i, 0))
    stat_shape = jax.ShapeDtypeStruct((b, out_planes, out_len, LANES), F32)
    return pl.pallas_call(
        functools.partial(_attn_kernel, split=split, interleave=interleave),
        grid=(b, n_planes // step_planes, length // rows),
        in_specs=[main, main, main, halo, halo, _layer_spec(bias.shape[1:], branch)],
        out_specs=[o_spec, stat, stat],
        out_shape=[jax.ShapeDtypeStruct((b, out_planes, out_len, d), BF16), stat_shape, stat_shape],
        scratch_shapes=scratch,
        compiler_params=_params(3),
        name=f"attn_d{DILATIONS[branch]}",
    )(q, k, v, k, v, bias)


def _merge_kernel(x_ref, *refs):
    n_br = len(DILATIONS)
    branch_refs = [refs[3 * i:3 * i + 3] for i in range(n_br)]
    e_ref, wo_ref, out_ref, y_scr = refs[3 * n_br:]
    planes, tj, _ = branch_refs[0][0].shape

    def plane_major(ref):
        return jnp.concatenate([ref[p] for p in range(planes)], axis=0)

    ms = [plane_major(br[1]) for br in branch_refs]
    ls = [plane_major(br[2]) for br in branch_refs]
    m_max = functools.reduce(jnp.maximum, ms)
    es = [jnp.exp2(m - m_max) for m in ms]
    total = functools.reduce(jnp.add, [e * l for e, l in zip(es, ls)])
    merged = None
    for e, br in zip(es, branch_refs):
        c = e / total
        c_hi = c.astype(BF16)
        c_lo = (c - c_hi.astype(F32)).astype(BF16)
        c_wide = jnp.dot(jnp.concatenate([c_hi, c_lo], axis=-1), e_ref[...],
                         preferred_element_type=F32)
        term = c_wide * plane_major(br[0]).astype(F32)
        merged = term if merged is None else merged + term
    y = jnp.dot(merged.astype(BF16), wo_ref[...], preferred_element_type=F32)
    for slab in range(D_MODEL // LANES):
        cols = slice(slab * LANES, (slab + 1) * LANES)
        for p in range(planes):
            y_scr[slab, pl.ds(p, tj, stride=planes), :] = y[p * tj:(p + 1) * tj, cols]
        out_ref[0, :, cols] = x_ref[0, :, cols] + y_scr[slab]


def _merge_out_proj(x, branches, expand, w_o):
    b, s, d = x.shape
    ts = ROW_TILE
    planes = branches[0][0].shape[1]
    row_spec = pl.BlockSpec((1, ts, d), lambda i, j: (i, j, 0))
    o_spec = pl.BlockSpec((None, planes, ts // planes, d), lambda i, j: (i, 0, j, 0))
    stat_spec = pl.BlockSpec((None, planes, ts // planes, LANES), lambda i, j: (i, 0, j, 0))
    args = [t for br in branches for t in br]
    return pl.pallas_call(
        _merge_kernel,
        grid=(b, s // ts),
        in_specs=[row_spec] + [o_spec, stat_spec, stat_spec] * len(branches)
                 + [_const_spec(expand.shape), _const_spec((d, d))],
        out_specs=row_spec,
        out_shape=jax.ShapeDtypeStruct(x.shape, F32),
        scratch_shapes=[pltpu.VMEM((d // LANES, ts, LANES), F32)],
        compiler_params=_params(2),
        name="merge_out_proj",
    )(x, *args, expand, w_o)


def _t5_causal_bucket(dist):
    max_exact = N_REL_BUCKETS // 2
    dd = jnp.maximum(dist, 1).astype(F32)
    large = max_exact + (jnp.log(dd / max_exact) / math.log(REL_MAX_DISTANCE / max_exact)
                         * (N_REL_BUCKETS - max_exact)).astype(jnp.int32)
    large = jnp.minimum(large, N_REL_BUCKETS - 1)
    return jnp.where(dist < max_exact, dist, large)


def _bias_tables(rel_table, splits):
    n = ATT_BLOCK
    n_br = len(DILATIONS)
    n_heads = rel_table.shape[1]
    assert splits[0] in (1, 4) and all(s == 1 for s in splits[1:])
    dist = (n - jnp.arange(n + 1))[None, :] * jnp.asarray(DILATIONS)[:, None]
    by_offset = LOG2E * jnp.swapaxes(rel_table.astype(F32)[_t5_causal_bucket(dist)], 1, 2)

    def mask_fill(count):
        return jnp.full((n_heads, count), MASK_VALUE, F32)

    split = splits[0]
    rows = []
    for br in range(n_br):
        w = by_offset[br]
        if br > 0 or split == 1:
            rows.append(jnp.stack([jnp.concatenate([w, mask_fill(4 * n - (n + 1))], axis=1)] * 4, 1))
            continue
        groups = []
        for r in range(split):
            parts = []
            for rp in range(split):
                first = 0 if rp >= r else 1
                vals = w[:, split * first + rp - r::split]
                parts += [mask_fill(first), vals, mask_fill(n - first - vals.shape[1])]
            groups.append(jnp.concatenate(parts, axis=1))
        rows.append(jnp.stack(groups, axis=1))
    rows = jnp.stack(rows)
    return pl.pallas_call(
        functools.partial(_bias_kernel, split=split),
        out_shape=jax.ShapeDtypeStruct((n_br, 2, n_heads, n, 2 * n), F32),
        name="bias_tables",
    )(rows[:, :, :, None, :])


def _bias_kernel(rows_ref, out_ref, *, split):
    n = ATT_BLOCK
    n_br, _, n_heads = out_ref.shape[:3]
    col = lax.broadcasted_iota(jnp.int32, (n, 2 * n), 1)
    for br in range(n_br):
        sp = split if br == 0 else 1
        prev_cols = col % (2 * n // sp) < n // sp
        for h in range(n_heads):
            if sp == 1:
                wide = jnp.broadcast_to(rows_ref[br, h, 0], (n, 4 * n))
                tab = pltpu.roll(wide, 0, 1, stride=1, stride_axis=0)[:, :2 * n]
            else:
                groups = []
                for r in range(sp):
                    wide = jnp.broadcast_to(rows_ref[br, h, r], (n // sp, 4 * n))
                    rolled = pltpu.roll(wide, 0, 1, stride=1, stride_axis=0)
                    groups.append(jnp.concatenate(
                        [rolled[:, rp * n:rp * n + 2 * n // sp] for rp in range(sp)], axis=1))
                tab = jnp.concatenate(groups, axis=0)
            out_ref[br, 0, h] = tab
            out_ref[br, 1, h] = jnp.where(prev_cols, MASK_VALUE, tab)


def _head_expand_matrix():
    row = jnp.arange(2 * LANES)[:, None] % LANES
    head = jnp.arange(D_MODEL)[None, :] // HEAD_DIM
    stat_lane = jnp.where(head % 2 == 0, HEAD_DIM + head // 2, head // 2)
    return (row == stat_lane).astype(BF16)


def kernel(x, rel_bias, even_norm, even_w_in, even_conv_w, even_pool_w, even_pool_scale,
           even_w_out, odd_norm, odd_w_qkv, odd_q_norm, odd_k_norm, odd_w_o, ffn_norm,
           ffn_w_up, ffn_conv_w, ffn_conv_b, ffn_w_down):
    depth, d, d_ff2 = ffn_w_up.shape
    x, w_up, w_down, w_qkv, w_o = _mixer(
        x, even_norm[0][None], even_w_in[0].astype(BF16), even_conv_w[0],
        even_pool_w[0].astype(BF16), even_pool_scale[0][None], even_w_out[0].astype(BF16),
        to_cast=[ffn_w_up.reshape(depth * d, d_ff2), ffn_w_down.reshape(depth * D_FF, d),
                 odd_w_qkv[0], odd_w_o[0]])
    ffn_params = (ffn_norm[:, None], w_up.reshape(depth, d, d_ff2), ffn_conv_w,
                  ffn_conv_b[:, None], w_down.reshape(depth, D_FF, d))

    def ffn(t, layer):
        return _ffn(t, layer, *ffn_params)

    x = ffn(x, 0)

    gq_pair = jnp.tile(odd_q_norm[0], 2)[None]
    gk_pair = jnp.tile(odd_k_norm[0], 2)[None]
    qkv4, qkv16 = _qkv(x, odd_norm[0][None], w_qkv, gq_pair, gk_pair)
    bias = _bias_tables(rel_bias, splits=(4, 1, 1))
    branches = [
        _attn_branch(*qkv4, bias, branch=0, split=4),
        _attn_branch(*qkv4, bias, branch=1),
        _attn_branch(*qkv16, bias, branch=2, interleave=4),
    ]
    x = _merge_out_proj(x, branches, _head_expand_matrix(), w_o)
    x = ffn(x, 1)
    return x
```

```python
import functools
import math

import jax
import jax.numpy as jnp
import numpy as np
from jax import lax
from jax.experimental import pallas as pl
from jax.experimental.pallas import tpu as pltpu

D_MODEL = 1024
CONV_WIDTH = 3
A_WIDTH = 512
B_WIDTH = 512
POOL_WINDOWS = (2, 4, 8, 16)
POOL_GROUP = 128
EVEN_IN = 3 * A_WIDTH + B_WIDTH
HEAD_DIM = 64
N_HEADS = 16
DILATED_PAIRS = ((128, 1), (512, 4), (2048, 16))
DILATIONS = tuple(dil for _, dil in DILATED_PAIRS)
N_REL_BUCKETS = 32
REL_MAX_DISTANCE = 2048
D_FF = 2816
EPS = 1e-6
MASK_VALUE = -1e30

BF16 = jnp.bfloat16
F32 = jnp.float32

LANES = 128
ATT_BLOCK = 128
ATT_UNITS_PER_STEP = 16
LOG2E = math.log2(math.e)
QKV_ROW_TILE = 512
ROW_TILE = 1024
POOL_HALO = 16
CONV_HALO = 8
FF_CHUNK = 256
QKV_CHUNK = 256
MIX_CHUNK = 256
SHIFT_BASE = 8
VMEM_LIMIT = 56 * 1024 * 1024


def _const_spec(shape):
    nd = len(shape)
    return pl.BlockSpec(shape, lambda *_: (0,) * nd, pipeline_mode=pl.Buffered(1))


def _layer_spec(shape, layer):
    nd = len(shape)
    return pl.BlockSpec((None,) + tuple(shape), lambda *_: (layer,) + (0,) * nd,
                        pipeline_mode=pl.Buffered(1))


def _params(n_axes):
    return pltpu.CompilerParams(
        dimension_semantics=("arbitrary",) * n_axes, vmem_limit_bytes=VMEM_LIMIT)


def _rmsnorm(x, g):
    ms = jnp.mean(x * x, axis=-1, keepdims=True)
    return x * lax.rsqrt(ms + EPS) * g


def _stage_rows(scr, slot, base, tile, prev):
    n_prev = prev.shape[0]
    scr[slot, pl.ds(base - 2 * n_prev, n_prev, stride=2), :] = prev
    scr[slot, pl.ds(base, tile.shape[0], stride=2), :] = tile


def _rows_back(scr, slot, base, n_rows, k):
    return scr[slot, pl.ds(base - 2 * k, n_rows, stride=2), :]


def _mixer_kernel(x_ref, g_ref, win_ref, cw_ref, pw_ref, ps_ref, wout_ref, *refs):
    n_cast = (len(refs) - 5) // 2
    cast_src, o_ref, cast_dst = refs[:n_cast], refs[n_cast], refs[n_cast + 1:2 * n_cast + 1]
    y_buf, carry_a, carry_p, scr = refs[2 * n_cast + 1:]
    for src, dst in zip(cast_src, cast_dst):
        dst[...] = src[...].astype(BF16)
    ts = x_ref.shape[1]
    s = pl.program_id(1)
    base = 2 * POOL_HALO

    @pl.when(s == 0)
    def _():
        carry_a[...] = jnp.zeros(carry_a.shape, F32)
        carry_p[...] = jnp.zeros(carry_p.shape, F32)

    x = x_ref[0]
    xn = _rmsnorm(x, g_ref[...]).astype(BF16)
    slabs = MIX_CHUNK // LANES

    def proj(col0):
        return jnp.dot(xn, win_ref[:, col0:col0 + MIX_CHUNK], preferred_element_type=F32)

    for c in range(A_WIDTH // MIX_CHUNK):
        h = proj(c * MIX_CHUNK)
        gate_b = proj(A_WIDTH + c * MIX_CHUNK)
        gate_c = proj(2 * A_WIDTH + c * MIX_CHUNK)
        ch = gate_c * h
        for sl in range(slabs):
            slot = c * slabs + sl
            cols = slice(slot * LANES, (slot + 1) * LANES)
            ch_sl = ch[:, sl * LANES:(sl + 1) * LANES]
            _stage_rows(scr, slot, base, ch_sl, carry_a[CONV_HALO - 2:, cols])
            carry_a[:, cols] = ch_sl[ts - CONV_HALO:, :]
            cw = cw_ref[:, cols]
            conv = (cw[0:1] * _rows_back(scr, slot, base, ts, 2)
                    + cw[1:2] * _rows_back(scr, slot, base, ts, 1) + cw[2:3] * ch_sl)
            y_buf[:, cols] = (gate_b[:, sl * LANES:(sl + 1) * LANES] * conv).astype(BF16)

    pos = s * ts + lax.broadcasted_iota(jnp.int32, (ts, 1), 0)
    for c in range(B_WIDTH // MIX_CHUNK):
        pin = proj(3 * A_WIDTH + c * MIX_CHUNK)
        for sl in range(slabs):
            g = c * slabs + sl
            k = POOL_WINDOWS[g]
            slot = A_WIDTH // LANES + g
            cols = slice(g * POOL_GROUP, (g + 1) * POOL_GROUP)
            cur = pin[:, sl * LANES:(sl + 1) * LANES]
            _stage_rows(scr, slot, base, cur, carry_p[:, cols])
            carry_p[:, cols] = cur[ts - POOL_HALO:, :]
            acc = cur
            for j in range(1, k):
                acc = acc + _rows_back(scr, slot, base, ts, j)
            cnt = jnp.minimum(pos + 1, k).astype(F32)
            pooled = acc / cnt - cur
            yb = jnp.dot(pooled.astype(BF16), pw_ref[g], preferred_element_type=F32)
            y_buf[:, A_WIDTH + g * POOL_GROUP:A_WIDTH + (g + 1) * POOL_GROUP] = (
                yb * ps_ref[:, cols]).astype(BF16)

    o_ref[0] = x + jnp.dot(y_buf[...], wout_ref[...], preferred_element_type=F32)


def _mixer(x, g, w_in, conv_w, pool_w, pool_scale, w_out, to_cast):
    b, s, d = x.shape
    ts = ROW_TILE
    n_steps = b * (s // ts)
    row_spec = pl.BlockSpec((1, ts, d), lambda i, j: (i, j, 0))
    cast_specs = [pl.BlockSpec((w.shape[0] // n_steps, w.shape[1]),
                               lambda i, j: (i * (s // ts) + j, 0)) for w in to_cast]
    return pl.pallas_call(
        _mixer_kernel,
        grid=(b, s // ts),
        in_specs=[row_spec, _const_spec((1, d)), _const_spec((d, EVEN_IN)),
                  _const_spec((CONV_WIDTH, A_WIDTH)),
                  _const_spec((len(POOL_WINDOWS), POOL_GROUP, POOL_GROUP)),
                  _const_spec((1, B_WIDTH)), _const_spec((d, d))] + cast_specs,
        out_specs=[row_spec] + cast_specs,
        out_shape=[jax.ShapeDtypeStruct(x.shape, F32)]
                  + [jax.ShapeDtypeStruct(w.shape, BF16) for w in to_cast],
        scratch_shapes=[pltpu.VMEM((ts, d), BF16),
                        pltpu.VMEM((CONV_HALO, A_WIDTH), F32),
                        pltpu.VMEM((POOL_HALO, B_WIDTH), F32),
                        pltpu.VMEM(((A_WIDTH + B_WIDTH) // LANES, 2 * (POOL_HALO + ts), LANES),
                                   F32)],
        compiler_params=_params(2),
        name="mixer",
    )(x, g, w_in, conv_w, pool_w, pool_scale, w_out, *to_cast)


def _ffn_kernel(x_ref, g_ref, wup_ref, cw_ref, cb_ref, wdn_ref, o_ref, h_buf, carry, u_scr):
    ts = x_ref.shape[1]
    s = pl.program_id(1)

    @pl.when(s == 0)
    def _():
        carry[...] = jnp.zeros(carry.shape, F32)

    x = x_ref[0]
    xn = _rmsnorm(x, g_ref[...]).astype(BF16)
    slabs = FF_CHUNK // LANES
    for j in range(D_FF // FF_CHUNK):
        halves = []
        for half in range(2):
            c0 = half * D_FF + j * FF_CHUNK
            u = jnp.dot(xn, wup_ref[:, c0:c0 + FF_CHUNK], preferred_element_type=F32)
            parts = []
            for sl in range(slabs):
                cols = slice(c0 + sl * LANES, c0 + (sl + 1) * LANES)
                slot = ((j % 2) * 2 + half) * slabs + sl
                u_sl = u[:, sl * LANES:(sl + 1) * LANES]
                _stage_rows(u_scr, slot, SHIFT_BASE, u_sl, carry[CONV_HALO - 2:, cols])
                carry[:, cols] = u_sl[ts - CONV_HALO:ts, :]
                w = cw_ref[:, cols]
                parts.append(w[0:1] * _rows_back(u_scr, slot, SHIFT_BASE, ts, 2)
                             + w[1:2] * _rows_back(u_scr, slot, SHIFT_BASE, ts, 1)
                             + w[2:3] * u_sl + cb_ref[:, cols])
            halves.append(jnp.concatenate(parts, axis=-1))
        gate, up = halves
        h_buf[:, j * FF_CHUNK:(j + 1) * FF_CHUNK] = (gate * jax.nn.sigmoid(gate) * up).astype(BF16)
    o_ref[0] = x + jnp.dot(h_buf[...], wdn_ref[...], preferred_element_type=F32)


def _ffn(x, layer, g, w_up, conv_w, conv_b, w_down):
    b, s, d = x.shape
    ts = ROW_TILE
    row_spec = pl.BlockSpec((1, ts, d), lambda i, j: (i, j, 0))
    return pl.pallas_call(
        _ffn_kernel,
        grid=(b, s // ts),
        in_specs=[row_spec, _layer_spec((1, d), layer), _layer_spec((d, 2 * D_FF), layer),
                  _layer_spec((CONV_WIDTH, 2 * D_FF), layer),
                  _layer_spec((1, 2 * D_FF), layer), _layer_spec((D_FF, d), layer)],
        out_specs=row_spec,
        out_shape=jax.ShapeDtypeStruct(x.shape, F32),
        scratch_shapes=[pltpu.VMEM((ts, D_FF), BF16),
                        pltpu.VMEM((CONV_HALO, 2 * D_FF), F32),
                        pltpu.VMEM((4 * FF_CHUNK // LANES, SHIFT_BASE + 2 * ts, LANES), F32)],
        compiler_params=_params(2),
        name="ffn",
    )(x, g, w_up, conv_w, conv_b, w_down)


def _head_pair_norm(t, g_pair, scale):
    lane = lax.broadcasted_iota(jnp.int32, t.shape, 1)
    lo = lane < HEAD_DIM
    sq = t * t
    s_lo = jnp.sum(jnp.where(lo, sq, 0.0), axis=-1, keepdims=True)
    s_hi = jnp.sum(jnp.where(lo, 0.0, sq), axis=-1, keepdims=True)
    r = lax.rsqrt(jnp.where(lo, s_lo, s_hi) * (1.0 / HEAD_DIM) + EPS)
    return t * r * (g_pair * scale)


def _qkv_kernel(x_ref, g_ref, w_ref, gq_ref, gk_ref, *refs):
    assert DILATIONS == (1, 4, 16)
    out4, out16 = refs[0:3], refs[3:6]
    x_stage, stage = refs[6:]
    d = D_MODEL
    ts = x_ref.shape[1]
    tj = ts // 4
    for sl in range(d // LANES):
        x_stage[sl] = x_ref[0, :, sl * LANES:(sl + 1) * LANES]
    x = jnp.concatenate(
        [jnp.concatenate([x_stage[sl, pl.ds(r4, tj, stride=4), :] for sl in range(d // LANES)],
                         axis=-1) for r4 in range(4)], axis=0)
    xn = _rmsnorm(x, g_ref[...]).astype(BF16)
    slabs = QKV_CHUNK // LANES
    for c in range(3 * d // QKV_CHUNK):
        res = jnp.dot(xn, w_ref[:, c * QKV_CHUNK:(c + 1) * QKV_CHUNK],
                      preferred_element_type=F32)
        for sl in range(slabs):
            slot = c * slabs + sl
            which, hp = divmod(slot, d // LANES)
            cols = slice(hp * LANES, (hp + 1) * LANES)
            slab = res[:, sl * LANES:(sl + 1) * LANES]
            if which == 0:
                slab = _head_pair_norm(slab, gq_ref[...], HEAD_DIM ** -0.5 * LOG2E)
            elif which == 1:
                slab = _head_pair_norm(slab, gk_ref[...], 1.0)
            stage[slot] = slab
            for r4 in range(4):
                out4[which][r4, :, cols] = slab[r4 * tj:(r4 + 1) * tj].astype(BF16)
                for c4 in range(4):
                    p16 = stage[slot, pl.ds(r4 * tj + c4, tj // 4, stride=4), :]
                    out16[which][4 * r4 + c4, :, cols] = p16.astype(BF16)


def _qkv(x, g, w_qkv, gq_pair, gk_pair):
    b, s, d = x.shape
    ts = QKV_ROW_TILE
    row_spec = pl.BlockSpec((1, ts, d), lambda i, j: (i, j, 0))
    out_specs, out_shapes = [], []
    for planes in (4, 16):
        out_specs += [pl.BlockSpec((None, planes, ts // planes, d), lambda i, j: (i, 0, j, 0))] * 3
        out_shapes += [jax.ShapeDtypeStruct((b, planes, s // planes, d), BF16)] * 3
    outs = pl.pallas_call(
        _qkv_kernel,
        grid=(b, s // ts),
        in_specs=[row_spec, _const_spec((1, d)), _const_spec((d, 3 * d)),
                  _const_spec((1, LANES)), _const_spec((1, LANES))],
        out_specs=out_specs,
        out_shape=out_shapes,
        scratch_shapes=[pltpu.VMEM((d // LANES, ts, LANES), F32),
                        pltpu.VMEM((3 * d // LANES, ts, LANES), F32)],
        compiler_params=_params(2),
        name="qkv",
    )(x, g, w_qkv, gq_pair, gk_pair)
    return outs[0:3], outs[3:6]


def _stat_lane(h):
    return HEAD_DIM + h // 2 if h % 2 == 0 else h // 2


def _rows_cat(parts):
    return parts[0] if len(parts) == 1 else jnp.concatenate(parts, axis=0)


def _attn_kernel(q_ref, k_ref, v_ref, kh_ref, vh_ref, bias_ref, o_ref, m_ref, l_ref, *scratch,
                 split, interleave):
    nb = ATT_BLOCK
    n_planes, rows, _ = q_ref.shape
    piece = nb // split
    first_step = pl.program_id(2) == 0
    first_table = jnp.where(first_step, 1, 0)
    if interleave:
        o_stage, m_tiles, l_tiles = scratch

    lo = lax.broadcasted_iota(jnp.int32, (nb, LANES), 1) < HEAD_DIM
    lo_kv = lax.broadcasted_iota(jnp.int32, (2 * nb, LANES), 1) < HEAD_DIM

    for grp in range(n_planes // split):
        planes = range(grp * split, (grp + 1) * split)
        for u in range(rows // piece):
            cur = slice(u * piece, (u + 1) * piece)
            tile = grp * (rows // piece) + u
            out_rows = pl.ds(interleave * nb * u + grp, nb, stride=interleave) if interleave else None

            def block_rows(ref, cols):
                return _rows_cat([ref[p, cur, cols] for p in planes])

            def window_rows(ref, halo_ref, cols):
                parts = []
                for p in planes:
                    if u == 0:
                        parts += [halo_ref[p, :, cols], ref[p, cur, cols]]
                    else:
                        parts.append(ref[p, (u - 1) * piece:(u + 1) * piece, cols])
                return _rows_cat(parts)

            def put_stat(ref, tiles, lane_cols, value):
                if interleave:
                    tiles[tile, :, lane_cols] = value
                else:
                    for n, p in enumerate(planes):
                        ref[p, cur, lane_cols] = value[n * piece:(n + 1) * piece]

            put_stat(m_ref, m_tiles if interleave else None, slice(None), jnp.zeros((nb, LANES), F32))
            put_stat(l_ref, l_tiles if interleave else None, slice(None), jnp.ones((nb, LANES), F32))
            for hp in range(N_HEADS // 2):
                cols = slice(hp * LANES, (hp + 1) * LANES)
                q_pair = block_rows(q_ref, cols)
                k_win = window_rows(k_ref, kh_ref, cols)
                v_win = window_rows(v_ref, vh_ref, cols)
                table = first_table if u == 0 else 0
                outs = []
                for sub in range(2):
                    h = 2 * hp + sub
                    own = lo if sub == 0 else jnp.logical_not(lo)
                    own_kv = lo_kv if sub == 0 else jnp.logical_not(lo_kv)
                    q_h = jnp.where(own, q_pair, jnp.zeros_like(q_pair))
                    sc = lax.dot_general(q_h, k_win, (((1,), (1,)), ((), ())),
                                         preferred_element_type=F32)
                    sc = sc + bias_ref[table, h]
                    m = jnp.max(sc, axis=-1, keepdims=True)
                    p = jnp.exp2(sc - m).astype(BF16)
                    v_h = jnp.where(own_kv, v_win, jnp.ones_like(v_win))
                    pv = jnp.dot(p, v_h, preferred_element_type=F32)
                    outs.append(pv)
                    sl = slice(_stat_lane(h), _stat_lane(h) + 1)
                    put_stat(m_ref, m_tiles if interleave else None, sl, m)
                    put_stat(l_ref, l_tiles if interleave else None, sl, pv[:, sl])
                o_pair = jnp.where(lo, outs[0], outs[1])
                if interleave:
                    o_stage[hp, out_rows, :] = o_pair
                else:
                    for n, p in enumerate(planes):
                        o_ref[p, cur, cols] = o_pair[n * piece:(n + 1) * piece].astype(BF16)
            if interleave:
                m_ref[out_rows, :] = m_tiles[tile]
                l_ref[out_rows, :] = l_tiles[tile]
    if interleave:
        for hp in range(N_HEADS // 2):
            o_ref[:, hp * LANES:(hp + 1) * LANES] = o_stage[hp].astype(BF16)


def _attn_branch(q, k, v, bias, *, branch, split=1, interleave=0):
    b, n_planes, length, d = q.shape
    piece = ATT_BLOCK // split
    units = min(ATT_UNITS_PER_STEP, length // piece)
    step_planes = interleave or split * min(n_planes // split, ATT_UNITS_PER_STEP // units)
    rows = units * piece
    main = pl.BlockSpec((None, step_planes, rows, d), lambda bi, r, i: (bi, r, i, 0))
    halo = pl.BlockSpec((None, step_planes, piece, d),
                        lambda bi, r, i: (bi, r, jnp.maximum(i * units - 1, 0), 0))
    scratch = []
    if interleave:
        assert interleave == step_planes and rows == length and split == 1
        out_planes, out_len = n_planes // interleave, interleave * length
        o_spec = pl.BlockSpec((None, None, out_len, d), lambda bi, r, i: (bi, r, 0, 0))
        stat = pl.BlockSpec((None, None, out_len, LANES), lambda bi, r, i: (bi, r, 0, 0))
        n_tiles = step_planes * units
        scratch = [pltpu.VMEM((d // LANES, out_len, LANES), F32),
                   pltpu.VMEM((n_tiles, ATT_BLOCK, LANES), F32),
                   pltpu.VMEM((n_tiles, ATT_BLOCK, LANES), F32)]
    else:
        out_planes, out_len = n_planes, length
        o_spec = main
        stat = pl.BlockSpec((None, step_planes, rows, LANES), lambda bi, r, i: (bi, r, i, 0))
    stat_shape = jax.ShapeDtypeStruct((b, out_planes, out_len, LANES), F32)
    return pl.pallas_call(
        functools.partial(_attn_kernel, split=split, interleave=interleave),
        grid=(b, n_planes // step_planes, length // rows),
        in_specs=[main, main, main, halo, halo, _layer_spec(bias.shape[1:], branch)],
        out_specs=[o_spec, stat, stat],
        out_shape=[jax.ShapeDtypeStruct((b, out_planes, out_len, d), BF16), stat_shape, stat_shape],
        scratch_shapes=scratch,
        compiler_params=_params(3),
        name=f"attn_d{DILATIONS[branch]}",
    )(q, k, v, k, v, bias)


def _merge_kernel(x_ref, *refs):
    n_br = len(DILATIONS)
    branch_refs = [refs[3 * i:3 * i + 3] for i in range(n_br)]
    e_ref, wo_ref, out_ref, y_scr = refs[3 * n_br:]
    planes, tj, _ = branch_refs[0][0].shape

    def plane_major(ref):
        return jnp.concatenate([ref[p] for p in range(planes)], axis=0)

    ms = [plane_major(br[1]) for br in branch_refs]
    ls = [plane_major(br[2]) for br in branch_refs]
    m_max = functools.reduce(jnp.maximum, ms)
    es = [jnp.exp2(m - m_max) for m in ms]
    total = functools.reduce(jnp.add, [e * l for e, l in zip(es, ls)])
    merged = None
    for e, br in zip(es, branch_refs):
        c = e / total
        c_hi = c.astype(BF16)
        c_lo = (c - c_hi.astype(F32)).astype(BF16)
        c_wide = jnp.dot(jnp.concatenate([c_hi, c_lo], axis=-1), e_ref[...],
                         preferred_element_type=F32)
        term = c_wide * plane_major(br[0]).astype(F32)
        merged = term if merged is None else merged + term
    y = jnp.dot(merged.astype(BF16), wo_ref[...], preferred_element_type=F32)
    for slab in range(D_MODEL // LANES):
        cols = slice(slab * LANES, (slab + 1) * LANES)
        for p in range(planes):
            y_scr[slab, pl.ds(p, tj, stride=planes), :] = y[p * tj:(p + 1) * tj, cols]
        out_ref[0, :, cols] = x_ref[0, :, cols] + y_scr[slab]


def _merge_out_proj(x, branches, expand, w_o):
    b, s, d = x.shape
    ts = ROW_TILE
    planes = branches[0][0].shape[1]
    row_spec = pl.BlockSpec((1, ts, d), lambda i, j: (i, j, 0))
    o_spec = pl.BlockSpec((None, planes, ts // planes, d), lambda i, j: (i, 0, j, 0))
    stat_spec = pl.BlockSpec((None, planes, ts // planes, LANES), lambda i, j: (i, 0, j, 0))
    args = [t for br in branches for t in br]
    return pl.pallas_call(
        _merge_kernel,
        grid=(b, s // ts),
        in_specs=[row_spec] + [o_spec, stat_spec, stat_spec] * len(branches)
                 + [_const_spec(expand.shape), _const_spec((d, d))],
        out_specs=row_spec,
        out_shape=jax.ShapeDtypeStruct(x.shape, F32),
        scratch_shapes=[pltpu.VMEM((d // LANES, ts, LANES), F32)],
        compiler_params=_params(2),
        name="merge_out_proj",
    )(x, *args, expand, w_o)


def _t5_causal_bucket(dist):
    max_exact = N_REL_BUCKETS // 2
    dd = jnp.maximum(dist, 1).astype(F32)
    large = max_exact + (jnp.log(dd / max_exact) / math.log(REL_MAX_DISTANCE / max_exact)
                         * (N_REL_BUCKETS - max_exact)).astype(jnp.int32)
    large = jnp.minimum(large, N_REL_BUCKETS - 1)
    return jnp.where(dist < max_exact, dist, large)


def _bias_tables(rel_table, splits):
    n = ATT_BLOCK
    n_br = len(DILATIONS)
    n_heads = rel_table.shape[1]
    assert splits[0] in (1, 4) and all(s == 1 for s in splits[1:])
    dist = (n - jnp.arange(n + 1))[None, :] * jnp.asarray(DILATIONS)[:, None]
    by_offset = LOG2E * jnp.swapaxes(rel_table.astype(F32)[_t5_causal_bucket(dist)], 1, 2)

    def mask_fill(count):
        return jnp.full((n_heads, count), MASK_VALUE, F32)

    split = splits[0]
    rows = []
    for br in range(n_br):
        w = by_offset[br]
        if br > 0 or split == 1:
            rows.append(jnp.stack([jnp.concatenate([w, mask_fill(4 * n - (n + 1))], axis=1)] * 4, 1))
            continue
        groups = []
        for r in range(split):
            parts = []
            for rp in range(split):
                first = 0 if rp >= r else 1
                vals = w[:, split * first + rp - r::split]
                parts += [mask_fill(first), vals, mask_fill(n - first - vals.shape[1])]
            groups.append(jnp.concatenate(parts, axis=1))
        rows.append(jnp.stack(groups, axis=1))
    rows = jnp.stack(rows)
    return pl.pallas_call(
        functools.partial(_bias_kernel, split=split),
        out_shape=jax.ShapeDtypeStruct((n_br, 2, n_heads, n, 2 * n), F32),
        name="bias_tables",
    )(rows[:, :, :, None, :])


def _bias_kernel(rows_ref, out_ref, *, split):
    n = ATT_BLOCK
    n_br, _, n_heads = out_ref.shape[:3]
    col = lax.broadcasted_iota(jnp.int32, (n, 2 * n), 1)
    for br in range(n_br):
        sp = split if br == 0 else 1
        prev_cols = col % (2 * n // sp) < n // sp
        for h in range(n_heads):
            if sp == 1:
                wide = jnp.broadcast_to(rows_ref[br, h, 0], (n, 4 * n))
                tab = pltpu.roll(wide, 0, 1, stride=1, stride_axis=0)[:, :2 * n]
            else:
                groups = []
                for r in range(sp):
                    wide = jnp.broadcast_to(rows_ref[br, h, r], (n // sp, 4 * n))
                    rolled = pltpu.roll(wide, 0, 1, stride=1, stride_axis=0)
                    groups.append(jnp.concatenate(
                        [rolled[:, rp * n:rp * n + 2 * n // sp] for rp in range(sp)], axis=1))
                tab = jnp.concatenate(groups, axis=0)
            out_ref[br, 0, h] = tab
            out_ref[br, 1, h] = jnp.where(prev_cols, MASK_VALUE, tab)


def _head_expand_matrix():
    row = jnp.arange(2 * LANES)[:, None] % LANES
    head = jnp.arange(D_MODEL)[None, :] // HEAD_DIM
    stat_lane = jnp.where(head % 2 == 0, HEAD_DIM + head // 2, head // 2)
    return (row == stat_lane).astype(BF16)


def kernel(x, rel_bias, even_norm, even_w_in, even_conv_w, even_pool_w, even_pool_scale,
           even_w_out, odd_norm, odd_w_qkv, odd_q_norm, odd_k_norm, odd_w_o, ffn_norm,
           ffn_w_up, ffn_conv_w, ffn_conv_b, ffn_w_down):
    depth, d, d_ff2 = ffn_w_up.shape
    x, w_up, w_down, w_qkv, w_o = _mixer(
        x, even_norm[0][None], even_w_in[0].astype(BF16), even_conv_w[0],
        even_pool_w[0].astype(BF16), even_pool_scale[0][None], even_w_out[0].astype(BF16),
        to_cast=[ffn_w_up.reshape(depth * d, d_ff2), ffn_w_down.reshape(depth * D_FF, d),
                 odd_w_qkv[0], odd_w_o[0]])
    ffn_params = (ffn_norm[:, None], w_up.reshape(depth, d, d_ff2), ffn_conv_w,
                  ffn_conv_b[:, None], w_down.reshape(depth, D_FF, d))

    def ffn(t, layer):
        return _ffn(t, layer, *ffn_params)

    x = ffn(x, 0)

    gq_pair = jnp.tile(odd_q_norm[0], 2)[None]
    gk_pair = jnp.tile(odd_k_norm[0], 2)[None]
    qkv4, qkv16 = _qkv(x, odd_norm[0][None], w_qkv, gq_pair, gk_pair)
    bias = _bias_tables(rel_bias, splits=(4, 1, 1))
    branches = [
        _attn_branch(*qkv4, bias, branch=0, split=4),
        _attn_branch(*qkv4, bias, branch=1),
        _attn_branch(*qkv16, bias, branch=2, interleave=4),
    ]
    x = _merge_out_proj(x, branches, _head_expand_matrix(), w_o)
    x = ffn(x, 1)
    return x
```

```python
import functools
import math

import jax
import jax.numpy as jnp
import numpy as np
from jax import lax
from jax.experimental import pallas as pl
from jax.experimental.pallas import tpu as pltpu

D_MODEL = 1024
CONV_WIDTH = 3
A_WIDTH = 512
B_WIDTH = 512
POOL_WINDOWS = (2, 4, 8, 16)
POOL_GROUP = 128
EVEN_IN = 3 * A_WIDTH + B_WIDTH
HEAD_DIM = 64
N_HEADS = 16
DILATED_PAIRS = ((128, 1), (512, 4), (2048, 16))
DILATIONS = tuple(dil for _, dil in DILATED_PAIRS)
N_REL_BUCKETS = 32
REL_MAX_DISTANCE = 2048
D_FF = 2816
EPS = 1e-6
MASK_VALUE = -1e30

BF16 = jnp.bfloat16
F32 = jnp.float32

LANES = 128
ATT_BLOCK = 128
ATT_UNITS_PER_STEP = 16
LOG2E = math.log2(math.e)
QKV_ROW_TILE = 1024
QKV_STAGE_SLOTS = 8
ROW_TILE = 1024
POOL_HALO = 16
CONV_HALO = 8
FF_CHUNK = 256
QKV_CHUNK = 256
MIX_CHUNK = 256
SHIFT_BASE = 8
VMEM_LIMIT = 56 * 1024 * 1024


def _const_spec(shape):
    nd = len(shape)
    return pl.BlockSpec(shape, lambda *_: (0,) * nd, pipeline_mode=pl.Buffered(1))


def _layer_spec(shape, layer):
    nd = len(shape)
    return pl.BlockSpec((None,) + tuple(shape), lambda *_: (layer,) + (0,) * nd,
                        pipeline_mode=pl.Buffered(1))


def _params(n_axes):
    return pltpu.CompilerParams(
        dimension_semantics=("arbitrary",) * n_axes, vmem_limit_bytes=VMEM_LIMIT)


def _rmsnorm(x, g):
    ms = jnp.mean(x * x, axis=-1, keepdims=True)
    return x * lax.rsqrt(ms + EPS) * g


def _stage_rows(scr, slot, base, tile, prev):
    n_prev = prev.shape[0]
    scr[slot, pl.ds(base - 2 * n_prev, n_prev, stride=2), :] = prev
    scr[slot, pl.ds(base, tile.shape[0], stride=2), :] = tile


def _rows_back(scr, slot, base, n_rows, k):
    return scr[slot, pl.ds(base - 2 * k, n_rows, stride=2), :]


def _mixer_kernel(x_ref, g_ref, win_ref, cw_ref, pw_ref, ps_ref, wout_ref, *refs):
    n_cast = (len(refs) - 5) // 2
    cast_src, o_ref, cast_dst = refs[:n_cast], refs[n_cast], refs[n_cast + 1:2 * n_cast + 1]
    y_buf, carry_a, carry_p, scr = refs[2 * n_cast + 1:]
    for src, dst in zip(cast_src, cast_dst):
        dst[...] = src[...].astype(BF16)
    ts = x_ref.shape[1]
    s = pl.program_id(1)
    base = 2 * POOL_HALO

    @pl.when(s == 0)
    def _():
        carry_a[...] = jnp.zeros(carry_a.shape, F32)
        carry_p[...] = jnp.zeros(carry_p.shape, F32)

    x = x_ref[0]
    xn = _rmsnorm(x, g_ref[...]).astype(BF16)
    slabs = MIX_CHUNK // LANES

    def proj(col0):
        return jnp.dot(xn, win_ref[:, col0:col0 + MIX_CHUNK], preferred_element_type=F32)

    for c in range(A_WIDTH // MIX_CHUNK):
        h = proj(c * MIX_CHUNK)
        gate_b = proj(A_WIDTH + c * MIX_CHUNK)
        gate_c = proj(2 * A_WIDTH + c * MIX_CHUNK)
        ch = gate_c * h
        for sl in range(slabs):
            slot = c * slabs + sl
            cols = slice(slot * LANES, (slot + 1) * LANES)
            ch_sl = ch[:, sl * LANES:(sl + 1) * LANES]
            _stage_rows(scr, slot, base, ch_sl, carry_a[CONV_HALO - 2:, cols])
            carry_a[:, cols] = ch_sl[ts - CONV_HALO:, :]
            cw = cw_ref[:, cols]
            conv = (cw[0:1] * _rows_back(scr, slot, base, ts, 2)
                    + cw[1:2] * _rows_back(scr, slot, base, ts, 1) + cw[2:3] * ch_sl)
            y_buf[:, cols] = (gate_b[:, sl * LANES:(sl + 1) * LANES] * conv).astype(BF16)

    pos = s * ts + lax.broadcasted_iota(jnp.int32, (ts, 1), 0)
    for c in range(B_WIDTH // MIX_CHUNK):
        pin = proj(3 * A_WIDTH + c * MIX_CHUNK)
        for sl in range(slabs):
            g = c * slabs + sl
            k = POOL_WINDOWS[g]
            slot = A_WIDTH // LANES + g
            cols = slice(g * POOL_GROUP, (g + 1) * POOL_GROUP)
            cur = pin[:, sl * LANES:(sl + 1) * LANES]
            _stage_rows(scr, slot, base, cur, carry_p[:, cols])
            carry_p[:, cols] = cur[ts - POOL_HALO:, :]
            acc = cur
            for j in range(1, k):
                acc = acc + _rows_back(scr, slot, base, ts, j)
            cnt = jnp.minimum(pos + 1, k).astype(F32)
            pooled = acc / cnt - cur
            yb = jnp.dot(pooled.astype(BF16), pw_ref[g], preferred_element_type=F32)
            y_buf[:, A_WIDTH + g * POOL_GROUP:A_WIDTH + (g + 1) * POOL_GROUP] = (
                yb * ps_ref[:, cols]).astype(BF16)

    o_ref[0] = x + jnp.dot(y_buf[...], wout_ref[...], preferred_element_type=F32)


def _mixer(x, g, w_in, conv_w, pool_w, pool_scale, w_out, to_cast):
    b, s, d = x.shape
    ts = ROW_TILE
    n_steps = b * (s // ts)
    row_spec = pl.BlockSpec((1, ts, d), lambda i, j: (i, j, 0))
    cast_specs = [pl.BlockSpec((w.shape[0] // n_steps, w.shape[1]),
                               lambda i, j: (i * (s // ts) + j, 0)) for w in to_cast]
    return pl.pallas_call(
        _mixer_kernel,
        grid=(b, s // ts),
        in_specs=[row_spec, _const_spec((1, d)), _const_spec((d, EVEN_IN)),
                  _const_spec((CONV_WIDTH, A_WIDTH)),
                  _const_spec((len(POOL_WINDOWS), POOL_GROUP, POOL_GROUP)),
                  _const_spec((1, B_WIDTH)), _const_spec((d, d))] + cast_specs,
        out_specs=[row_spec] + cast_specs,
        out_shape=[jax.ShapeDtypeStruct(x.shape, F32)]
                  + [jax.ShapeDtypeStruct(w.shape, BF16) for w in to_cast],
        scratch_shapes=[pltpu.VMEM((ts, d), BF16),
                        pltpu.VMEM((CONV_HALO, A_WIDTH), F32),
                        pltpu.VMEM((POOL_HALO, B_WIDTH), F32),
                        pltpu.VMEM(((A_WIDTH + B_WIDTH) // LANES, 2 * (POOL_HALO + ts), LANES),
                                   F32)],
        compiler_params=_params(2),
        name="mixer",
    )(x, g, w_in, conv_w, pool_w, pool_scale, w_out, *to_cast)


def _ffn_kernel(x_ref, g_ref, wup_ref, cw_ref, cb_ref, wdn_ref, o_ref, h_buf, carry, u_scr):
    ts = x_ref.shape[1]
    s = pl.program_id(1)

    @pl.when(s == 0)
    def _():
        carry[...] = jnp.zeros(carry.shape, F32)

    x = x_ref[0]
    xn = _rmsnorm(x, g_ref[...]).astype(BF16)
    slabs = FF_CHUNK // LANES
    for j in range(D_FF // FF_CHUNK):
        halves = []
        for half in range(2):
            c0 = half * D_FF + j * FF_CHUNK
            u = jnp.dot(xn, wup_ref[:, c0:c0 + FF_CHUNK], preferred_element_type=F32)
            parts = []
            for sl in range(slabs):
                cols = slice(c0 + sl * LANES, c0 + (sl + 1) * LANES)
                slot = ((j % 2) * 2 + half) * slabs + sl
                u_sl = u[:, sl * LANES:(sl + 1) * LANES]
                _stage_rows(u_scr, slot, SHIFT_BASE, u_sl, carry[CONV_HALO - 2:, cols])
                carry[:, cols] = u_sl[ts - CONV_HALO:ts, :]
                w = cw_ref[:, cols]
                parts.append(w[0:1] * _rows_back(u_scr, slot, SHIFT_BASE, ts, 2)
                             + w[1:2] * _rows_back(u_scr, slot, SHIFT_BASE, ts, 1)
                             + w[2:3] * u_sl + cb_ref[:, cols])
            halves.append(jnp.concatenate(parts, axis=-1))
        gate, up = halves
        h_buf[:, j * FF_CHUNK:(j + 1) * FF_CHUNK] = (gate * jax.nn.sigmoid(gate) * up).astype(BF16)
    o_ref[0] = x + jnp.dot(h_buf[...], wdn_ref[...], preferred_element_type=F32)


def _ffn(x, layer, g, w_up, conv_w, conv_b, w_down):
    b, s, d = x.shape
    ts = ROW_TILE
    row_spec = pl.BlockSpec((1, ts, d), lambda i, j: (i, j, 0))
    return pl.pallas_call(
        _ffn_kernel,
        grid=(b, s // ts),
        in_specs=[row_spec, _layer_spec((1, d), layer), _layer_spec((d, 2 * D_FF), layer),
                  _layer_spec((CONV_WIDTH, 2 * D_FF), layer),
                  _layer_spec((1, 2 * D_FF), layer), _layer_spec((D_FF, d), layer)],
        out_specs=row_spec,
        out_shape=jax.ShapeDtypeStruct(x.shape, F32),
        scratch_shapes=[pltpu.VMEM((ts, D_FF), BF16),
                        pltpu.VMEM((CONV_HALO, 2 * D_FF), F32),
                        pltpu.VMEM((4 * FF_CHUNK // LANES, SHIFT_BASE + 2 * ts, LANES), F32)],
        compiler_params=_params(2),
        name="ffn",
    )(x, g, w_up, conv_w, conv_b, w_down)


def _head_pair_norm(t, g_pair, scale):
    lane = lax.broadcasted_iota(jnp.int32, t.shape, 1)
    lo = lane < HEAD_DIM
    sq = t * t
    s_lo = jnp.sum(jnp.where(lo, sq, 0.0), axis=-1, keepdims=True)
    s_hi = jnp.sum(jnp.where(lo, 0.0, sq), axis=-1, keepdims=True)
    r = lax.rsqrt(jnp.where(lo, s_lo, s_hi) * (1.0 / HEAD_DIM) + EPS)
    return t * r * (g_pair * scale)


def _qkv_kernel(x_ref, g_ref, w_ref, gq_ref, gk_ref, *refs):
    assert DILATIONS == (1, 4, 16)
    out4, out16 = refs[0:3], refs[3:6]
    x_stage, stage = refs[6:]
    d = D_MODEL
    ts = x_ref.shape[1]
    tj = ts // 4
    for sl in range(d // LANES):
        x_stage[sl] = x_ref[0, :, sl * LANES:(sl + 1) * LANES]
    x = jnp.concatenate(
        [jnp.concatenate([x_stage[sl, pl.ds(r4, tj, stride=4), :] for sl in range(d // LANES)],
                         axis=-1) for r4 in range(4)], axis=0)
    xn = _rmsnorm(x, g_ref[...]).astype(BF16)
    slabs = QKV_CHUNK // LANES
    for c in range(3 * d // QKV_CHUNK):
        res = jnp.dot(xn, w_ref[:, c * QKV_CHUNK:(c + 1) * QKV_CHUNK],
                      preferred_element_type=F32)
        for sl in range(slabs):
            slot = c * slabs + sl
            which, hp = divmod(slot, d // LANES)
            cols = slice(hp * LANES, (hp + 1) * LANES)
            slab = res[:, sl * LANES:(sl + 1) * LANES]
            if which == 0:
                slab = _head_pair_norm(slab, gq_ref[...], HEAD_DIM ** -0.5 * LOG2E)
            elif which == 1:
                slab = _head_pair_norm(slab, gk_ref[...], 1.0)
            stage[slot % QKV_STAGE_SLOTS] = slab
            for r4 in range(4):
                out4[which][r4, :, cols] = slab[r4 * tj:(r4 + 1) * tj].astype(BF16)
                for c4 in range(4):
                    p16 = stage[slot % QKV_STAGE_SLOTS, pl.ds(r4 * tj + c4, tj // 4, stride=4), :]
                    out16[which][4 * r4 + c4, :, cols] = p16.astype(BF16)


def _qkv(x, g, w_qkv, gq_pair, gk_pair):
    b, s, d = x.shape
    ts = QKV_ROW_TILE
    row_spec = pl.BlockSpec((1, ts, d), lambda i, j: (i, j, 0))
    out_specs, out_shapes = [], []
    for planes in (4, 16):
        out_specs += [pl.BlockSpec((None, planes, ts // planes, d), lambda i, j: (i, 0, j, 0))] * 3
        out_shapes += [jax.ShapeDtypeStruct((b, planes, s // planes, d), BF16)] * 3
    outs = pl.pallas_call(
        _qkv_kernel,
        grid=(b, s // ts),
        in_specs=[row_spec, _const_spec((1, d)), _const_spec((d, 3 * d)),
                  _const_spec((1, LANES)), _const_spec((1, LANES))],
        out_specs=out_specs,
        out_shape=out_shapes,
        scratch_shapes=[pltpu.VMEM((d // LANES, ts, LANES), F32),
                        pltpu.VMEM((QKV_STAGE_SLOTS, ts, LANES), F32)],
        compiler_params=_params(2),
        name="qkv",
    )(x, g, w_qkv, gq_pair, gk_pair)
    return outs[0:3], outs[3:6]


def _stat_lane(h):
    return HEAD_DIM + h // 2 if h % 2 == 0 else h // 2


def _rows_cat(parts):
    return parts[0] if len(parts) == 1 else jnp.concatenate(parts, axis=0)


def _attn_kernel(q_ref, k_ref, v_ref, kh_ref, vh_ref, bias_ref, o_ref, m_ref, l_ref, *scratch,
                 split, interleave):
    nb = ATT_BLOCK
    n_planes, rows, _ = q_ref.shape
    piece = nb // split
    first_step = pl.program_id(2) == 0
    first_table = jnp.where(first_step, 1, 0)
    if interleave:
        o_stage, m_tiles, l_tiles = scratch

    lo = lax.broadcasted_iota(jnp.int32, (nb, LANES), 1) < HEAD_DIM
    lo_kv = lax.broadcasted_iota(jnp.int32, (2 * nb, LANES), 1) < HEAD_DIM

    for grp in range(n_planes // split):
        planes = range(grp * split, (grp + 1) * split)
        for u in range(rows // piece):
            cur = slice(u * piece, (u + 1) * piece)
            tile = grp * (rows // piece) + u
            out_rows = pl.ds(interleave * nb * u + grp, nb, stride=interleave) if interleave else None

            def block_rows(ref, cols):
                return _rows_cat([ref[p, cur, cols] for p in planes])

            def window_rows(ref, halo_ref, cols):
                parts = []
                for p in planes:
                    if u == 0:
                        parts += [halo_ref[p, :, cols], ref[p, cur, cols]]
                    else:
                        parts.append(ref[p, (u - 1) * piece:(u + 1) * piece, cols])
                return _rows_cat(parts)

            def put_stat(ref, tiles, lane_cols, value):
                if interleave:
                    tiles[tile, :, lane_cols] = value
                else:
                    for n, p in enumerate(planes):
                        ref[p, cur, lane_cols] = value[n * piece:(n + 1) * piece]

            put_stat(m_ref, m_tiles if interleave else None, slice(None), jnp.zeros((nb, LANES), F32))
            put_stat(l_ref, l_tiles if interleave else None, slice(None), jnp.ones((nb, LANES), F32))
            for hp in range(N_HEADS // 2):
                cols = slice(hp * LANES, (hp + 1) * LANES)
                q_pair = block_rows(q_ref, cols)
                k_win = window_rows(k_ref, kh_ref, cols)
                v_win = window_rows(v_ref, vh_ref, cols)
                table = first_table if u == 0 else 0
                outs = []
                for sub in range(2):
                    h = 2 * hp + sub
                    own = lo if sub == 0 else jnp.logical_not(lo)
                    own_kv = lo_kv if sub == 0 else jnp.logical_not(lo_kv)
                    q_h = jnp.where(own, q_pair, jnp.zeros_like(q_pair))
                    sc = lax.dot_general(q_h, k_win, (((1,), (1,)), ((), ())),
                                         preferred_element_type=F32)
                    sc = sc + bias_ref[table, h]
                    m = jnp.max(sc, axis=-1, keepdims=True)
                    p = jnp.exp2(sc - m).astype(BF16)
                    v_h = jnp.where(own_kv, v_win, jnp.ones_like(v_win))
                    pv = jnp.dot(p, v_h, preferred_element_type=F32)
                    outs.append(pv)
                    sl = slice(_stat_lane(h), _stat_lane(h) + 1)
                    put_stat(m_ref, m_tiles if interleave else None, sl, m)
                    put_stat(l_ref, l_tiles if interleave else None, sl, pv[:, sl])
                o_pair = jnp.where(lo, outs[0], outs[1])
                if interleave:
                    o_stage[hp, out_rows, :] = o_pair
                else:
                    for n, p in enumerate(planes):
                        o_ref[p, cur, cols] = o_pair[n * piece:(n + 1) * piece].astype(BF16)
            if interleave:
                m_ref[out_rows, :] = m_tiles[tile]
                l_ref[out_rows, :] = l_tiles[tile]
    if interleave:
        for hp in range(N_HEADS // 2):
            o_ref[:, hp * LANES:(hp + 1) * LANES] = o_stage[hp].astype(BF16)


def _attn_branch(q, k, v, bias, *, branch, split=1, interleave=0):
    b, n_planes, length, d = q.shape
    piece = ATT_BLOCK // split
    units = min(ATT_UNITS_PER_STEP, length // piece)
    step_planes = interleave or split * min(n_planes // split, ATT_UNITS_PER_STEP // units)
    rows = units * piece
    main = pl.BlockSpec((None, step_planes, rows, d), lambda bi, r, i: (bi, r, i, 0))
    halo = pl.BlockSpec((None, step_planes, piece, d),
                        lambda bi, r, i: (bi, r, jnp.maximum(i * units - 1, 0), 0))
    scratch = []
    if interleave:
        assert interleave == step_planes and rows == length and split == 1
        out_planes, out_len = n_planes // interleave, interleave * length
        o_spec = pl.BlockSpec((None, None, out_len, d), lambda bi, r, i: (bi, r, 0, 0))
        stat = pl.BlockSpec((None, None, out_len, LANES), lambda bi, r, i: (bi, r, 0, 0))
        n_tiles = step_planes * units
        scratch = [pltpu.VMEM((d // LANES, out_len, LANES), F32),
                   pltpu.VMEM((n_tiles, ATT_BLOCK, LANES), F32),
                   pltpu.VMEM((n_tiles, ATT_BLOCK, LANES), F32)]
    else:
        out_planes, out_len = n_planes, length
        o_spec = main
        stat = pl.BlockSpec((None, step_planes, rows, LANES), lambda bi, r, i: (bi, r, i, 0))
    stat_shape = jax.ShapeDtypeStruct((b, out_planes, out_len, LANES), F32)
    return pl.pallas_call(
        functools.partial(_attn_kernel, split=split, interleave=interleave),
        grid=(b, n_planes // step_planes, length // rows),
        in_specs=[main, main, main, halo, halo, _layer_spec(bias.shape[1:], branch)],
        out_specs=[o_spec, stat, stat],
        out_shape=[jax.ShapeDtypeStruct((b, out_planes, out_len, d), BF16), stat_shape, stat_shape],
        scratch_shapes=scratch,
        compiler_params=_params(3),
        name=f"attn_d{DILATIONS[branch]}",
    )(q, k, v, k, v, bias)


def _merge_kernel(x_ref, *refs):
    n_br = len(DILATIONS)
    branch_refs = [refs[3 * i:3 * i + 3] for i in range(n_br)]
    e_ref, wo_ref, out_ref, y_scr = refs[3 * n_br:]
    planes, tj, _ = branch_refs[0][0].shape

    def plane_major(ref):
        return jnp.concatenate([ref[p] for p in range(planes)], axis=0)

    ms = [plane_major(br[1]) for br in branch_refs]
    ls = [plane_major(br[2]) for br in branch_refs]
    m_max = functools.reduce(jnp.maximum, ms)
    es = [jnp.exp2(m - m_max) for m in ms]
    total = functools.reduce(jnp.add, [e * l for e, l in zip(es, ls)])
    merged = None
    for e, br in zip(es, branch_refs):
        c = e / total
        c_hi = c.astype(BF16)
        c_lo = (c - c_hi.astype(F32)).astype(BF16)
        c_wide = jnp.dot(jnp.concatenate([c_hi, c_lo], axis=-1), e_ref[...],
                         preferred_element_type=F32)
        term = c_wide * plane_major(br[0]).astype(F32)
        merged = term if merged is None else merged + term
    y = jnp.dot(merged.astype(BF16), wo_ref[...], preferred_element_type=F32)
    for slab in range(D_MODEL // LANES):
        cols = slice(slab * LANES, (slab + 1) * LANES)
        for p in range(planes):
            y_scr[slab, pl.ds(p, tj, stride=planes), :] = y[p * tj:(p + 1) * tj, cols]
        out_ref[0, :, cols] = x_ref[0, :, cols] + y_scr[slab]


def _merge_out_proj(x, branches, expand, w_o):
    b, s, d = x.shape
    ts = ROW_TILE
    planes = branches[0][0].shape[1]
    row_spec = pl.BlockSpec((1, ts, d), lambda i, j: (i, j, 0))
    o_spec = pl.BlockSpec((None, planes, ts // planes, d), lambda i, j: (i, 0, j, 0))
    stat_spec = pl.BlockSpec((None, planes, ts // planes, LANES), lambda i, j: (i, 0, j, 0))
    args = [t for br in branches for t in br]
    return pl.pallas_call(
        _merge_kernel,
        grid=(b, s // ts),
        in_specs=[row_spec] + [o_spec, stat_spec, stat_spec] * len(branches)
                 + [_const_spec(expand.shape), _const_spec((d, d))],
        out_specs=row_spec,
        out_shape=jax.ShapeDtypeStruct(x.shape, F32),
        scratch_shapes=[pltpu.VMEM((d // LANES, ts, LANES), F32)],
        compiler_params=_params(2),
        name="merge_out_proj",
    )(x, *args, expand, w_o)


def _t5_causal_bucket(dist):
    max_exact = N_REL_BUCKETS // 2
    dd = jnp.maximum(dist, 1).astype(F32)
    large = max_exact + (jnp.log(dd / max_exact) / math.log(REL_MAX_DISTANCE / max_exact)
                         * (N_REL_BUCKETS - max_exact)).astype(jnp.int32)
    large = jnp.minimum(large, N_REL_BUCKETS - 1)
    return jnp.where(dist < max_exact, dist, large)


def _bias_tables(rel_table, splits):
    n = ATT_BLOCK
    n_br = len(DILATIONS)
    n_heads = rel_table.shape[1]
    assert splits[0] in (1, 4) and all(s == 1 for s in splits[1:])
    dist = (n - jnp.arange(n + 1))[None, :] * jnp.asarray(DILATIONS)[:, None]
    by_offset = LOG2E * jnp.swapaxes(rel_table.astype(F32)[_t5_causal_bucket(dist)], 1, 2)

    def mask_fill(count):
        return jnp.full((n_heads, count), MASK_VALUE, F32)

    split = splits[0]
    rows = []
    for br in range(n_br):
        w = by_offset[br]
        if br > 0 or split == 1:
            rows.append(jnp.stack([jnp.concatenate([w, mask_fill(4 * n - (n + 1))], axis=1)] * 4, 1))
            continue
        groups = []
        for r in range(split):
            parts = []
            for rp in range(split):
                first = 0 if rp >= r else 1
                vals = w[:, split * first + rp - r::split]
                parts += [mask_fill(first), vals, mask_fill(n - first - vals.shape[1])]
            groups.append(jnp.concatenate(parts, axis=1))
        rows.append(jnp.stack(groups, axis=1))
    rows = jnp.stack(rows)
    return pl.pallas_call(
        functools.partial(_bias_kernel, split=split),
        out_shape=jax.ShapeDtypeStruct((n_br, 2, n_heads, n, 2 * n), F32),
        name="bias_tables",
    )(rows[:, :, :, None, :])


def _bias_kernel(rows_ref, out_ref, *, split):
    n = ATT_BLOCK
    n_br, _, n_heads = out_ref.shape[:3]
    col = lax.broadcasted_iota(jnp.int32, (n, 2 * n), 1)
    for br in range(n_br):
        sp = split if br == 0 else 1
        prev_cols = col % (2 * n // sp) < n // sp
        for h in range(n_heads):
            if sp == 1:
                wide = jnp.broadcast_to(rows_ref[br, h, 0], (n, 4 * n))
                tab = pltpu.roll(wide, 0, 1, stride=1, stride_axis=0)[:, :2 * n]
            else:
                groups = []
                for r in range(sp):
                    wide = jnp.broadcast_to(rows_ref[br, h, r], (n // sp, 4 * n))
                    rolled = pltpu.roll(wide, 0, 1, stride=1, stride_axis=0)
                    groups.append(jnp.concatenate(
                        [rolled[:, rp * n:rp * n + 2 * n // sp] for rp in range(sp)], axis=1))
                tab = jnp.concatenate(groups, axis=0)
            out_ref[br, 0, h] = tab
            out_ref[br, 1, h] = jnp.where(prev_cols, MASK_VALUE, tab)


def _head_expand_matrix():
    row = jnp.arange(2 * LANES)[:, None] % LANES
    head = jnp.arange(D_MODEL)[None, :] // HEAD_DIM
    stat_lane = jnp.where(head % 2 == 0, HEAD_DIM + head // 2, head // 2)
    return (row == stat_lane).astype(BF16)


def kernel(x, rel_bias, even_norm, even_w_in, even_conv_w, even_pool_w, even_pool_scale,
           even_w_out, odd_norm, odd_w_qkv, odd_q_norm, odd_k_norm, odd_w_o, ffn_norm,
           ffn_w_up, ffn_conv_w, ffn_conv_b, ffn_w_down):
    depth, d, d_ff2 = ffn_w_up.shape
    x, w_up, w_down, w_qkv, w_o = _mixer(
        x, even_norm[0][None], even_w_in[0].astype(BF16), even_conv_w[0],
        even_pool_w[0].astype(BF16), even_pool_scale[0][None], even_w_out[0].astype(BF16),
        to_cast=[ffn_w_up.reshape(depth * d, d_ff2), ffn_w_down.reshape(depth * D_FF, d),
                 odd_w_qkv[0], odd_w_o[0]])
    ffn_params = (ffn_norm[:, None], w_up.reshape(depth, d, d_ff2), ffn_conv_w,
                  ffn_conv_b[:, None], w_down.reshape(depth, D_FF, d))

    def ffn(t, layer):
        return _ffn(t, layer, *ffn_params)

    x = ffn(x, 0)

    gq_pair = jnp.tile(odd_q_norm[0], 2)[None]
    gk_pair = jnp.tile(odd_k_norm[0], 2)[None]
    qkv4, qkv16 = _qkv(x, odd_norm[0][None], w_qkv, gq_pair, gk_pair)
    bias = _bias_tables(rel_bias, splits=(4, 1, 1))
    branches = [
        _attn_branch(*qkv4, bias, branch=0, split=4),
        _attn_branch(*qkv4, bias, branch=1),
        _attn_branch(*qkv16, bias, branch=2, interleave=4),
    ]
    x = _merge_out_proj(x, branches, _head_expand_matrix(), w_o)
    x = ffn(x, 1)
    return x
```

```python
import functools
import math

import jax
import jax.numpy as jnp
import numpy as np
from jax import lax
from jax.experimental import pallas as pl
from jax.experimental.pallas import tpu as pltpu

D_MODEL = 1024
CONV_WIDTH = 3
A_WIDTH = 512
B_WIDTH = 512
POOL_WINDOWS = (2, 4, 8, 16)
POOL_GROUP = 128
EVEN_IN = 3 * A_WIDTH + B_WIDTH
HEAD_DIM = 64
N_HEADS = 16
DILATED_PAIRS = ((128, 1), (512, 4), (2048, 16))
DILATIONS = tuple(dil for _, dil in DILATED_PAIRS)
N_REL_BUCKETS = 32
REL_MAX_DISTANCE = 2048
D_FF = 2816
EPS = 1e-6
MASK_VALUE = -1e30

BF16 = jnp.bfloat16
F32 = jnp.float32

LANES = 128
ATT_BLOCK = 128
ATT_UNITS_PER_STEP = 16
LOG2E = math.log2(math.e)
QKV_ROW_TILE = 1024
QKV_STAGE_SLOTS = 8
ROW_TILE = 1024
POOL_HALO = 16
CONV_HALO = 8
FF_CHUNK = 256
QKV_CHUNK = 256
MIX_CHUNK = 256
SHIFT_BASE = 8
VMEM_LIMIT = 56 * 1024 * 1024


def _const_spec(shape):
    nd = len(shape)
    return pl.BlockSpec(shape, lambda *_: (0,) * nd, pipeline_mode=pl.Buffered(1))


def _layer_spec(shape, layer):
    nd = len(shape)
    return pl.BlockSpec((None,) + tuple(shape), lambda *_: (layer,) + (0,) * nd,
                        pipeline_mode=pl.Buffered(1))


def _params(n_axes):
    return pltpu.CompilerParams(
        dimension_semantics=("arbitrary",) * n_axes, vmem_limit_bytes=VMEM_LIMIT)


def _rmsnorm(x, g):
    ms = jnp.mean(x * x, axis=-1, keepdims=True)
    return x * lax.rsqrt(ms + EPS) * g


def _stage_rows(scr, slot, base, tile, prev):
    n_prev = prev.shape[0]
    scr[slot, pl.ds(base - 2 * n_prev, n_prev, stride=2), :] = prev
    scr[slot, pl.ds(base, tile.shape[0], stride=2), :] = tile


def _rows_back(scr, slot, base, n_rows, k):
    return scr[slot, pl.ds(base - 2 * k, n_rows, stride=2), :]


def _mixer_kernel(x_ref, g_ref, win_ref, cw_ref, pw_ref, ps_ref, wout_ref, *refs):
    n_cast = (len(refs) - 5) // 2
    cast_src, o_ref, cast_dst = refs[:n_cast], refs[n_cast], refs[n_cast + 1:2 * n_cast + 1]
    y_buf, carry_a, carry_p, scr = refs[2 * n_cast + 1:]
    for src, dst in zip(cast_src, cast_dst):
        dst[...] = src[...].astype(BF16)
    ts = x_ref.shape[1]
    s = pl.program_id(1)
    base = 2 * POOL_HALO

    @pl.when(s == 0)
    def _():
        carry_a[...] = jnp.zeros(carry_a.shape, F32)
        carry_p[...] = jnp.zeros(carry_p.shape, F32)

    x = x_ref[0]
    xn = _rmsnorm(x, g_ref[...]).astype(BF16)
    slabs = MIX_CHUNK // LANES

    def proj(col0):
        return jnp.dot(xn, win_ref[:, col0:col0 + MIX_CHUNK], preferred_element_type=F32)

    for c in range(A_WIDTH // MIX_CHUNK):
        h = proj(c * MIX_CHUNK)
        gate_b = proj(A_WIDTH + c * MIX_CHUNK)
        gate_c = proj(2 * A_WIDTH + c * MIX_CHUNK)
        ch = gate_c * h
        for sl in range(slabs):
            slot = c * slabs + sl
            cols = slice(slot * LANES, (slot + 1) * LANES)
            ch_sl = ch[:, sl * LANES:(sl + 1) * LANES]
            _stage_rows(scr, slot, base, ch_sl, carry_a[CONV_HALO - 2:, cols])
            carry_a[:, cols] = ch_sl[ts - CONV_HALO:, :]
            cw = cw_ref[:, cols]
            conv = (cw[0:1] * _rows_back(scr, slot, base, ts, 2)
                    + cw[1:2] * _rows_back(scr, slot, base, ts, 1) + cw[2:3] * ch_sl)
            y_buf[:, cols] = (gate_b[:, sl * LANES:(sl + 1) * LANES] * conv).astype(BF16)

    pos = s * ts + lax.broadcasted_iota(jnp.int32, (ts, 1), 0)
    for c in range(B_WIDTH // MIX_CHUNK):
        pin = proj(3 * A_WIDTH + c * MIX_CHUNK)
        for sl in range(slabs):
            g = c * slabs + sl
            k = POOL_WINDOWS[g]
            slot = A_WIDTH // LANES + g
            cols = slice(g * POOL_GROUP, (g + 1) * POOL_GROUP)
            cur = pin[:, sl * LANES:(sl + 1) * LANES]
            _stage_rows(scr, slot, base, cur, carry_p[:, cols])
            carry_p[:, cols] = cur[ts - POOL_HALO:, :]
            acc = cur
            for j in range(1, k):
                acc = acc + _rows_back(scr, slot, base, ts, j)
            cnt = jnp.minimum(pos + 1, k).astype(F32)
            pooled = acc / cnt - cur
            yb = jnp.dot(pooled.astype(BF16), pw_ref[g], preferred_element_type=F32)
            y_buf[:, A_WIDTH + g * POOL_GROUP:A_WIDTH + (g + 1) * POOL_GROUP] = (
                yb * ps_ref[:, cols]).astype(BF16)

    o_ref[0] = x + jnp.dot(y_buf[...], wout_ref[...], preferred_element_type=F32)


def _mixer(x, g, w_in, conv_w, pool_w, pool_scale, w_out, to_cast):
    b, s, d = x.shape
    ts = ROW_TILE
    n_steps = b * (s // ts)
    row_spec = pl.BlockSpec((1, ts, d), lambda i, j: (i, j, 0))
    cast_specs = [pl.BlockSpec((w.shape[0] // n_steps, w.shape[1]),
                               lambda i, j: (i * (s // ts) + j, 0)) for w in to_cast]
    return pl.pallas_call(
        _mixer_kernel,
        grid=(b, s // ts),
        in_specs=[row_spec, _const_spec((1, d)), _const_spec((d, EVEN_IN)),
                  _const_spec((CONV_WIDTH, A_WIDTH)),
                  _const_spec((len(POOL_WINDOWS), POOL_GROUP, POOL_GROUP)),
                  _const_spec((1, B_WIDTH)), _const_spec((d, d))] + cast_specs,
        out_specs=[row_spec] + cast_specs,
        out_shape=[jax.ShapeDtypeStruct(x.shape, F32)]
                  + [jax.ShapeDtypeStruct(w.shape, BF16) for w in to_cast],
        scratch_shapes=[pltpu.VMEM((ts, d), BF16),
                        pltpu.VMEM((CONV_HALO, A_WIDTH), F32),
                        pltpu.VMEM((POOL_HALO, B_WIDTH), F32),
                        pltpu.VMEM(((A_WIDTH + B_WIDTH) // LANES, 2 * (POOL_HALO + ts), LANES),
                                   F32)],
        compiler_params=_params(2),
        name="mixer",
    )(x, g, w_in, conv_w, pool_w, pool_scale, w_out, *to_cast)


def _ffn_kernel(x_ref, g_ref, wup_ref, cw_ref, cb_ref, wdn_ref, o_ref, h_buf, carry, u_scr):
    ts = x_ref.shape[1]
    s = pl.program_id(1)

    @pl.when(s == 0)
    def _():
        carry[...] = jnp.zeros(carry.shape, F32)

    x = x_ref[0]
    xn = _rmsnorm(x, g_ref[...]).astype(BF16)
    slabs = FF_CHUNK // LANES
    for j in range(D_FF // FF_CHUNK):
        halves = []
        for half in range(2):
            c0 = half * D_FF + j * FF_CHUNK
            u = jnp.dot(xn, wup_ref[:, c0:c0 + FF_CHUNK], preferred_element_type=F32)
            parts = []
            for sl in range(slabs):
                cols = slice(c0 + sl * LANES, c0 + (sl + 1) * LANES)
                slot = ((j % 2) * 2 + half) * slabs + sl
                u_sl = u[:, sl * LANES:(sl + 1) * LANES]
                _stage_rows(u_scr, slot, SHIFT_BASE, u_sl, carry[CONV_HALO - 2:, cols])
                carry[:, cols] = u_sl[ts - CONV_HALO:ts, :]
                w = cw_ref[:, cols]
                parts.append(w[0:1] * _rows_back(u_scr, slot, SHIFT_BASE, ts, 2)
                             + w[1:2] * _rows_back(u_scr, slot, SHIFT_BASE, ts, 1)
                             + w[2:3] * u_sl + cb_ref[:, cols])
            halves.append(jnp.concatenate(parts, axis=-1))
        gate, up = halves
        h_buf[:, j * FF_CHUNK:(j + 1) * FF_CHUNK] = (gate * jax.nn.sigmoid(gate) * up).astype(BF16)
    o_ref[0] = x + jnp.dot(h_buf[...], wdn_ref[...], preferred_element_type=F32)


def _ffn(x, layer, g, w_up, conv_w, conv_b, w_down):
    b, s, d = x.shape
    ts = ROW_TILE
    row_spec = pl.BlockSpec((1, ts, d), lambda i, j: (i, j, 0))
    return pl.pallas_call(
        _ffn_kernel,
        grid=(b, s // ts),
        in_specs=[row_spec, _layer_spec((1, d), layer), _layer_spec((d, 2 * D_FF), layer),
                  _layer_spec((CONV_WIDTH, 2 * D_FF), layer),
                  _layer_spec((1, 2 * D_FF), layer), _layer_spec((D_FF, d), layer)],
        out_specs=row_spec,
        out_shape=jax.ShapeDtypeStruct(x.shape, F32),
        scratch_shapes=[pltpu.VMEM((ts, D_FF), BF16),
                        pltpu.VMEM((CONV_HALO, 2 * D_FF), F32),
                        pltpu.VMEM((4 * FF_CHUNK // LANES, SHIFT_BASE + 2 * ts, LANES), F32)],
        compiler_params=_params(2),
        name="ffn",
    )(x, g, w_up, conv_w, conv_b, w_down)


def _head_pair_norm(t, g_pair, scale):
    lane = lax.broadcasted_iota(jnp.int32, t.shape, 1)
    lo = lane < HEAD_DIM
    sq = t * t
    s_lo = jnp.sum(jnp.where(lo, sq, 0.0), axis=-1, keepdims=True)
    s_hi = jnp.sum(jnp.where(lo, 0.0, sq), axis=-1, keepdims=True)
    r = lax.rsqrt(jnp.where(lo, s_lo, s_hi) * (1.0 / HEAD_DIM) + EPS)
    return t * r * (g_pair * scale)


def _qkv_kernel(x_ref, g_ref, w_ref, gq_ref, gk_ref, *refs):
    assert DILATIONS == (1, 4, 16)
    out4, out16 = refs[0:3], refs[3:6]
    x_stage, stage = refs[6:]
    d = D_MODEL
    ts = x_ref.shape[1]
    tj = ts // 4
    for sl in range(d // LANES):
        x_stage[sl] = x_ref[0, :, sl * LANES:(sl + 1) * LANES]
    x = jnp.concatenate(
        [jnp.concatenate([x_stage[sl, pl.ds(r4, tj, stride=4), :] for sl in range(d // LANES)],
                         axis=-1) for r4 in range(4)], axis=0)
    xn = _rmsnorm(x, g_ref[...]).astype(BF16)
    slabs = QKV_CHUNK // LANES
    for c in range(3 * d // QKV_CHUNK):
        res = jnp.dot(xn, w_ref[:, c * QKV_CHUNK:(c + 1) * QKV_CHUNK],
                      preferred_element_type=F32)
        for sl in range(slabs):
            slot = c * slabs + sl
            which, hp = divmod(slot, d // LANES)
            cols = slice(hp * LANES, (hp + 1) * LANES)
            slab = res[:, sl * LANES:(sl + 1) * LANES]
            if which == 0:
                slab = _head_pair_norm(slab, gq_ref[...], HEAD_DIM ** -0.5 * LOG2E)
            elif which == 1:
                slab = _head_pair_norm(slab, gk_ref[...], 1.0)
            stage[slot % QKV_STAGE_SLOTS] = slab
            for r4 in range(4):
                out4[which][r4, :, cols] = slab[r4 * tj:(r4 + 1) * tj].astype(BF16)
                for c4 in range(4):
                    p16 = stage[slot % QKV_STAGE_SLOTS, pl.ds(r4 * tj + c4, tj // 4, stride=4), :]
                    out16[which][4 * r4 + c4, :, cols] = p16.astype(BF16)


def _qkv(x, g, w_qkv, gq_pair, gk_pair):
    b, s, d = x.shape
    ts = QKV_ROW_TILE
    row_spec = pl.BlockSpec((1, ts, d), lambda i, j: (i, j, 0))
    out_specs, out_shapes = [], []
    for planes in (4, 16):
        out_specs += [pl.BlockSpec((None, planes, ts // planes, d), lambda i, j: (i, 0, j, 0))] * 3
        out_shapes += [jax.ShapeDtypeStruct((b, planes, s // planes, d), BF16)] * 3
    outs = pl.pallas_call(
        _qkv_kernel,
        grid=(b, s // ts),
        in_specs=[row_spec, _const_spec((1, d)), _const_spec((d, 3 * d)),
                  _const_spec((1, LANES)), _const_spec((1, LANES))],
        out_specs=out_specs,
        out_shape=out_shapes,
        scratch_shapes=[pltpu.VMEM((d // LANES, ts, LANES), F32),
                        pltpu.VMEM((QKV_STAGE_SLOTS, ts, LANES), F32)],
        compiler_params=_params(2),
        name="qkv",
    )(x, g, w_qkv, gq_pair, gk_pair)
    return outs[0:3], outs[3:6]


def _stat_lane(h):
    return HEAD_DIM + h // 2 if h % 2 == 0 else h // 2


def _rows_cat(parts):
    return parts[0] if len(parts) == 1 else jnp.concatenate(parts, axis=0)


def _attn_kernel(q_ref, k_ref, v_ref, kh_ref, vh_ref, bias_ref, o_ref, m_ref, l_ref, *scratch,
                 split, interleave):
    nb = ATT_BLOCK
    n_planes, rows, _ = q_ref.shape
    piece = nb // split
    col = lax.broadcasted_iota(jnp.int32, (nb, 2 * nb), 1)
    before_start = jnp.logical_and(col % (2 * piece) < piece, pl.program_id(2) == 0)
    if interleave:
        o_stage, m_tiles, l_tiles = scratch

    lo = lax.broadcasted_iota(jnp.int32, (nb, LANES), 1) < HEAD_DIM
    lo_kv = lax.broadcasted_iota(jnp.int32, (2 * nb, LANES), 1) < HEAD_DIM

    for grp in range(n_planes // split):
        planes = range(grp * split, (grp + 1) * split)
        for u in range(rows // piece):
            cur = slice(u * piece, (u + 1) * piece)
            tile = grp * (rows // piece) + u
            out_rows = pl.ds(interleave * nb * u + grp, nb, stride=interleave) if interleave else None

            def block_rows(ref, cols):
                return _rows_cat([ref[p, cur, cols] for p in planes])

            def window_rows(ref, halo_ref, cols):
                parts = []
                for p in planes:
                    if u == 0:
                        parts += [halo_ref[p, :, cols], ref[p, cur, cols]]
                    else:
                        parts.append(ref[p, (u - 1) * piece:(u + 1) * piece, cols])
                return _rows_cat(parts)

            def put_stat(ref, tiles, lane_cols, value):
                if interleave:
                    tiles[tile, :, lane_cols] = value
                else:
                    for n, p in enumerate(planes):
                        ref[p, cur, lane_cols] = value[n * piece:(n + 1) * piece]

            put_stat(m_ref, m_tiles if interleave else None, slice(None), jnp.zeros((nb, LANES), F32))
            put_stat(l_ref, l_tiles if interleave else None, slice(None), jnp.ones((nb, LANES), F32))
            for hp in range(N_HEADS // 2):
                cols = slice(hp * LANES, (hp + 1) * LANES)
                q_pair = block_rows(q_ref, cols)
                k_win = window_rows(k_ref, kh_ref, cols)
                v_win = window_rows(v_ref, vh_ref, cols)
                outs = []
                for sub in range(2):
                    h = 2 * hp + sub
                    own = lo if sub == 0 else jnp.logical_not(lo)
                    own_kv = lo_kv if sub == 0 else jnp.logical_not(lo_kv)
                    q_h = jnp.where(own, q_pair, jnp.zeros_like(q_pair))
                    sc = lax.dot_general(q_h, k_win, (((1,), (1,)), ((), ())),
                                         preferred_element_type=F32)
                    sc = sc + bias_ref[h]
                    if u == 0:
                        sc = jnp.where(before_start, MASK_VALUE, sc)
                    m = jnp.max(sc, axis=-1, keepdims=True)
                    p = jnp.exp2(sc - m).astype(BF16)
                    v_h = jnp.where(own_kv, v_win, jnp.ones_like(v_win))
                    pv = jnp.dot(p, v_h, preferred_element_type=F32)
                    outs.append(pv)
                    sl = slice(_stat_lane(h), _stat_lane(h) + 1)
                    put_stat(m_ref, m_tiles if interleave else None, sl, m)
                    put_stat(l_ref, l_tiles if interleave else None, sl, pv[:, sl])
                o_pair = jnp.where(lo, outs[0], outs[1])
                if interleave:
                    o_stage[hp, out_rows, :] = o_pair
                else:
                    for n, p in enumerate(planes):
                        o_ref[p, cur, cols] = o_pair[n * piece:(n + 1) * piece].astype(BF16)
            if interleave:
                m_ref[out_rows, :] = m_tiles[tile]
                l_ref[out_rows, :] = l_tiles[tile]
    if interleave:
        for hp in range(N_HEADS // 2):
            o_ref[:, hp * LANES:(hp + 1) * LANES] = o_stage[hp].astype(BF16)


def _attn_branch(q, k, v, bias, *, branch, split=1, interleave=0):
    b, n_planes, length, d = q.shape
    piece = ATT_BLOCK // split
    units = min(ATT_UNITS_PER_STEP, length // piece)
    step_planes = interleave or split * min(n_planes // split, ATT_UNITS_PER_STEP // units)
    rows = units * piece
    main = pl.BlockSpec((None, step_planes, rows, d), lambda bi, r, i: (bi, r, i, 0))
    halo = pl.BlockSpec((None, step_planes, piece, d),
                        lambda bi, r, i: (bi, r, jnp.maximum(i * units - 1, 0), 0))
    scratch = []
    if interleave:
        assert interleave == step_planes and rows == length and split == 1
        out_planes, out_len = n_planes // interleave, interleave * length
        o_spec = pl.BlockSpec((None, None, out_len, d), lambda bi, r, i: (bi, r, 0, 0))
        stat = pl.BlockSpec((None, None, out_len, LANES), lambda bi, r, i: (bi, r, 0, 0))
        n_tiles = step_planes * units
        scratch = [pltpu.VMEM((d // LANES, out_len, LANES), F32),
                   pltpu.VMEM((n_tiles, ATT_BLOCK, LANES), F32),
                   pltpu.VMEM((n_tiles, ATT_BLOCK, LANES), F32)]
    else:
        out_planes, out_len = n_planes, length
        o_spec = main
        stat = pl.BlockSpec((None, step_planes, rows, LANES), lambda bi, r, i: (bi, r, i, 0))
    stat_shape = jax.ShapeDtypeStruct((b, out_planes, out_len, LANES), F32)
    return pl.pallas_call(
        functools.partial(_attn_kernel, split=split, interleave=interleave),
        grid=(b, n_planes // step_planes, length // rows),
        in_specs=[main, main, main, halo, halo, _layer_spec(bias.shape[1:], branch)],
        out_specs=[o_spec, stat, stat],
        out_shape=[jax.ShapeDtypeStruct((b, out_planes, out_len, d), BF16), stat_shape, stat_shape],
        scratch_shapes=scratch,
        compiler_params=_params(3),
        name=f"attn_d{DILATIONS[branch]}",
    )(q, k, v, k, v, bias)


def _merge_kernel(x_ref, *refs):
    n_br = len(DILATIONS)
    branch_refs = [refs[3 * i:3 * i + 3] for i in range(n_br)]
    e_ref, wo_ref, out_ref, y_scr = refs[3 * n_br:]
    planes, tj, _ = branch_refs[0][0].shape

    def plane_major(ref):
        return jnp.concatenate([ref[p] for p in range(planes)], axis=0)

    ms = [plane_major(br[1]) for br in branch_refs]
    ls = [plane_major(br[2]) for br in branch_refs]
    m_max = functools.reduce(jnp.maximum, ms)
    es = [jnp.exp2(m - m_max) for m in ms]
    total = functools.reduce(jnp.add, [e * l for e, l in zip(es, ls)])
    merged = None
    for e, br in zip(es, branch_refs):
        c = e / total
        c_hi = c.astype(BF16)
        c_lo = (c - c_hi.astype(F32)).astype(BF16)
        c_wide = jnp.dot(jnp.concatenate([c_hi, c_lo], axis=-1), e_ref[...],
                         preferred_element_type=F32)
        term = c_wide * plane_major(br[0]).astype(F32)
        merged = term if merged is None else merged + term
    y = jnp.dot(merged.astype(BF16), wo_ref[...], preferred_element_type=F32)
    for slab in range(D_MODEL // LANES):
        cols = slice(slab * LANES, (slab + 1) * LANES)
        for p in range(planes):
            y_scr[slab, pl.ds(p, tj, stride=planes), :] = y[p * tj:(p + 1) * tj, cols]
        out_ref[0, :, cols] = x_ref[0, :, cols] + y_scr[slab]


def _merge_out_proj(x, branches, expand, w_o):
    b, s, d = x.shape
    ts = ROW_TILE
    planes = branches[0][0].shape[1]
    row_spec = pl.BlockSpec((1, ts, d), lambda i, j: (i, j, 0))
    o_spec = pl.BlockSpec((None, planes, ts // planes, d), lambda i, j: (i, 0, j, 0))
    stat_spec = pl.BlockSpec((None, planes, ts // planes, LANES), lambda i, j: (i, 0, j, 0))
    args = [t for br in branches for t in br]
    return pl.pallas_call(
        _merge_kernel,
        grid=(b, s // ts),
        in_specs=[row_spec] + [o_spec, stat_spec, stat_spec] * len(branches)
                 + [_const_spec(expand.shape), _const_spec((d, d))],
        out_specs=row_spec,
        out_shape=jax.ShapeDtypeStruct(x.shape, F32),
        scratch_shapes=[pltpu.VMEM((d // LANES, ts, LANES), F32)],
        compiler_params=_params(2),
        name="merge_out_proj",
    )(x, *args, expand, w_o)


def _t5_causal_bucket(dist):
    max_exact = N_REL_BUCKETS // 2
    dd = jnp.maximum(dist, 1).astype(F32)
    large = max_exact + (jnp.log(dd / max_exact) / math.log(REL_MAX_DISTANCE / max_exact)
                         * (N_REL_BUCKETS - max_exact)).astype(jnp.int32)
    large = jnp.minimum(large, N_REL_BUCKETS - 1)
    return jnp.where(dist < max_exact, dist, large)


def _bias_tables(rel_table, splits):
    n = ATT_BLOCK
    n_br = len(DILATIONS)
    n_heads = rel_table.shape[1]
    assert splits[0] in (1, 4) and all(s == 1 for s in splits[1:])
    dist = (n - jnp.arange(n + 1))[None, :] * jnp.asarray(DILATIONS)[:, None]
    by_offset = LOG2E * jnp.swapaxes(rel_table.astype(F32)[_t5_causal_bucket(dist)], 1, 2)

    def mask_fill(count):
        return jnp.full((n_heads, count), MASK_VALUE, F32)

    split = splits[0]
    rows = []
    for br in range(n_br):
        w = by_offset[br]
        if br > 0 or split == 1:
            rows.append(jnp.stack([jnp.concatenate([w, mask_fill(4 * n - (n + 1))], axis=1)] * 4, 1))
            continue
        groups = []
        for r in range(split):
            parts = []
            for rp in range(split):
                first = 0 if rp >= r else 1
                vals = w[:, split * first + rp - r::split]
                parts += [mask_fill(first), vals, mask_fill(n - first - vals.shape[1])]
            groups.append(jnp.concatenate(parts, axis=1))
        rows.append(jnp.stack(groups, axis=1))
    rows = jnp.stack(rows)
    return pl.pallas_call(
        functools.partial(_bias_kernel, split=split),
        out_shape=jax.ShapeDtypeStruct((n_br, n_heads, n, 2 * n), F32),
        name="bias_tables",
    )(rows[:, :, :, None, :])


def _bias_kernel(rows_ref, out_ref, *, split):
    n = ATT_BLOCK
    n_br, n_heads = out_ref.shape[:2]
    for br in range(n_br):
        sp = split if br == 0 else 1
        for h in range(n_heads):
            if sp == 1:
                wide = jnp.broadcast_to(rows_ref[br, h, 0], (n, 4 * n))
                tab = pltpu.roll(wide, 0, 1, stride=1, stride_axis=0)[:, :2 * n]
            else:
                groups = []
                for r in range(sp):
                    wide = jnp.broadcast_to(rows_ref[br, h, r], (n // sp, 4 * n))
                    rolled = pltpu.roll(wide, 0, 1, stride=1, stride_axis=0)
                    groups.append(jnp.concatenate(
                        [rolled[:, rp * n:rp * n + 2 * n // sp] for rp in range(sp)], axis=1))
                tab = jnp.concatenate(groups, axis=0)
            out_ref[br, h] = tab


def _head_expand_matrix():
    row = jnp.arange(2 * LANES)[:, None] % LANES
    head = jnp.arange(D_MODEL)[None, :] // HEAD_DIM
    stat_lane = jnp.where(head % 2 == 0, HEAD_DIM + head // 2, head // 2)
    return (row == stat_lane).astype(BF16)


def kernel(x, rel_bias, even_norm, even_w_in, even_conv_w, even_pool_w, even_pool_scale,
           even_w_out, odd_norm, odd_w_qkv, odd_q_norm, odd_k_norm, odd_w_o, ffn_norm,
           ffn_w_up, ffn_conv_w, ffn_conv_b, ffn_w_down):
    depth, d, d_ff2 = ffn_w_up.shape
    x, w_up, w_down, w_qkv, w_o = _mixer(
        x, even_norm[0][None], even_w_in[0].astype(BF16), even_conv_w[0],
        even_pool_w[0].astype(BF16), even_pool_scale[0][None], even_w_out[0].astype(BF16),
        to_cast=[ffn_w_up.reshape(depth * d, d_ff2), ffn_w_down.reshape(depth * D_FF, d),
                 odd_w_qkv[0], odd_w_o[0]])
    ffn_params = (ffn_norm[:, None], w_up.reshape(depth, d, d_ff2), ffn_conv_w,
                  ffn_conv_b[:, None], w_down.reshape(depth, D_FF, d))

    def ffn(t, layer):
        return _ffn(t, layer, *ffn_params)

    x = ffn(x, 0)

    gq_pair = jnp.tile(odd_q_norm[0], 2)[None]
    gk_pair = jnp.tile(odd_k_norm[0], 2)[None]
    qkv4, qkv16 = _qkv(x, odd_norm[0][None], w_qkv, gq_pair, gk_pair)
    bias = _bias_tables(rel_bias, splits=(4, 1, 1))
    branches = [
        _attn_branch(*qkv4, bias, branch=0, split=4),
        _attn_branch(*qkv4, bias, branch=1),
        _attn_branch(*qkv16, bias, branch=2, interleave=4),
    ]
    x = _merge_out_proj(x, branches, _head_expand_matrix(), w_o)
    x = ffn(x, 1)
    return x
```

```python
import functools
import math

import jax
import jax.numpy as jnp
import numpy as np
from jax import lax
from jax.experimental import pallas as pl
from jax.experimental.pallas import tpu as pltpu

D_MODEL = 1024
CONV_WIDTH = 3
A_WIDTH = 512
B_WIDTH = 512
POOL_WINDOWS = (2, 4, 8, 16)
POOL_GROUP = 128
EVEN_IN = 3 * A_WIDTH + B_WIDTH
HEAD_DIM = 64
N_HEADS = 16
DILATED_PAIRS = ((128, 1), (512, 4), (2048, 16))
DILATIONS = tuple(dil for _, dil in DILATED_PAIRS)
N_REL_BUCKETS = 32
REL_MAX_DISTANCE = 2048
D_FF = 2816
EPS = 1e-6
MASK_VALUE = -1e30

BF16 = jnp.bfloat16
F32 = jnp.float32

LANES = 128
ATT_BLOCK = 128
ATT_UNITS_PER_STEP = 16
LOG2E = math.log2(math.e)
QKV_ROW_TILE = 1024
QKV_STAGE_SLOTS = 8
ROW_TILE = 1024
POOL_HALO = 16
CONV_HALO = 8
FF_CHUNK = 256
QKV_CHUNK = 256
MIX_CHUNK = 256
MERGE_CHUNK = 256
SHIFT_BASE = 8
VMEM_LIMIT = 56 * 1024 * 1024


def _const_spec(shape):
    nd = len(shape)
    return pl.BlockSpec(shape, lambda *_: (0,) * nd, pipeline_mode=pl.Buffered(1))


def _layer_spec(shape, layer):
    nd = len(shape)
    return pl.BlockSpec((None,) + tuple(shape), lambda *_: (layer,) + (0,) * nd,
                        pipeline_mode=pl.Buffered(1))


def _params(n_axes):
    return pltpu.CompilerParams(
        dimension_semantics=("arbitrary",) * n_axes, vmem_limit_bytes=VMEM_LIMIT)


def _rmsnorm(x, g):
    ms = jnp.mean(x * x, axis=-1, keepdims=True)
    return x * lax.rsqrt(ms + EPS) * g


def _stage_rows(scr, slot, base, tile, prev):
    n_prev = prev.shape[0]
    scr[slot, pl.ds(base - 2 * n_prev, n_prev, stride=2), :] = prev
    scr[slot, pl.ds(base, tile.shape[0], stride=2), :] = tile


def _rows_back(scr, slot, base, n_rows, k):
    return scr[slot, pl.ds(base - 2 * k, n_rows, stride=2), :]


def _mixer_kernel(x_ref, g_ref, win_ref, cw_ref, pw_ref, ps_ref, wout_ref, *refs):
    n_cast = (len(refs) - 5) // 2
    cast_src, o_ref, cast_dst = refs[:n_cast], refs[n_cast], refs[n_cast + 1:2 * n_cast + 1]
    y_buf, carry_a, carry_p, scr = refs[2 * n_cast + 1:]
    for src, dst in zip(cast_src, cast_dst):
        dst[...] = src[...].astype(BF16)
    ts = x_ref.shape[1]
    s = pl.program_id(1)
    base = 2 * POOL_HALO

    @pl.when(s == 0)
    def _():
        carry_a[...] = jnp.zeros(carry_a.shape, F32)
        carry_p[...] = jnp.zeros(carry_p.shape, F32)

    x = x_ref[0]
    xn = _rmsnorm(x, g_ref[...]).astype(BF16)
    slabs = MIX_CHUNK // LANES

    def proj(col0):
        return jnp.dot(xn, win_ref[:, col0:col0 + MIX_CHUNK], preferred_element_type=F32)

    for c in range(A_WIDTH // MIX_CHUNK):
        h = proj(c * MIX_CHUNK)
        gate_b = proj(A_WIDTH + c * MIX_CHUNK)
        gate_c = proj(2 * A_WIDTH + c * MIX_CHUNK)
        ch = gate_c * h
        for sl in range(slabs):
            slot = c * slabs + sl
            cols = slice(slot * LANES, (slot + 1) * LANES)
            ch_sl = ch[:, sl * LANES:(sl + 1) * LANES]
            _stage_rows(scr, slot, base, ch_sl, carry_a[CONV_HALO - 2:, cols])
            carry_a[:, cols] = ch_sl[ts - CONV_HALO:, :]
            cw = cw_ref[:, cols]
            conv = (cw[0:1] * _rows_back(scr, slot, base, ts, 2)
                    + cw[1:2] * _rows_back(scr, slot, base, ts, 1) + cw[2:3] * ch_sl)
            y_buf[:, cols] = (gate_b[:, sl * LANES:(sl + 1) * LANES] * conv).astype(BF16)

    pos = s * ts + lax.broadcasted_iota(jnp.int32, (ts, 1), 0)
    for c in range(B_WIDTH // MIX_CHUNK):
        pin = proj(3 * A_WIDTH + c * MIX_CHUNK)
        for sl in range(slabs):
            g = c * slabs + sl
            k = POOL_WINDOWS[g]
            slot = A_WIDTH // LANES + g
            cols = slice(g * POOL_GROUP, (g + 1) * POOL_GROUP)
            cur = pin[:, sl * LANES:(sl + 1) * LANES]
            _stage_rows(scr, slot, base, cur, carry_p[:, cols])
            carry_p[:, cols] = cur[ts - POOL_HALO:, :]
            acc = cur
            for j in range(1, k):
                acc = acc + _rows_back(scr, slot, base, ts, j)
            cnt = jnp.minimum(pos + 1, k).astype(F32)
            pooled = acc / cnt - cur
            yb = jnp.dot(pooled.astype(BF16), pw_ref[g], preferred_element_type=F32)
            y_buf[:, A_WIDTH + g * POOL_GROUP:A_WIDTH + (g + 1) * POOL_GROUP] = (
                yb * ps_ref[:, cols]).astype(BF16)

    o_ref[0] = x + jnp.dot(y_buf[...], wout_ref[...], preferred_element_type=F32)


def _mixer(x, g, w_in, conv_w, pool_w, pool_scale, w_out, to_cast):
    b, s, d = x.shape
    ts = ROW_TILE
    n_steps = b * (s // ts)
    row_spec = pl.BlockSpec((1, ts, d), lambda i, j: (i, j, 0))
    cast_specs = [pl.BlockSpec((w.shape[0] // n_steps, w.shape[1]),
                               lambda i, j: (i * (s // ts) + j, 0)) for w in to_cast]
    return pl.pallas_call(
        _mixer_kernel,
        grid=(b, s // ts),
        in_specs=[row_spec, _const_spec((1, d)), _const_spec((d, EVEN_IN)),
                  _const_spec((CONV_WIDTH, A_WIDTH)),
                  _const_spec((len(POOL_WINDOWS), POOL_GROUP, POOL_GROUP)),
                  _const_spec((1, B_WIDTH)), _const_spec((d, d))] + cast_specs,
        out_specs=[row_spec] + cast_specs,
        out_shape=[jax.ShapeDtypeStruct(x.shape, F32)]
                  + [jax.ShapeDtypeStruct(w.shape, BF16) for w in to_cast],
        scratch_shapes=[pltpu.VMEM((ts, d), BF16),
                        pltpu.VMEM((CONV_HALO, A_WIDTH), F32),
                        pltpu.VMEM((POOL_HALO, B_WIDTH), F32),
                        pltpu.VMEM(((A_WIDTH + B_WIDTH) // LANES, 2 * (POOL_HALO + ts), LANES),
                                   F32)],
        compiler_params=_params(2),
        name="mixer",
    )(x, g, w_in, conv_w, pool_w, pool_scale, w_out, *to_cast)


def _ffn_kernel(x_ref, g_ref, wup_ref, cw_ref, cb_ref, wdn_ref, o_ref, h_buf, carry, u_scr):
    ts = x_ref.shape[1]
    s = pl.program_id(1)

    @pl.when(s == 0)
    def _():
        carry[...] = jnp.zeros(carry.shape, F32)

    x = x_ref[0]
    xn = _rmsnorm(x, g_ref[...]).astype(BF16)
    slabs = FF_CHUNK // LANES
    for j in range(D_FF // FF_CHUNK):
        halves = []
        for half in range(2):
            c0 = half * D_FF + j * FF_CHUNK
            u = jnp.dot(xn, wup_ref[:, c0:c0 + FF_CHUNK], preferred_element_type=F32)
            parts = []
            for sl in range(slabs):
                cols = slice(c0 + sl * LANES, c0 + (sl + 1) * LANES)
                slot = ((j % 2) * 2 + half) * slabs + sl
                u_sl = u[:, sl * LANES:(sl + 1) * LANES]
                _stage_rows(u_scr, slot, SHIFT_BASE, u_sl, carry[CONV_HALO - 2:, cols])
                carry[:, cols] = u_sl[ts - CONV_HALO:ts, :]
                w = cw_ref[:, cols]
                parts.append(w[0:1] * _rows_back(u_scr, slot, SHIFT_BASE, ts, 2)
                             + w[1:2] * _rows_back(u_scr, slot, SHIFT_BASE, ts, 1)
                             + w[2:3] * u_sl + cb_ref[:, cols])
            halves.append(jnp.concatenate(parts, axis=-1))
        gate, up = halves
        h_buf[:, j * FF_CHUNK:(j + 1) * FF_CHUNK] = (gate * jax.nn.sigmoid(gate) * up).astype(BF16)
    o_ref[0] = x + jnp.dot(h_buf[...], wdn_ref[...], preferred_element_type=F32)


def _ffn(x, layer, g, w_up, conv_w, conv_b, w_down):
    b, s, d = x.shape
    ts = ROW_TILE
    row_spec = pl.BlockSpec((1, ts, d), lambda i, j: (i, j, 0))
    return pl.pallas_call(
        _ffn_kernel,
        grid=(b, s // ts),
        in_specs=[row_spec, _layer_spec((1, d), layer), _layer_spec((d, 2 * D_FF), layer),
                  _layer_spec((CONV_WIDTH, 2 * D_FF), layer),
                  _layer_spec((1, 2 * D_FF), layer), _layer_spec((D_FF, d), layer)],
        out_specs=row_spec,
        out_shape=jax.ShapeDtypeStruct(x.shape, F32),
        scratch_shapes=[pltpu.VMEM((ts, D_FF), BF16),
                        pltpu.VMEM((CONV_HALO, 2 * D_FF), F32),
                        pltpu.VMEM((4 * FF_CHUNK // LANES, SHIFT_BASE + 2 * ts, LANES), F32)],
        compiler_params=_params(2),
        name="ffn",
    )(x, g, w_up, conv_w, conv_b, w_down)


def _head_pair_norm(t, g_pair, scale):
    lane = lax.broadcasted_iota(jnp.int32, t.shape, 1)
    lo = lane < HEAD_DIM
    sq = t * t
    s_lo = jnp.sum(jnp.where(lo, sq, 0.0), axis=-1, keepdims=True)
    s_hi = jnp.sum(jnp.where(lo, 0.0, sq), axis=-1, keepdims=True)
    r = lax.rsqrt(jnp.where(lo, s_lo, s_hi) * (1.0 / HEAD_DIM) + EPS)
    return t * r * (g_pair * scale)


def _qkv_kernel(x_ref, g_ref, w_ref, gq_ref, gk_ref, *refs):
    assert DILATIONS == (1, 4, 16)
    out4, out16 = refs[0:3], refs[3:6]
    x_stage, stage = refs[6:]
    d = D_MODEL
    ts = x_ref.shape[1]
    tj = ts // 4
    for sl in range(d // LANES):
        x_stage[sl] = x_ref[0, :, sl * LANES:(sl + 1) * LANES]
    x = jnp.concatenate(
        [jnp.concatenate([x_stage[sl, pl.ds(r4, tj, stride=4), :] for sl in range(d // LANES)],
                         axis=-1) for r4 in range(4)], axis=0)
    xn = _rmsnorm(x, g_ref[...]).astype(BF16)
    slabs = QKV_CHUNK // LANES
    for c in range(3 * d // QKV_CHUNK):
        res = jnp.dot(xn, w_ref[:, c * QKV_CHUNK:(c + 1) * QKV_CHUNK],
                      preferred_element_type=F32)
        for sl in range(slabs):
            slot = c * slabs + sl
            which, hp = divmod(slot, d // LANES)
            cols = slice(hp * LANES, (hp + 1) * LANES)
            slab = res[:, sl * LANES:(sl + 1) * LANES]
            if which == 0:
                slab = _head_pair_norm(slab, gq_ref[...], HEAD_DIM ** -0.5 * LOG2E)
            elif which == 1:
                slab = _head_pair_norm(slab, gk_ref[...], 1.0)
            stage[slot % QKV_STAGE_SLOTS] = slab
            for r4 in range(4):
                out4[which][r4, :, cols] = slab[r4 * tj:(r4 + 1) * tj].astype(BF16)
                for c4 in range(4):
                    p16 = stage[slot % QKV_STAGE_SLOTS, pl.ds(r4 * tj + c4, tj // 4, stride=4), :]
                    out16[which][4 * r4 + c4, :, cols] = p16.astype(BF16)


def _qkv(x, g, w_qkv, gq_pair, gk_pair):
    b, s, d = x.shape
    ts = QKV_ROW_TILE
    row_spec = pl.BlockSpec((1, ts, d), lambda i, j: (i, j, 0))
    out_specs, out_shapes = [], []
    for planes in (4, 16):
        out_specs += [pl.BlockSpec((None, planes, ts // planes, d), lambda i, j: (i, 0, j, 0))] * 3
        out_shapes += [jax.ShapeDtypeStruct((b, planes, s // planes, d), BF16)] * 3
    outs = pl.pallas_call(
        _qkv_kernel,
        grid=(b, s // ts),
        in_specs=[row_spec, _const_spec((1, d)), _const_spec((d, 3 * d)),
                  _const_spec((1, LANES)), _const_spec((1, LANES))],
        out_specs=out_specs,
        out_shape=out_shapes,
        scratch_shapes=[pltpu.VMEM((d // LANES, ts, LANES), F32),
                        pltpu.VMEM((QKV_STAGE_SLOTS, ts, LANES), F32)],
        compiler_params=_params(2),
        name="qkv",
    )(x, g, w_qkv, gq_pair, gk_pair)
    return outs[0:3], outs[3:6]


def _stat_lane(h):
    return HEAD_DIM + h // 2 if h % 2 == 0 else h // 2


def _rows_cat(parts):
    return parts[0] if len(parts) == 1 else jnp.concatenate(parts, axis=0)


def _attn_kernel(q_ref, k_ref, v_ref, kh_ref, vh_ref, bias_ref, o_ref, m_ref, l_ref, *scratch,
                 split, interleave):
    nb = ATT_BLOCK
    n_planes, rows, _ = q_ref.shape
    piece = nb // split
    col = lax.broadcasted_iota(jnp.int32, (nb, 2 * nb), 1)
    before_start = jnp.logical_and(col % (2 * piece) < piece, pl.program_id(2) == 0)
    if interleave:
        o_stage, m_tiles, l_tiles = scratch

    lo = lax.broadcasted_iota(jnp.int32, (nb, LANES), 1) < HEAD_DIM
    lo_kv = lax.broadcasted_iota(jnp.int32, (2 * nb, LANES), 1) < HEAD_DIM

    for grp in range(n_planes // split):
        planes = range(grp * split, (grp + 1) * split)
        for u in range(rows // piece):
            cur = slice(u * piece, (u + 1) * piece)
            tile = grp * (rows // piece) + u
            out_rows = pl.ds(interleave * nb * u + grp, nb, stride=interleave) if interleave else None

            def block_rows(ref, cols):
                return _rows_cat([ref[p, cur, cols] for p in planes])

            def window_rows(ref, halo_ref, cols):
                parts = []
                for p in planes:
                    if u == 0:
                        parts += [halo_ref[p, :, cols], ref[p, cur, cols]]
                    else:
                        parts.append(ref[p, (u - 1) * piece:(u + 1) * piece, cols])
                return _rows_cat(parts)

            def put_stat(ref, tiles, lane_cols, value):
                if interleave:
                    tiles[tile, :, lane_cols] = value
                else:
                    for n, p in enumerate(planes):
                        ref[p, cur, lane_cols] = value[n * piece:(n + 1) * piece]

            put_stat(m_ref, m_tiles if interleave else None, slice(None), jnp.zeros((nb, LANES), F32))
            put_stat(l_ref, l_tiles if interleave else None, slice(None), jnp.ones((nb, LANES), F32))
            for hp in range(N_HEADS // 2):
                cols = slice(hp * LANES, (hp + 1) * LANES)
                q_pair = block_rows(q_ref, cols)
                k_win = window_rows(k_ref, kh_ref, cols)
                v_win = window_rows(v_ref, vh_ref, cols)
                outs = []
                for sub in range(2):
                    h = 2 * hp + sub
                    own = lo if sub == 0 else jnp.logical_not(lo)
                    own_kv = lo_kv if sub == 0 else jnp.logical_not(lo_kv)
                    q_h = jnp.where(own, q_pair, jnp.zeros_like(q_pair))
                    sc = lax.dot_general(q_h, k_win, (((1,), (1,)), ((), ())),
                                         preferred_element_type=F32)
                    sc = sc + bias_ref[h]
                    if u == 0:
                        sc = jnp.where(before_start, MASK_VALUE, sc)
                    m = jnp.max(sc, axis=-1, keepdims=True)
                    p = jnp.exp2(sc - m).astype(BF16)
                    v_h = jnp.where(own_kv, v_win, jnp.ones_like(v_win))
                    pv = jnp.dot(p, v_h, preferred_element_type=F32)
                    outs.append(pv)
                    sl = slice(_stat_lane(h), _stat_lane(h) + 1)
                    put_stat(m_ref, m_tiles if interleave else None, sl, m)
                    put_stat(l_ref, l_tiles if interleave else None, sl, pv[:, sl])
                o_pair = jnp.where(lo, outs[0], outs[1])
                if interleave:
                    o_stage[hp, out_rows, :] = o_pair
                else:
                    for n, p in enumerate(planes):
                        o_ref[p, cur, cols] = o_pair[n * piece:(n + 1) * piece].astype(BF16)
            if interleave:
                m_ref[out_rows, :] = m_tiles[tile]
                l_ref[out_rows, :] = l_tiles[tile]
    if interleave:
        for hp in range(N_HEADS // 2):
            o_ref[:, hp * LANES:(hp + 1) * LANES] = o_stage[hp].astype(BF16)


def _attn_branch(q, k, v, bias, *, branch, split=1, interleave=0):
    b, n_planes, length, d = q.shape
    piece = ATT_BLOCK // split
    units = min(ATT_UNITS_PER_STEP, length // piece)
    step_planes = interleave or split * min(n_planes // split, ATT_UNITS_PER_STEP // units)
    rows = units * piece
    main = pl.BlockSpec((None, step_planes, rows, d), lambda bi, r, i: (bi, r, i, 0))
    halo = pl.BlockSpec((None, step_planes, piece, d),
                        lambda bi, r, i: (bi, r, jnp.maximum(i * units - 1, 0), 0))
    scratch = []
    if interleave:
        assert interleave == step_planes and rows == length and split == 1
        out_planes, out_len = n_planes // interleave, interleave * length
        o_spec = pl.BlockSpec((None, None, out_len, d), lambda bi, r, i: (bi, r, 0, 0))
        stat = pl.BlockSpec((None, None, out_len, LANES), lambda bi, r, i: (bi, r, 0, 0))
        n_tiles = step_planes * units
        scratch = [pltpu.VMEM((d // LANES, out_len, LANES), F32),
                   pltpu.VMEM((n_tiles, ATT_BLOCK, LANES), F32),
                   pltpu.VMEM((n_tiles, ATT_BLOCK, LANES), F32)]
    else:
        out_planes, out_len = n_planes, length
        o_spec = main
        stat = pl.BlockSpec((None, step_planes, rows, LANES), lambda bi, r, i: (bi, r, i, 0))
    stat_shape = jax.ShapeDtypeStruct((b, out_planes, out_len, LANES), F32)
    return pl.pallas_call(
        functools.partial(_attn_kernel, split=split, interleave=interleave),
        grid=(b, n_planes // step_planes, length // rows),
        in_specs=[main, main, main, halo, halo, _layer_spec(bias.shape[1:], branch)],
        out_specs=[o_spec, stat, stat],
        out_shape=[jax.ShapeDtypeStruct((b, out_planes, out_len, d), BF16), stat_shape, stat_shape],
        scratch_shapes=scratch,
        compiler_params=_params(3),
        name=f"attn_d{DILATIONS[branch]}",
    )(q, k, v, k, v, bias)


def _merge_kernel(x_ref, *refs):
    n_br = len(DILATIONS)
    branch_refs = [refs[3 * i:3 * i + 3] for i in range(n_br)]
    e_ref, wo_ref, out_ref, y_scr, merged_buf = refs[3 * n_br:]
    planes, tj, _ = branch_refs[0][0].shape

    def plane_major(ref, cols=slice(None)):
        return jnp.concatenate([ref[p, :, cols] for p in range(planes)], axis=0)

    ms = [plane_major(br[1]) for br in branch_refs]
    ls = [plane_major(br[2]) for br in branch_refs]
    m_max = functools.reduce(jnp.maximum, ms)
    es = [jnp.exp2(m - m_max) for m in ms]
    total = functools.reduce(jnp.add, [e * l for e, l in zip(es, ls)])
    weights = []
    for e in es:
        c = e / total
        c_hi = c.astype(BF16)
        weights.append(jnp.concatenate([c_hi, (c - c_hi.astype(F32)).astype(BF16)], axis=-1))
    for chunk in range(D_MODEL // MERGE_CHUNK):
        cols = slice(chunk * MERGE_CHUNK, (chunk + 1) * MERGE_CHUNK)
        merged = None
        for w, br in zip(weights, branch_refs):
            c_wide = jnp.dot(w, e_ref[:, cols], preferred_element_type=F32)
            term = c_wide * plane_major(br[0], cols).astype(F32)
            merged = term if merged is None else merged + term
        merged_buf[:, cols] = merged.astype(BF16)
    y = jnp.dot(merged_buf[...], wo_ref[...], preferred_element_type=F32)
    for slab in range(D_MODEL // LANES):
        cols = slice(slab * LANES, (slab + 1) * LANES)
        for p in range(planes):
            y_scr[slab, pl.ds(p, tj, stride=planes), :] = y[p * tj:(p + 1) * tj, cols]
        out_ref[0, :, cols] = x_ref[0, :, cols] + y_scr[slab]


def _merge_out_proj(x, branches, expand, w_o):
    b, s, d = x.shape
    ts = ROW_TILE
    planes = branches[0][0].shape[1]
    row_spec = pl.BlockSpec((1, ts, d), lambda i, j: (i, j, 0))
    o_spec = pl.BlockSpec((None, planes, ts // planes, d), lambda i, j: (i, 0, j, 0))
    stat_spec = pl.BlockSpec((None, planes, ts // planes, LANES), lambda i, j: (i, 0, j, 0))
    args = [t for br in branches for t in br]
    return pl.pallas_call(
        _merge_kernel,
        grid=(b, s // ts),
        in_specs=[row_spec] + [o_spec, stat_spec, stat_spec] * len(branches)
                 + [_const_spec(expand.shape), _const_spec((d, d))],
        out_specs=row_spec,
        out_shape=jax.ShapeDtypeStruct(x.shape, F32),
        scratch_shapes=[pltpu.VMEM((d // LANES, ts, LANES), F32), pltpu.VMEM((ts, d), BF16)],
        compiler_params=_params(2),
        name="merge_out_proj",
    )(x, *args, expand, w_o)


def _t5_causal_bucket(dist):
    max_exact = N_REL_BUCKETS // 2
    dd = jnp.maximum(dist, 1).astype(F32)
    large = max_exact + (jnp.log(dd / max_exact) / math.log(REL_MAX_DISTANCE / max_exact)
                         * (N_REL_BUCKETS - max_exact)).astype(jnp.int32)
    large = jnp.minimum(large, N_REL_BUCKETS - 1)
    return jnp.where(dist < max_exact, dist, large)


def _bias_tables(rel_table, splits):
    n = ATT_BLOCK
    n_br = len(DILATIONS)
    n_heads = rel_table.shape[1]
    assert splits[0] in (1, 4) and all(s == 1 for s in splits[1:])
    dist = (n - jnp.arange(n + 1))[None, :] * jnp.asarray(DILATIONS)[:, None]
    by_offset = LOG2E * jnp.swapaxes(rel_table.astype(F32)[_t5_causal_bucket(dist)], 1, 2)

    def mask_fill(count):
        return jnp.full((n_heads, count), MASK_VALUE, F32)

    split = splits[0]
    rows = []
    for br in range(n_br):
        w = by_offset[br]
        if br > 0 or split == 1:
            rows.append(jnp.stack([jnp.concatenate([w, mask_fill(4 * n - (n + 1))], axis=1)] * 4, 1))
            continue
        groups = []
        for r in range(split):
            parts = []
            for rp in range(split):
                first = 0 if rp >= r else 1
                vals = w[:, split * first + rp - r::split]
                parts += [mask_fill(first), vals, mask_fill(n - first - vals.shape[1])]
            groups.append(jnp.concatenate(parts, axis=1))
        rows.append(jnp.stack(groups, axis=1))
    rows = jnp.stack(rows)
    return pl.pallas_call(
        functools.partial(_bias_kernel, split=split),
        out_shape=jax.ShapeDtypeStruct((n_br, n_heads, n, 2 * n), F32),
        name="bias_tables",
    )(rows[:, :, :, None, :])


def _bias_kernel(rows_ref, out_ref, *, split):
    n = ATT_BLOCK
    n_br, n_heads = out_ref.shape[:2]
    for br in range(n_br):
        sp = split if br == 0 else 1
        for h in range(n_heads):
            if sp == 1:
                wide = jnp.broadcast_to(rows_ref[br, h, 0], (n, 4 * n))
                tab = pltpu.roll(wide, 0, 1, stride=1, stride_axis=0)[:, :2 * n]
            else:
                groups = []
                for r in range(sp):
                    wide = jnp.broadcast_to(rows_ref[br, h, r], (n // sp, 4 * n))
                    rolled = pltpu.roll(wide, 0, 1, stride=1, stride_axis=0)
                    groups.append(jnp.concatenate(
                        [rolled[:, rp * n:rp * n + 2 * n // sp] for rp in range(sp)], axis=1))
                tab = jnp.concatenate(groups, axis=0)
            out_ref[br, h] = tab


def _head_expand_matrix():
    row = jnp.arange(2 * LANES)[:, None] % LANES
    head = jnp.arange(D_MODEL)[None, :] // HEAD_DIM
    stat_lane = jnp.where(head % 2 == 0, HEAD_DIM + head // 2, head // 2)
    return (row == stat_lane).astype(BF16)


def kernel(x, rel_bias, even_norm, even_w_in, even_conv_w, even_pool_w, even_pool_scale,
           even_w_out, odd_norm, odd_w_qkv, odd_q_norm, odd_k_norm, odd_w_o, ffn_norm,
           ffn_w_up, ffn_conv_w, ffn_conv_b, ffn_w_down):
    depth, d, d_ff2 = ffn_w_up.shape
    x, w_up, w_down, w_qkv, w_o = _mixer(
        x, even_norm[0][None], even_w_in[0].astype(BF16), even_conv_w[0],
        even_pool_w[0].astype(BF16), even_pool_scale[0][None], even_w_out[0].astype(BF16),
        to_cast=[ffn_w_up.reshape(depth * d, d_ff2), ffn_w_down.reshape(depth * D_FF, d),
                 odd_w_qkv[0], odd_w_o[0]])
    ffn_params = (ffn_norm[:, None], w_up.reshape(depth, d, d_ff2), ffn_conv_w,
                  ffn_conv_b[:, None], w_down.reshape(depth, D_FF, d))

    def ffn(t, layer):
        return _ffn(t, layer, *ffn_params)

    x = ffn(x, 0)

    gq_pair = jnp.tile(odd_q_norm[0], 2)[None]
    gk_pair = jnp.tile(odd_k_norm[0], 2)[None]
    qkv4, qkv16 = _qkv(x, odd_norm[0][None], w_qkv, gq_pair, gk_pair)
    bias = _bias_tables(rel_bias, splits=(4, 1, 1))
    branches = [
        _attn_branch(*qkv4, bias, branch=0, split=4),
        _attn_branch(*qkv4, bias, branch=1),
        _attn_branch(*qkv16, bias, branch=2, interleave=4),
    ]
    x = _merge_out_proj(x, branches, _head_expand_matrix(), w_o)
    x = ffn(x, 1)
    return x
```

```python
import functools
import math

import jax
import jax.numpy as jnp
import numpy as np
from jax import lax
from jax.experimental import pallas as pl
from jax.experimental.pallas import tpu as pltpu

D_MODEL = 1024
CONV_WIDTH = 3
A_WIDTH = 512
B_WIDTH = 512
POOL_WINDOWS = (2, 4, 8, 16)
POOL_GROUP = 128
EVEN_IN = 3 * A_WIDTH + B_WIDTH
HEAD_DIM = 64
N_HEADS = 16
DILATED_PAIRS = ((128, 1), (512, 4), (2048, 16))
DILATIONS = tuple(dil for _, dil in DILATED_PAIRS)
N_REL_BUCKETS = 32
REL_MAX_DISTANCE = 2048
D_FF = 2816
EPS = 1e-6
MASK_VALUE = -1e30

BF16 = jnp.bfloat16
F32 = jnp.float32

LANES = 128
ATT_BLOCK = 128
ATT_UNITS_PER_STEP = 16
LOG2E = math.log2(math.e)
QKV_ROW_TILE = 1024
QKV_STAGE_SLOTS = 8
ROW_TILE = 1024
POOL_HALO = 16
CONV_HALO = 8
FF_CHUNK = 256
QKV_CHUNK = 256
MIX_CHUNK = 256
MERGE_CHUNK = 256
SHIFT_BASE = 8
V7X_VMEM_BYTES = 64 * 1024 * 1024
VMEM_LIMIT = V7X_VMEM_BYTES - 8 * 1024 * 1024


def _const_spec(shape):
    nd = len(shape)
    return pl.BlockSpec(shape, lambda *_: (0,) * nd, pipeline_mode=pl.Buffered(1))


def _layer_spec(shape, layer):
    nd = len(shape)
    return pl.BlockSpec((None,) + tuple(shape), lambda *_: (layer,) + (0,) * nd,
                        pipeline_mode=pl.Buffered(1))


def _params(n_axes):
    return pltpu.CompilerParams(
        dimension_semantics=("arbitrary",) * n_axes, vmem_limit_bytes=VMEM_LIMIT)


def _rmsnorm(x, g):
    ms = jnp.mean(x * x, axis=-1, keepdims=True)
    return x * lax.rsqrt(ms + EPS) * g


def _stage_rows(scr, slot, base, tile, prev):
    n_prev = prev.shape[0]
    scr[slot, pl.ds(base - 2 * n_prev, n_prev, stride=2), :] = prev
    scr[slot, pl.ds(base, tile.shape[0], stride=2), :] = tile


def _rows_back(scr, slot, base, n_rows, k):
    return scr[slot, pl.ds(base - 2 * k, n_rows, stride=2), :]


def _mixer_kernel(x_ref, g_ref, win_ref, cw_ref, pw_ref, ps_ref, wout_ref, *refs):
    n_cast = (len(refs) - 5) // 2
    cast_src, o_ref, cast_dst = refs[:n_cast], refs[n_cast], refs[n_cast + 1:2 * n_cast + 1]
    y_buf, carry_a, carry_p, scr = refs[2 * n_cast + 1:]
    for src, dst in zip(cast_src, cast_dst):
        dst[...] = src[...].astype(BF16)
    ts = x_ref.shape[1]
    s = pl.program_id(1)
    base = 2 * POOL_HALO

    @pl.when(s == 0)
    def _():
        carry_a[...] = jnp.zeros(carry_a.shape, F32)
        carry_p[...] = jnp.zeros(carry_p.shape, F32)

    x = x_ref[0]
    xn = _rmsnorm(x, g_ref[...]).astype(BF16)
    slabs = MIX_CHUNK // LANES

    def proj(col0):
        return jnp.dot(xn, win_ref[:, col0:col0 + MIX_CHUNK], preferred_element_type=F32)

    for c in range(A_WIDTH // MIX_CHUNK):
        h = proj(c * MIX_CHUNK)
        gate_b = proj(A_WIDTH + c * MIX_CHUNK)
        gate_c = proj(2 * A_WIDTH + c * MIX_CHUNK)
        ch = gate_c * h
        for sl in range(slabs):
            slot = c * slabs + sl
            cols = slice(slot * LANES, (slot + 1) * LANES)
            ch_sl = ch[:, sl * LANES:(sl + 1) * LANES]
            _stage_rows(scr, slot, base, ch_sl, carry_a[CONV_HALO - 2:, cols])
            carry_a[:, cols] = ch_sl[ts - CONV_HALO:, :]
            cw = cw_ref[:, cols]
            conv = (cw[0:1] * _rows_back(scr, slot, base, ts, 2)
                    + cw[1:2] * _rows_back(scr, slot, base, ts, 1) + cw[2:3] * ch_sl)
            y_buf[:, cols] = (gate_b[:, sl * LANES:(sl + 1) * LANES] * conv).astype(BF16)

    pos = s * ts + lax.broadcasted_iota(jnp.int32, (ts, 1), 0)
    for c in range(B_WIDTH // MIX_CHUNK):
        pin = proj(3 * A_WIDTH + c * MIX_CHUNK)
        for sl in range(slabs):
            g = c * slabs + sl
            k = POOL_WINDOWS[g]
            slot = A_WIDTH // LANES + g
            cols = slice(g * POOL_GROUP, (g + 1) * POOL_GROUP)
            cur = pin[:, sl * LANES:(sl + 1) * LANES]
            _stage_rows(scr, slot, base, cur, carry_p[:, cols])
            carry_p[:, cols] = cur[ts - POOL_HALO:, :]
            acc = cur
            for j in range(1, k):
                acc = acc + _rows_back(scr, slot, base, ts, j)
            cnt = jnp.minimum(pos + 1, k).astype(F32)
            pooled = acc / cnt - cur
            yb = jnp.dot(pooled.astype(BF16), pw_ref[g], preferred_element_type=F32)
            y_buf[:, A_WIDTH + g * POOL_GROUP:A_WIDTH + (g + 1) * POOL_GROUP] = (
                yb * ps_ref[:, cols]).astype(BF16)

    o_ref[0] = x + jnp.dot(y_buf[...], wout_ref[...], preferred_element_type=F32)


def _mixer(x, g, w_in, conv_w, pool_w, pool_scale, w_out, to_cast):
    b, s, d = x.shape
    ts = ROW_TILE
    n_steps = b * (s // ts)
    row_spec = pl.BlockSpec((1, ts, d), lambda i, j: (i, j, 0))
    cast_specs = [pl.BlockSpec((w.shape[0] // n_steps, w.shape[1]),
                               lambda i, j: (i * (s // ts) + j, 0)) for w in to_cast]
    return pl.pallas_call(
        _mixer_kernel,
        grid=(b, s // ts),
        in_specs=[row_spec, _const_spec((1, d)), _const_spec((d, EVEN_IN)),
                  _const_spec((CONV_WIDTH, A_WIDTH)),
                  _const_spec((len(POOL_WINDOWS), POOL_GROUP, POOL_GROUP)),
                  _const_spec((1, B_WIDTH)), _const_spec((d, d))] + cast_specs,
        out_specs=[row_spec] + cast_specs,
        out_shape=[jax.ShapeDtypeStruct(x.shape, F32)]
                  + [jax.ShapeDtypeStruct(w.shape, BF16) for w in to_cast],
        scratch_shapes=[pltpu.VMEM((ts, d), BF16),
                        pltpu.VMEM((CONV_HALO, A_WIDTH), F32),
                        pltpu.VMEM((POOL_HALO, B_WIDTH), F32),
                        pltpu.VMEM(((A_WIDTH + B_WIDTH) // LANES, 2 * (POOL_HALO + ts), LANES),
                                   F32)],
        compiler_params=_params(2),
        name="mixer",
    )(x, g, w_in, conv_w, pool_w, pool_scale, w_out, *to_cast)


def _ffn_kernel(x_ref, g_ref, wup_ref, cw_ref, cb_ref, wdn_ref, o_ref, h_buf, carry, u_scr):
    ts = x_ref.shape[1]
    s = pl.program_id(1)

    @pl.when(s == 0)
    def _():
        carry[...] = jnp.zeros(carry.shape, F32)

    x = x_ref[0]
    xn = _rmsnorm(x, g_ref[...]).astype(BF16)
    slabs = FF_CHUNK // LANES
    for j in range(D_FF // FF_CHUNK):
        halves = []
        for half in range(2):
            c0 = half * D_FF + j * FF_CHUNK
            u = jnp.dot(xn, wup_ref[:, c0:c0 + FF_CHUNK], preferred_element_type=F32)
            parts = []
            for sl in range(slabs):
                cols = slice(c0 + sl * LANES, c0 + (sl + 1) * LANES)
                slot = ((j % 2) * 2 + half) * slabs + sl
                u_sl = u[:, sl * LANES:(sl + 1) * LANES]
                _stage_rows(u_scr, slot, SHIFT_BASE, u_sl, carry[CONV_HALO - 2:, cols])
                carry[:, cols] = u_sl[ts - CONV_HALO:ts, :]
                w = cw_ref[:, cols]
                parts.append(w[0:1] * _rows_back(u_scr, slot, SHIFT_BASE, ts, 2)
                             + w[1:2] * _rows_back(u_scr, slot, SHIFT_BASE, ts, 1)
                             + w[2:3] * u_sl + cb_ref[:, cols])
            halves.append(jnp.concatenate(parts, axis=-1))
        gate, up = halves
        h_buf[:, j * FF_CHUNK:(j + 1) * FF_CHUNK] = (gate * jax.nn.sigmoid(gate) * up).astype(BF16)
    o_ref[0] = x + jnp.dot(h_buf[...], wdn_ref[...], preferred_element_type=F32)


def _ffn(x, layer, g, w_up, conv_w, conv_b, w_down):
    b, s, d = x.shape
    ts = ROW_TILE
    row_spec = pl.BlockSpec((1, ts, d), lambda i, j: (i, j, 0))
    return pl.pallas_call(
        _ffn_kernel,
        grid=(b, s // ts),
        in_specs=[row_spec, _layer_spec((1, d), layer), _layer_spec((d, 2 * D_FF), layer),
                  _layer_spec((CONV_WIDTH, 2 * D_FF), layer),
                  _layer_spec((1, 2 * D_FF), layer), _layer_spec((D_FF, d), layer)],
        out_specs=row_spec,
        out_shape=jax.ShapeDtypeStruct(x.shape, F32),
        scratch_shapes=[pltpu.VMEM((ts, D_FF), BF16),
                        pltpu.VMEM((CONV_HALO, 2 * D_FF), F32),
                        pltpu.VMEM((4 * FF_CHUNK // LANES, SHIFT_BASE + 2 * ts, LANES), F32)],
        compiler_params=_params(2),
        name="ffn",
    )(x, g, w_up, conv_w, conv_b, w_down)


def _head_pair_norm(t, g_pair, scale):
    lane = lax.broadcasted_iota(jnp.int32, t.shape, 1)
    lo = lane < HEAD_DIM
    sq = t * t
    s_lo = jnp.sum(jnp.where(lo, sq, 0.0), axis=-1, keepdims=True)
    s_hi = jnp.sum(jnp.where(lo, 0.0, sq), axis=-1, keepdims=True)
    r = lax.rsqrt(jnp.where(lo, s_lo, s_hi) * (1.0 / HEAD_DIM) + EPS)
    return t * r * (g_pair * scale)


def _qkv_kernel(x_ref, g_ref, w_ref, gq_ref, gk_ref, *refs):
    assert DILATIONS == (1, 4, 16)
    out4, out16 = refs[0:3], refs[3:6]
    x_stage, stage = refs[6:]
    d = D_MODEL
    ts = x_ref.shape[1]
    tj = ts // 4
    for sl in range(d // LANES):
        x_stage[sl] = x_ref[0, :, sl * LANES:(sl + 1) * LANES]
    x = jnp.concatenate(
        [jnp.concatenate([x_stage[sl, pl.ds(r4, tj, stride=4), :] for sl in range(d // LANES)],
                         axis=-1) for r4 in range(4)], axis=0)
    xn = _rmsnorm(x, g_ref[...]).astype(BF16)
    slabs = QKV_CHUNK // LANES
    for c in range(3 * d // QKV_CHUNK):
        res = jnp.dot(xn, w_ref[:, c * QKV_CHUNK:(c + 1) * QKV_CHUNK],
                      preferred_element_type=F32)
        for sl in range(slabs):
            slot = c * slabs + sl
            which, hp = divmod(slot, d // LANES)
            cols = slice(hp * LANES, (hp + 1) * LANES)
            slab = res[:, sl * LANES:(sl + 1) * LANES]
            if which == 0:
                slab = _head_pair_norm(slab, gq_ref[...], HEAD_DIM ** -0.5 * LOG2E)
            elif which == 1:
                slab = _head_pair_norm(slab, gk_ref[...], 1.0)
            stage[slot % QKV_STAGE_SLOTS] = slab
            for r4 in range(4):
                out4[which][r4, :, cols] = slab[r4 * tj:(r4 + 1) * tj].astype(BF16)
                for c4 in range(4):
                    p16 = stage[slot % QKV_STAGE_SLOTS, pl.ds(r4 * tj + c4, tj // 4, stride=4), :]
                    out16[which][4 * r4 + c4, :, cols] = p16.astype(BF16)


def _qkv(x, g, w_qkv, gq_pair, gk_pair):
    b, s, d = x.shape
    ts = QKV_ROW_TILE
    row_spec = pl.BlockSpec((1, ts, d), lambda i, j: (i, j, 0))
    out_specs, out_shapes = [], []
    for planes in (4, 16):
        out_specs += [pl.BlockSpec((None, planes, ts // planes, d), lambda i, j: (i, 0, j, 0))] * 3
        out_shapes += [jax.ShapeDtypeStruct((b, planes, s // planes, d), BF16)] * 3
    outs = pl.pallas_call(
        _qkv_kernel,
        grid=(b, s // ts),
        in_specs=[row_spec, _const_spec((1, d)), _const_spec((d, 3 * d)),
                  _const_spec((1, LANES)), _const_spec((1, LANES))],
        out_specs=out_specs,
        out_shape=out_shapes,
        scratch_shapes=[pltpu.VMEM((d // LANES, ts, LANES), F32),
                        pltpu.VMEM((QKV_STAGE_SLOTS, ts, LANES), F32)],
        compiler_params=_params(2),
        name="qkv",
    )(x, g, w_qkv, gq_pair, gk_pair)
    return outs[0:3], outs[3:6]


def _stat_lane(h):
    return HEAD_DIM + h // 2 if h % 2 == 0 else h // 2


def _rows_cat(parts):
    return parts[0] if len(parts) == 1 else jnp.concatenate(parts, axis=0)


def _attn_kernel(q_ref, k_ref, v_ref, kh_ref, vh_ref, bias_ref, o_ref, m_ref, l_ref, *scratch,
                 split, interleave):
    nb = ATT_BLOCK
    n_planes, rows, _ = q_ref.shape
    piece = nb // split
    col = lax.broadcasted_iota(jnp.int32, (nb, 2 * nb), 1)
    before_start = jnp.logical_and(col % (2 * piece) < piece, pl.program_id(2) == 0)
    if interleave:
        o_stage, m_tiles, l_tiles = scratch

    lo = lax.broadcasted_iota(jnp.int32, (nb, LANES), 1) < HEAD_DIM
    lo_kv = lax.broadcasted_iota(jnp.int32, (2 * nb, LANES), 1) < HEAD_DIM

    for grp in range(n_planes // split):
        planes = range(grp * split, (grp + 1) * split)
        for u in range(rows // piece):
            cur = slice(u * piece, (u + 1) * piece)
            tile = grp * (rows // piece) + u
            out_rows = pl.ds(interleave * nb * u + grp, nb, stride=interleave) if interleave else None

            def block_rows(ref, cols):
                return _rows_cat([ref[p, cur, cols] for p in planes])

            def window_rows(ref, halo_ref, cols):
                parts = []
                for p in planes:
                    if u == 0:
                        parts += [halo_ref[p, :, cols], ref[p, cur, cols]]
                    else:
                        parts.append(ref[p, (u - 1) * piece:(u + 1) * piece, cols])
                return _rows_cat(parts)

            def put_stat(ref, tiles, lane_cols, value):
                if interleave:
                    tiles[tile, :, lane_cols] = value
                else:
                    for n, p in enumerate(planes):
                        ref[p, cur, lane_cols] = value[n * piece:(n + 1) * piece]

            put_stat(m_ref, m_tiles if interleave else None, slice(None), jnp.zeros((nb, LANES), F32))
            put_stat(l_ref, l_tiles if interleave else None, slice(None), jnp.ones((nb, LANES), F32))
            for hp in range(N_HEADS // 2):
                cols = slice(hp * LANES, (hp + 1) * LANES)
                q_pair = block_rows(q_ref, cols)
                k_win = window_rows(k_ref, kh_ref, cols)
                v_win = window_rows(v_ref, vh_ref, cols)
                outs = []
                for sub in range(2):
                    h = 2 * hp + sub
                    own = lo if sub == 0 else jnp.logical_not(lo)
                    own_kv = lo_kv if sub == 0 else jnp.logical_not(lo_kv)
                    q_h = jnp.where(own, q_pair, jnp.zeros_like(q_pair))
                    sc = lax.dot_general(q_h, k_win, (((1,), (1,)), ((), ())),
                                         preferred_element_type=F32)
                    sc = sc + bias_ref[h]
                    if u == 0:
                        sc = jnp.where(before_start, MASK_VALUE, sc)
                    m = jnp.max(sc, axis=-1, keepdims=True)
                    p = jnp.exp2(sc - m).astype(BF16)
                    v_h = jnp.where(own_kv, v_win, jnp.ones_like(v_win))
                    pv = jnp.dot(p, v_h, preferred_element_type=F32)
                    outs.append(pv)
                    sl = slice(_stat_lane(h), _stat_lane(h) + 1)
                    put_stat(m_ref, m_tiles if interleave else None, sl, m)
                    put_stat(l_ref, l_tiles if interleave else None, sl, pv[:, sl])
                o_pair = jnp.where(lo, outs[0], outs[1])
                if interleave:
                    o_stage[hp, out_rows, :] = o_pair
                else:
                    for n, p in enumerate(planes):
                        o_ref[p, cur, cols] = o_pair[n * piece:(n + 1) * piece].astype(BF16)
            if interleave:
                m_ref[out_rows, :] = m_tiles[tile]
                l_ref[out_rows, :] = l_tiles[tile]
    if interleave:
        for hp in range(N_HEADS // 2):
            o_ref[:, hp * LANES:(hp + 1) * LANES] = o_stage[hp].astype(BF16)


def _attn_branch(q, k, v, bias, *, branch, split=1, interleave=0):
    b, n_planes, length, d = q.shape
    piece = ATT_BLOCK // split
    units = min(ATT_UNITS_PER_STEP, length // piece)
    step_planes = interleave or split * min(n_planes // split, ATT_UNITS_PER_STEP // units)
    rows = units * piece
    main = pl.BlockSpec((None, step_planes, rows, d), lambda bi, r, i: (bi, r, i, 0))
    halo = pl.BlockSpec((None, step_planes, piece, d),
                        lambda bi, r, i: (bi, r, jnp.maximum(i * units - 1, 0), 0))
    scratch = []
    if interleave:
        assert interleave == step_planes and rows == length and split == 1
        out_planes, out_len = n_planes // interleave, interleave * length
        o_spec = pl.BlockSpec((None, None, out_len, d), lambda bi, r, i: (bi, r, 0, 0))
        stat = pl.BlockSpec((None, None, out_len, LANES), lambda bi, r, i: (bi, r, 0, 0))
        n_tiles = step_planes * units
        scratch = [pltpu.VMEM((d // LANES, out_len, LANES), F32),
                   pltpu.VMEM((n_tiles, ATT_BLOCK, LANES), F32),
                   pltpu.VMEM((n_tiles, ATT_BLOCK, LANES), F32)]
    else:
        out_planes, out_len = n_planes, length
        o_spec = main
        stat = pl.BlockSpec((None, step_planes, rows, LANES), lambda bi, r, i: (bi, r, i, 0))
    stat_shape = jax.ShapeDtypeStruct((b, out_planes, out_len, LANES), F32)
    return pl.pallas_call(
        functools.partial(_attn_kernel, split=split, interleave=interleave),
        grid=(b, n_planes // step_planes, length // rows),
        in_specs=[main, main, main, halo, halo, _layer_spec(bias.shape[1:], branch)],
        out_specs=[o_spec, stat, stat],
        out_shape=[jax.ShapeDtypeStruct((b, out_planes, out_len, d), BF16), stat_shape, stat_shape],
        scratch_shapes=scratch,
        compiler_params=_params(3),
        name=f"attn_d{DILATIONS[branch]}",
    )(q, k, v, k, v, bias)


def _merge_kernel(x_ref, *refs):
    n_br = len(DILATIONS)
    branch_refs = [refs[3 * i:3 * i + 3] for i in range(n_br)]
    e_ref, wo_ref, out_ref, y_scr, merged_buf = refs[3 * n_br:]
    planes, tj, _ = branch_refs[0][0].shape

    def plane_major(ref, cols=slice(None)):
        return jnp.concatenate([ref[p, :, cols] for p in range(planes)], axis=0)

    ms = [plane_major(br[1]) for br in branch_refs]
    ls = [plane_major(br[2]) for br in branch_refs]
    m_max = functools.reduce(jnp.maximum, ms)
    es = [jnp.exp2(m - m_max) for m in ms]
    total = functools.reduce(jnp.add, [e * l for e, l in zip(es, ls)])
    weights = []
    for e in es:
        c = e / total
        c_hi = c.astype(BF16)
        weights.append(jnp.concatenate([c_hi, (c - c_hi.astype(F32)).astype(BF16)], axis=-1))
    for chunk in range(D_MODEL // MERGE_CHUNK):
        cols = slice(chunk * MERGE_CHUNK, (chunk + 1) * MERGE_CHUNK)
        merged = None
        for w, br in zip(weights, branch_refs):
            c_wide = jnp.dot(w, e_ref[:, cols], preferred_element_type=F32)
            term = c_wide * plane_major(br[0], cols).astype(F32)
            merged = term if merged is None else merged + term
        merged_buf[:, cols] = merged.astype(BF16)
    y = jnp.dot(merged_buf[...], wo_ref[...], preferred_element_type=F32)
    for slab in range(D_MODEL // LANES):
        cols = slice(slab * LANES, (slab + 1) * LANES)
        for p in range(planes):
            y_scr[slab, pl.ds(p, tj, stride=planes), :] = y[p * tj:(p + 1) * tj, cols]
        out_ref[0, :, cols] = x_ref[0, :, cols] + y_scr[slab]


def _merge_out_proj(x, branches, expand, w_o):
    b, s, d = x.shape
    ts = ROW_TILE
    planes = branches[0][0].shape[1]
    row_spec = pl.BlockSpec((1, ts, d), lambda i, j: (i, j, 0))
    o_spec = pl.BlockSpec((None, planes, ts // planes, d), lambda i, j: (i, 0, j, 0))
    stat_spec = pl.BlockSpec((None, planes, ts // planes, LANES), lambda i, j: (i, 0, j, 0))
    args = [t for br in branches for t in br]
    return pl.pallas_call(
        _merge_kernel,
        grid=(b, s // ts),
        in_specs=[row_spec] + [o_spec, stat_spec, stat_spec] * len(branches)
                 + [_const_spec(expand.shape), _const_spec((d, d))],
        out_specs=row_spec,
        out_shape=jax.ShapeDtypeStruct(x.shape, F32),
        scratch_shapes=[pltpu.VMEM((d // LANES, ts, LANES), F32), pltpu.VMEM((ts, d), BF16)],
        compiler_params=_params(2),
        name="merge_out_proj",
    )(x, *args, expand, w_o)


def _t5_causal_bucket(dist):
    max_exact = N_REL_BUCKETS // 2
    dd = jnp.maximum(dist, 1).astype(F32)
    large = max_exact + (jnp.log(dd / max_exact) / math.log(REL_MAX_DISTANCE / max_exact)
                         * (N_REL_BUCKETS - max_exact)).astype(jnp.int32)
    large = jnp.minimum(large, N_REL_BUCKETS - 1)
    return jnp.where(dist < max_exact, dist, large)


def _bias_tables(rel_table, splits):
    n = ATT_BLOCK
    n_br = len(DILATIONS)
    n_heads = rel_table.shape[1]
    assert splits[0] in (1, 4) and all(s == 1 for s in splits[1:])
    dist = (n - jnp.arange(n + 1))[None, :] * jnp.asarray(DILATIONS)[:, None]
    by_offset = LOG2E * jnp.swapaxes(rel_table.astype(F32)[_t5_causal_bucket(dist)], 1, 2)

    def mask_fill(count):
        return jnp.full((n_heads, count), MASK_VALUE, F32)

    split = splits[0]
    rows = []
    for br in range(n_br):
        w = by_offset[br]
        if br > 0 or split == 1:
            rows.append(jnp.stack([jnp.concatenate([w, mask_fill(4 * n - (n + 1))], axis=1)] * 4, 1))
            continue
        groups = []
        for r in range(split):
            parts = []
            for rp in range(split):
                first = 0 if rp >= r else 1
                vals = w[:, split * first + rp - r::split]
                parts += [mask_fill(first), vals, mask_fill(n - first - vals.shape[1])]
            groups.append(jnp.concatenate(parts, axis=1))
        rows.append(jnp.stack(groups, axis=1))
    rows = jnp.stack(rows)
    return pl.pallas_call(
        functools.partial(_bias_kernel, split=split),
        out_shape=jax.ShapeDtypeStruct((n_br, n_heads, n, 2 * n), F32),
        name="bias_tables",
    )(rows[:, :, :, None, :])


def _bias_kernel(rows_ref, out_ref, *, split):
    n = ATT_BLOCK
    n_br, n_heads = out_ref.shape[:2]
    for br in range(n_br):
        sp = split if br == 0 else 1
        for h in range(n_heads):
            if sp == 1:
                wide = jnp.broadcast_to(rows_ref[br, h, 0][:, :3 * n], (n, 3 * n))
                tab = pltpu.roll(wide, 0, 1, stride=1, stride_axis=0)[:, :2 * n]
            else:
                groups = []
                for r in range(sp):
                    wide = jnp.broadcast_to(rows_ref[br, h, r], (n // sp, 4 * n))
                    rolled = pltpu.roll(wide, 0, 1, stride=1, stride_axis=0)
                    groups.append(jnp.concatenate(
                        [rolled[:, rp * n:rp * n + 2 * n // sp] for rp in range(sp)], axis=1))
                tab = jnp.concatenate(groups, axis=0)
            out_ref[br, h] = tab


def _head_expand_matrix():
    row = jnp.arange(2 * LANES)[:, None] % LANES
    head = jnp.arange(D_MODEL)[None, :] // HEAD_DIM
    stat_lane = jnp.where(head % 2 == 0, HEAD_DIM + head // 2, head // 2)
    return (row == stat_lane).astype(BF16)


def kernel(x, rel_bias, even_norm, even_w_in, even_conv_w, even_pool_w, even_pool_scale,
           even_w_out, odd_norm, odd_w_qkv, odd_q_norm, odd_k_norm, odd_w_o, ffn_norm,
           ffn_w_up, ffn_conv_w, ffn_conv_b, ffn_w_down):
    depth, d, d_ff2 = ffn_w_up.shape
    x, w_up, w_down, w_qkv, w_o = _mixer(
        x, even_norm[0][None], even_w_in[0].astype(BF16), even_conv_w[0],
        even_pool_w[0].astype(BF16), even_pool_scale[0][None], even_w_out[0].astype(BF16),
        to_cast=[ffn_w_up.reshape(depth * d, d_ff2), ffn_w_down.reshape(depth * D_FF, d),
                 odd_w_qkv[0], odd_w_o[0]])
    ffn_params = (ffn_norm[:, None], w_up.reshape(depth, d, d_ff2), ffn_conv_w,
                  ffn_conv_b[:, None], w_down.reshape(depth, D_FF, d))

    def ffn(t, layer):
        return _ffn(t, layer, *ffn_params)

    x = ffn(x, 0)

    gq_pair = jnp.tile(odd_q_norm[0], 2)[None]
    gk_pair = jnp.tile(odd_k_norm[0], 2)[None]
    qkv4, qkv16 = _qkv(x, odd_norm[0][None], w_qkv, gq_pair, gk_pair)
    bias = _bias_tables(rel_bias, splits=(4, 1, 1))
    branches = [
        _attn_branch(*qkv4, bias, branch=0, split=4),
        _attn_branch(*qkv4, bias, branch=1),
        _attn_branch(*qkv16, bias, branch=2, interleave=4),
    ]
    x = _merge_out_proj(x, branches, _head_expand_matrix(), w_o)
    x = ffn(x, 1)
    return x
```

```python
import functools
import math

import jax
import jax.numpy as jnp
import numpy as np
from jax import lax
from jax.experimental import pallas as pl
from jax.experimental.pallas import tpu as pltpu

D_MODEL = 1024
CONV_WIDTH = 3
A_WIDTH = 512
B_WIDTH = 512
POOL_WINDOWS = (2, 4, 8, 16)
POOL_GROUP = 128
EVEN_IN = 3 * A_WIDTH + B_WIDTH
HEAD_DIM = 64
N_HEADS = 16
DILATED_PAIRS = ((128, 1), (512, 4), (2048, 16))
DILATIONS = tuple(dil for _, dil in DILATED_PAIRS)
N_REL_BUCKETS = 32
REL_MAX_DISTANCE = 2048
D_FF = 2816
EPS = 1e-6
MASK_VALUE = -1e30

BF16 = jnp.bfloat16
F32 = jnp.float32

LANES = 128
ATT_BLOCK = 128
ATT_UNITS_PER_STEP = 16
LOG2E = math.log2(math.e)
QKV_ROW_TILE = 1024
QKV_STAGE_SLOTS = 8
ROW_TILE = 1024
POOL_HALO = 16
CONV_HALO = 8
FF_CHUNK = 256
QKV_CHUNK = 256
MIX_CHUNK = 256
MERGE_CHUNK = 256
SHIFT_BASE = 8
V7X_VMEM_BYTES = 64 * 1024 * 1024
VMEM_LIMIT = V7X_VMEM_BYTES - 8 * 1024 * 1024


def _const_spec(shape):
    nd = len(shape)
    return pl.BlockSpec(shape, lambda *_: (0,) * nd, pipeline_mode=pl.Buffered(1))


def _layer_spec(shape, layer):
    nd = len(shape)
    return pl.BlockSpec((None,) + tuple(shape), lambda *_: (layer,) + (0,) * nd,
                        pipeline_mode=pl.Buffered(1))


def _params(n_axes):
    return pltpu.CompilerParams(
        dimension_semantics=("arbitrary",) * n_axes, vmem_limit_bytes=VMEM_LIMIT)


def _rmsnorm(x, g):
    ms = jnp.mean(x * x, axis=-1, keepdims=True)
    return x * lax.rsqrt(ms + EPS) * g


def _stage_rows(scr, slot, base, tile, prev):
    n_prev = prev.shape[0]
    scr[slot, pl.ds(base - 2 * n_prev, n_prev, stride=2), :] = prev
    scr[slot, pl.ds(base, tile.shape[0], stride=2), :] = tile


def _rows_back(scr, slot, base, n_rows, k):
    return scr[slot, pl.ds(base - 2 * k, n_rows, stride=2), :]


def _mixer_kernel(x_ref, g_ref, win_ref, cw_ref, pw_ref, ps_ref, wout_ref, *refs):
    n_cast = (len(refs) - 5) // 2
    cast_src, o_ref, cast_dst = refs[:n_cast], refs[n_cast], refs[n_cast + 1:2 * n_cast + 1]
    y_buf, carry_a, carry_p, scr = refs[2 * n_cast + 1:]
    for src, dst in zip(cast_src, cast_dst):
        dst[...] = src[...].astype(BF16)
    ts = x_ref.shape[1]
    s = pl.program_id(1)
    base = 2 * POOL_HALO

    @pl.when(s == 0)
    def _():
        carry_a[...] = jnp.zeros(carry_a.shape, F32)
        carry_p[...] = jnp.zeros(carry_p.shape, F32)

    x = x_ref[0]
    xn = _rmsnorm(x, g_ref[...]).astype(BF16)
    slabs = MIX_CHUNK // LANES

    def proj(col0):
        return jnp.dot(xn, win_ref[:, col0:col0 + MIX_CHUNK], preferred_element_type=F32)

    for c in range(A_WIDTH // MIX_CHUNK):
        h = proj(c * MIX_CHUNK)
        gate_b = proj(A_WIDTH + c * MIX_CHUNK)
        gate_c = proj(2 * A_WIDTH + c * MIX_CHUNK)
        ch = gate_c * h
        for sl in range(slabs):
            slot = c * slabs + sl
            cols = slice(slot * LANES, (slot + 1) * LANES)
            ch_sl = ch[:, sl * LANES:(sl + 1) * LANES]
            _stage_rows(scr, slot, base, ch_sl, carry_a[CONV_HALO - 2:, cols])
            carry_a[:, cols] = ch_sl[ts - CONV_HALO:, :]
            cw = cw_ref[:, cols]
            conv = (cw[0:1] * _rows_back(scr, slot, base, ts, 2)
                    + cw[1:2] * _rows_back(scr, slot, base, ts, 1) + cw[2:3] * ch_sl)
            y_buf[:, cols] = (gate_b[:, sl * LANES:(sl + 1) * LANES] * conv).astype(BF16)

    pos = s * ts + lax.broadcasted_iota(jnp.int32, (ts, 1), 0)
    for c in range(B_WIDTH // MIX_CHUNK):
        pin = proj(3 * A_WIDTH + c * MIX_CHUNK)
        for sl in range(slabs):
            g = c * slabs + sl
            k = POOL_WINDOWS[g]
            slot = A_WIDTH // LANES + g
            cols = slice(g * POOL_GROUP, (g + 1) * POOL_GROUP)
            cur = pin[:, sl * LANES:(sl + 1) * LANES]
            _stage_rows(scr, slot, base, cur, carry_p[:, cols])
            carry_p[:, cols] = cur[ts - POOL_HALO:, :]
            acc = cur
            for j in range(1, k):
                acc = acc + _rows_back(scr, slot, base, ts, j)
            cnt = jnp.minimum(pos + 1, k).astype(F32)
            pooled = acc / cnt - cur
            yb = jnp.dot(pooled.astype(BF16), pw_ref[g], preferred_element_type=F32)
            y_buf[:, A_WIDTH + g * POOL_GROUP:A_WIDTH + (g + 1) * POOL_GROUP] = (
                yb * ps_ref[:, cols]).astype(BF16)

    o_ref[0] = x + jnp.dot(y_buf[...], wout_ref[...], preferred_element_type=F32)


def _mixer(x, g, w_in, conv_w, pool_w, pool_scale, w_out, to_cast):
    b, s, d = x.shape
    ts = ROW_TILE
    n_steps = b * (s // ts)
    row_spec = pl.BlockSpec((1, ts, d), lambda i, j: (i, j, 0))
    cast_specs = [pl.BlockSpec((w.shape[0] // n_steps, w.shape[1]),
                               lambda i, j: (i * (s // ts) + j, 0)) for w in to_cast]
    return pl.pallas_call(
        _mixer_kernel,
        grid=(b, s // ts),
        in_specs=[row_spec, _const_spec((1, d)), _const_spec((d, EVEN_IN)),
                  _const_spec((CONV_WIDTH, A_WIDTH)),
                  _const_spec((len(POOL_WINDOWS), POOL_GROUP, POOL_GROUP)),
                  _const_spec((1, B_WIDTH)), _const_spec((d, d))] + cast_specs,
        out_specs=[row_spec] + cast_specs,
        out_shape=[jax.ShapeDtypeStruct(x.shape, F32)]
                  + [jax.ShapeDtypeStruct(w.shape, BF16) for w in to_cast],
        scratch_shapes=[pltpu.VMEM((ts, d), BF16),
                        pltpu.VMEM((CONV_HALO, A_WIDTH), F32),
                        pltpu.VMEM((POOL_HALO, B_WIDTH), F32),
                        pltpu.VMEM(((A_WIDTH + B_WIDTH) // LANES, 2 * (POOL_HALO + ts), LANES),
                                   F32)],
        compiler_params=_params(2),
        name="mixer",
    )(x, g, w_in, conv_w, pool_w, pool_scale, w_out, *to_cast)


def _ffn_kernel(x_ref, g_ref, wup_ref, cw_ref, cb_ref, wdn_ref, o_ref, h_buf, carry, u_scr):
    ts = x_ref.shape[1]
    s = pl.program_id(1)

    @pl.when(s == 0)
    def _():
        carry[...] = jnp.zeros(carry.shape, F32)

    x = x_ref[0]
    xn = _rmsnorm(x, g_ref[...]).astype(BF16)
    slabs = FF_CHUNK // LANES
    for j in range(D_FF // FF_CHUNK):
        halves = []
        for half in range(2):
            c0 = half * D_FF + j * FF_CHUNK
            u = jnp.dot(xn, wup_ref[:, c0:c0 + FF_CHUNK], preferred_element_type=F32)
            parts = []
            for sl in range(slabs):
                cols = slice(c0 + sl * LANES, c0 + (sl + 1) * LANES)
                slot = ((j % 2) * 2 + half) * slabs + sl
                u_sl = u[:, sl * LANES:(sl + 1) * LANES]
                _stage_rows(u_scr, slot, SHIFT_BASE, u_sl, carry[CONV_HALO - 2:, cols])
                carry[:, cols] = u_sl[ts - CONV_HALO:ts, :]
                w = cw_ref[:, cols]
                parts.append(w[0:1] * _rows_back(u_scr, slot, SHIFT_BASE, ts, 2)
                             + w[1:2] * _rows_back(u_scr, slot, SHIFT_BASE, ts, 1)
                             + w[2:3] * u_sl + cb_ref[:, cols])
            halves.append(jnp.concatenate(parts, axis=-1))
        gate, up = halves
        h_buf[:, j * FF_CHUNK:(j + 1) * FF_CHUNK] = (gate * jax.nn.sigmoid(gate) * up).astype(BF16)
    o_ref[0] = x + jnp.dot(h_buf[...], wdn_ref[...], preferred_element_type=F32)


def _ffn(x, layer, g, w_up, conv_w, conv_b, w_down):
    b, s, d = x.shape
    ts = ROW_TILE
    row_spec = pl.BlockSpec((1, ts, d), lambda i, j: (i, j, 0))
    return pl.pallas_call(
        _ffn_kernel,
        grid=(b, s // ts),
        in_specs=[row_spec, _layer_spec((1, d), layer), _layer_spec((d, 2 * D_FF), layer),
                  _layer_spec((CONV_WIDTH, 2 * D_FF), layer),
                  _layer_spec((1, 2 * D_FF), layer), _layer_spec((D_FF, d), layer)],
        out_specs=row_spec,
        out_shape=jax.ShapeDtypeStruct(x.shape, F32),
        scratch_shapes=[pltpu.VMEM((ts, D_FF), BF16),
                        pltpu.VMEM((CONV_HALO, 2 * D_FF), F32),
                        pltpu.VMEM((4 * FF_CHUNK // LANES, SHIFT_BASE + 2 * ts, LANES), F32)],
        compiler_params=_params(2),
        name="ffn",
    )(x, g, w_up, conv_w, conv_b, w_down)


def _head_pair_norm(t, g_pair, scale):
    lane = lax.broadcasted_iota(jnp.int32, t.shape, 1)
    lo = lane < HEAD_DIM
    sq = t * t
    s_lo = jnp.sum(jnp.where(lo, sq, 0.0), axis=-1, keepdims=True)
    s_hi = jnp.sum(jnp.where(lo, 0.0, sq), axis=-1, keepdims=True)
    r = lax.rsqrt(jnp.where(lo, s_lo, s_hi) * (1.0 / HEAD_DIM) + EPS)
    return t * r * (g_pair * scale)


def _qkv_kernel(x_ref, g_ref, w_ref, gq_ref, gk_ref, rows_ref, *refs, splits):
    assert DILATIONS == (1, 4, 16)
    out4, out16, bias_ref = refs[0:3], refs[3:6], refs[6]
    x_stage, stage = refs[7:]
    _write_bias_tables(rows_ref, bias_ref, splits)
    d = D_MODEL
    ts = x_ref.shape[1]
    tj = ts // 4
    for sl in range(d // LANES):
        x_stage[sl] = x_ref[0, :, sl * LANES:(sl + 1) * LANES]
    x = jnp.concatenate(
        [jnp.concatenate([x_stage[sl, pl.ds(r4, tj, stride=4), :] for sl in range(d // LANES)],
                         axis=-1) for r4 in range(4)], axis=0)
    xn = _rmsnorm(x, g_ref[...]).astype(BF16)
    slabs = QKV_CHUNK // LANES
    for c in range(3 * d // QKV_CHUNK):
        res = jnp.dot(xn, w_ref[:, c * QKV_CHUNK:(c + 1) * QKV_CHUNK],
                      preferred_element_type=F32)
        for sl in range(slabs):
            slot = c * slabs + sl
            which, hp = divmod(slot, d // LANES)
            cols = slice(hp * LANES, (hp + 1) * LANES)
            slab = res[:, sl * LANES:(sl + 1) * LANES]
            if which == 0:
                slab = _head_pair_norm(slab, gq_ref[...], HEAD_DIM ** -0.5 * LOG2E)
            elif which == 1:
                slab = _head_pair_norm(slab, gk_ref[...], 1.0)
            stage[slot % QKV_STAGE_SLOTS] = slab
            for r4 in range(4):
                out4[which][r4, :, cols] = slab[r4 * tj:(r4 + 1) * tj].astype(BF16)
                for c4 in range(4):
                    p16 = stage[slot % QKV_STAGE_SLOTS, pl.ds(r4 * tj + c4, tj // 4, stride=4), :]
                    out16[which][4 * r4 + c4, :, cols] = p16.astype(BF16)


def _qkv(x, g, w_qkv, gq_pair, gk_pair, bias_rows, splits):
    b, s, d = x.shape
    ts = QKV_ROW_TILE
    nt = s // ts
    n_br, n_heads = bias_rows.shape[:2]
    assert b * nt == n_heads
    row_spec = pl.BlockSpec((1, ts, d), lambda i, j: (i, j, 0))
    out_specs, out_shapes = [], []
    for planes in (4, 16):
        out_specs += [pl.BlockSpec((None, planes, ts // planes, d), lambda i, j: (i, 0, j, 0))] * 3
        out_shapes += [jax.ShapeDtypeStruct((b, planes, s // planes, d), BF16)] * 3
    out_specs.append(pl.BlockSpec((n_br, 1, ATT_BLOCK, 2 * ATT_BLOCK),
                                  lambda i, j: (0, i * nt + j, 0, 0)))
    out_shapes.append(jax.ShapeDtypeStruct((n_br, n_heads, ATT_BLOCK, 2 * ATT_BLOCK), F32))
    outs = pl.pallas_call(
        functools.partial(_qkv_kernel, splits=splits),
        grid=(b, nt),
        in_specs=[row_spec, _const_spec((1, d)), _const_spec((d, 3 * d)),
                  _const_spec((1, LANES)), _const_spec((1, LANES)),
                  pl.BlockSpec((n_br, 1) + bias_rows.shape[2:],
                               lambda i, j: (0, i * nt + j, 0, 0, 0))],
        out_specs=out_specs,
        out_shape=out_shapes,
        scratch_shapes=[pltpu.VMEM((d // LANES, ts, LANES), F32),
                        pltpu.VMEM((QKV_STAGE_SLOTS, ts, LANES), F32)],
        compiler_params=_params(2),
        name="qkv",
    )(x, g, w_qkv, gq_pair, gk_pair, bias_rows)
    return outs[0:3], outs[3:6], outs[6]


def _stat_lane(h):
    return HEAD_DIM + h // 2 if h % 2 == 0 else h // 2


def _rows_cat(parts):
    return parts[0] if len(parts) == 1 else jnp.concatenate(parts, axis=0)


def _attn_kernel(q_ref, k_ref, v_ref, kh_ref, vh_ref, bias_ref, o_ref, m_ref, l_ref, *scratch,
                 split, interleave):
    nb = ATT_BLOCK
    n_planes, rows, _ = q_ref.shape
    piece = nb // split
    col = lax.broadcasted_iota(jnp.int32, (nb, 2 * nb), 1)
    before_start = jnp.logical_and(col % (2 * piece) < piece, pl.program_id(2) == 0)
    if interleave:
        o_stage, m_tiles, l_tiles = scratch

    lo = lax.broadcasted_iota(jnp.int32, (nb, LANES), 1) < HEAD_DIM
    lo_kv = lax.broadcasted_iota(jnp.int32, (2 * nb, LANES), 1) < HEAD_DIM

    for grp in range(n_planes // split):
        planes = range(grp * split, (grp + 1) * split)
        for u in range(rows // piece):
            cur = slice(u * piece, (u + 1) * piece)
            tile = grp * (rows // piece) + u
            out_rows = pl.ds(interleave * nb * u + grp, nb, stride=interleave) if interleave else None

            def block_rows(ref, cols):
                return _rows_cat([ref[p, cur, cols] for p in planes])

            def window_rows(ref, halo_ref, cols):
                parts = []
                for p in planes:
                    if u == 0:
                        parts += [halo_ref[p, :, cols], ref[p, cur, cols]]
                    else:
                        parts.append(ref[p, (u - 1) * piece:(u + 1) * piece, cols])
                return _rows_cat(parts)

            def put_stat(ref, tiles, lane_cols, value):
                if interleave:
                    tiles[tile, :, lane_cols] = value
                else:
                    for n, p in enumerate(planes):
                        ref[p, cur, lane_cols] = value[n * piece:(n + 1) * piece]

            put_stat(m_ref, m_tiles if interleave else None, slice(None), jnp.zeros((nb, LANES), F32))
            put_stat(l_ref, l_tiles if interleave else None, slice(None), jnp.ones((nb, LANES), F32))
            for hp in range(N_HEADS // 2):
                cols = slice(hp * LANES, (hp + 1) * LANES)
                q_pair = block_rows(q_ref, cols)
                k_win = window_rows(k_ref, kh_ref, cols)
                v_win = window_rows(v_ref, vh_ref, cols)
                outs = []
                for sub in range(2):
                    h = 2 * hp + sub
                    own = lo if sub == 0 else jnp.logical_not(lo)
                    own_kv = lo_kv if sub == 0 else jnp.logical_not(lo_kv)
                    q_h = jnp.where(own, q_pair, jnp.zeros_like(q_pair))
                    sc = lax.dot_general(q_h, k_win, (((1,), (1,)), ((), ())),
                                         preferred_element_type=F32)
                    sc = sc + bias_ref[h]
                    if u == 0:
                        sc = jnp.where(before_start, MASK_VALUE, sc)
                    m = jnp.max(sc, axis=-1, keepdims=True)
                    p = jnp.exp2(sc - m).astype(BF16)
                    v_h = jnp.where(own_kv, v_win, jnp.ones_like(v_win))
                    pv = jnp.dot(p, v_h, preferred_element_type=F32)
                    outs.append(pv)
                    sl = slice(_stat_lane(h), _stat_lane(h) + 1)
                    put_stat(m_ref, m_tiles if interleave else None, sl, m)
                    put_stat(l_ref, l_tiles if interleave else None, sl, pv[:, sl])
                o_pair = jnp.where(lo, outs[0], outs[1])
                if interleave:
                    o_stage[hp, out_rows, :] = o_pair
                else:
                    for n, p in enumerate(planes):
                        o_ref[p, cur, cols] = o_pair[n * piece:(n + 1) * piece].astype(BF16)
            if interleave:
                m_ref[out_rows, :] = m_tiles[tile]
                l_ref[out_rows, :] = l_tiles[tile]
    if interleave:
        for hp in range(N_HEADS // 2):
            o_ref[:, hp * LANES:(hp + 1) * LANES] = o_stage[hp].astype(BF16)


def _attn_branch(q, k, v, bias, *, branch, split=1, interleave=0):
    b, n_planes, length, d = q.shape
    piece = ATT_BLOCK // split
    units = min(ATT_UNITS_PER_STEP, length // piece)
    step_planes = interleave or split * min(n_planes // split, ATT_UNITS_PER_STEP // units)
    rows = units * piece
    main = pl.BlockSpec((None, step_planes, rows, d), lambda bi, r, i: (bi, r, i, 0))
    halo = pl.BlockSpec((None, step_planes, piece, d),
                        lambda bi, r, i: (bi, r, jnp.maximum(i * units - 1, 0), 0))
    scratch = []
    if interleave:
        assert interleave == step_planes and rows == length and split == 1
        out_planes, out_len = n_planes // interleave, interleave * length
        o_spec = pl.BlockSpec((None, None, out_len, d), lambda bi, r, i: (bi, r, 0, 0))
        stat = pl.BlockSpec((None, None, out_len, LANES), lambda bi, r, i: (bi, r, 0, 0))
        n_tiles = step_planes * units
        scratch = [pltpu.VMEM((d // LANES, out_len, LANES), F32),
                   pltpu.VMEM((n_tiles, ATT_BLOCK, LANES), F32),
                   pltpu.VMEM((n_tiles, ATT_BLOCK, LANES), F32)]
    else:
        out_planes, out_len = n_planes, length
        o_spec = main
        stat = pl.BlockSpec((None, step_planes, rows, LANES), lambda bi, r, i: (bi, r, i, 0))
    stat_shape = jax.ShapeDtypeStruct((b, out_planes, out_len, LANES), F32)
    return pl.pallas_call(
        functools.partial(_attn_kernel, split=split, interleave=interleave),
        grid=(b, n_planes // step_planes, length // rows),
        in_specs=[main, main, main, halo, halo, _layer_spec(bias.shape[1:], branch)],
        out_specs=[o_spec, stat, stat],
        out_shape=[jax.ShapeDtypeStruct((b, out_planes, out_len, d), BF16), stat_shape, stat_shape],
        scratch_shapes=scratch,
        compiler_params=_params(3),
        name=f"attn_d{DILATIONS[branch]}",
    )(q, k, v, k, v, bias)


def _merge_kernel(x_ref, *refs):
    n_br = len(DILATIONS)
    branch_refs = [refs[3 * i:3 * i + 3] for i in range(n_br)]
    e_ref, wo_ref, out_ref, y_scr, merged_buf = refs[3 * n_br:]
    planes, tj, _ = branch_refs[0][0].shape

    def plane_major(ref, cols=slice(None)):
        return jnp.concatenate([ref[p, :, cols] for p in range(planes)], axis=0)

    ms = [plane_major(br[1]) for br in branch_refs]
    ls = [plane_major(br[2]) for br in branch_refs]
    m_max = functools.reduce(jnp.maximum, ms)
    es = [jnp.exp2(m - m_max) for m in ms]
    total = functools.reduce(jnp.add, [e * l for e, l in zip(es, ls)])
    weights = []
    for e in es:
        c = e / total
        c_hi = c.astype(BF16)
        weights.append(jnp.concatenate([c_hi, (c - c_hi.astype(F32)).astype(BF16)], axis=-1))
    for chunk in range(D_MODEL // MERGE_CHUNK):
        cols = slice(chunk * MERGE_CHUNK, (chunk + 1) * MERGE_CHUNK)
        merged = None
        for w, br in zip(weights, branch_refs):
            c_wide = jnp.dot(w, e_ref[:, cols], preferred_element_type=F32)
            term = c_wide * plane_major(br[0], cols).astype(F32)
            merged = term if merged is None else merged + term
        merged_buf[:, cols] = merged.astype(BF16)
    y = jnp.dot(merged_buf[...], wo_ref[...], preferred_element_type=F32)
    for slab in range(D_MODEL // LANES):
        cols = slice(slab * LANES, (slab + 1) * LANES)
        for p in range(planes):
            y_scr[slab, pl.ds(p, tj, stride=planes), :] = y[p * tj:(p + 1) * tj, cols]
        out_ref[0, :, cols] = x_ref[0, :, cols] + y_scr[slab]


def _merge_out_proj(x, branches, expand, w_o):
    b, s, d = x.shape
    ts = ROW_TILE
    planes = branches[0][0].shape[1]
    row_spec = pl.BlockSpec((1, ts, d), lambda i, j: (i, j, 0))
    o_spec = pl.BlockSpec((None, planes, ts // planes, d), lambda i, j: (i, 0, j, 0))
    stat_spec = pl.BlockSpec((None, planes, ts // planes, LANES), lambda i, j: (i, 0, j, 0))
    args = [t for br in branches for t in br]
    return pl.pallas_call(
        _merge_kernel,
        grid=(b, s // ts),
        in_specs=[row_spec] + [o_spec, stat_spec, stat_spec] * len(branches)
                 + [_const_spec(expand.shape), _const_spec((d, d))],
        out_specs=row_spec,
        out_shape=jax.ShapeDtypeStruct(x.shape, F32),
        scratch_shapes=[pltpu.VMEM((d // LANES, ts, LANES), F32), pltpu.VMEM((ts, d), BF16)],
        compiler_params=_params(2),
        name="merge_out_proj",
    )(x, *args, expand, w_o)


def _t5_causal_bucket(dist):
    max_exact = N_REL_BUCKETS // 2
    dd = jnp.maximum(dist, 1).astype(F32)
    large = max_exact + (jnp.log(dd / max_exact) / math.log(REL_MAX_DISTANCE / max_exact)
                         * (N_REL_BUCKETS - max_exact)).astype(jnp.int32)
    large = jnp.minimum(large, N_REL_BUCKETS - 1)
    return jnp.where(dist < max_exact, dist, large)


def _bias_rows(rel_table, splits):
    n = ATT_BLOCK
    n_br = len(DILATIONS)
    n_heads = rel_table.shape[1]
    assert splits[0] in (1, 4) and all(s == 1 for s in splits[1:])
    dist = (n - jnp.arange(n + 1))[None, :] * jnp.asarray(DILATIONS)[:, None]
    by_offset = LOG2E * jnp.swapaxes(rel_table.astype(F32)[_t5_causal_bucket(dist)], 1, 2)

    def mask_fill(count):
        return jnp.full((n_heads, count), MASK_VALUE, F32)

    split = splits[0]
    rows = []
    for br in range(n_br):
        w = by_offset[br]
        if br > 0 or split == 1:
            rows.append(jnp.stack([jnp.concatenate([w, mask_fill(4 * n - (n + 1))], axis=1)] * 4, 1))
            continue
        groups = []
        for r in range(split):
            parts = []
            for rp in range(split):
                first = 0 if rp >= r else 1
                vals = w[:, split * first + rp - r::split]
                parts += [mask_fill(first), vals, mask_fill(n - first - vals.shape[1])]
            groups.append(jnp.concatenate(parts, axis=1))
        rows.append(jnp.stack(groups, axis=1))
    return jnp.stack(rows)[:, :, :, None, :]


def _write_bias_tables(rows_ref, out_ref, splits):
    n = ATT_BLOCK
    for br, sp in enumerate(splits):
        if sp == 1:
            wide = jnp.broadcast_to(rows_ref[br, 0, 0][:, :3 * n], (n, 3 * n))
            tab = pltpu.roll(wide, 0, 1, stride=1, stride_axis=0)[:, :2 * n]
        else:
            groups = []
            for r in range(sp):
                wide = jnp.broadcast_to(rows_ref[br, 0, r], (n // sp, 4 * n))
                rolled = pltpu.roll(wide, 0, 1, stride=1, stride_axis=0)
                groups.append(jnp.concatenate(
                    [rolled[:, rp * n:rp * n + 2 * n // sp] for rp in range(sp)], axis=1))
            tab = jnp.concatenate(groups, axis=0)
        out_ref[br, 0] = tab


def _head_expand_matrix():
    row = jnp.arange(2 * LANES)[:, None] % LANES
    head = jnp.arange(D_MODEL)[None, :] // HEAD_DIM
    stat_lane = jnp.where(head % 2 == 0, HEAD_DIM + head // 2, head // 2)
    return (row == stat_lane).astype(BF16)


def kernel(x, rel_bias, even_norm, even_w_in, even_conv_w, even_pool_w, even_pool_scale,
           even_w_out, odd_norm, odd_w_qkv, odd_q_norm, odd_k_norm, odd_w_o, ffn_norm,
           ffn_w_up, ffn_conv_w, ffn_conv_b, ffn_w_down):
    depth, d, d_ff2 = ffn_w_up.shape
    x, w_up, w_down, w_qkv, w_o = _mixer(
        x, even_norm[0][None], even_w_in[0].astype(BF16), even_conv_w[0],
        even_pool_w[0].astype(BF16), even_pool_scale[0][None], even_w_out[0].astype(BF16),
        to_cast=[ffn_w_up.reshape(depth * d, d_ff2), ffn_w_down.reshape(depth * D_FF, d),
                 odd_w_qkv[0], odd_w_o[0]])
    ffn_params = (ffn_norm[:, None], w_up.reshape(depth, d, d_ff2), ffn_conv_w,
                  ffn_conv_b[:, None], w_down.reshape(depth, D_FF, d))

    def ffn(t, layer):
        return _ffn(t, layer, *ffn_params)

    x = ffn(x, 0)

    gq_pair = jnp.tile(odd_q_norm[0], 2)[None]
    gk_pair = jnp.tile(odd_k_norm[0], 2)[None]
    splits = (4, 1, 1)
    qkv4, qkv16, bias = _qkv(x, odd_norm[0][None], w_qkv, gq_pair, gk_pair,
                             _bias_rows(rel_bias, splits), splits)
    branches = [
        _attn_branch(*qkv4, bias, branch=0, split=splits[0]),
        _attn_branch(*qkv4, bias, branch=1),
        _attn_branch(*qkv16, bias, branch=2, interleave=4),
    ]
    x = _merge_out_proj(x, branches, _head_expand_matrix(), w_o)
    x = ffn(x, 1)
    return x
```

```python
import functools
import math

import jax
import jax.numpy as jnp
import numpy as np
from jax import lax
from jax.experimental import pallas as pl
from jax.experimental.pallas import tpu as pltpu

D_MODEL = 1024
CONV_WIDTH = 3
A_WIDTH = 512
B_WIDTH = 512
POOL_WINDOWS = (2, 4, 8, 16)
POOL_GROUP = 128
EVEN_IN = 3 * A_WIDTH + B_WIDTH
HEAD_DIM = 64
N_HEADS = 16
DILATED_PAIRS = ((128, 1), (512, 4), (2048, 16))
DILATIONS = tuple(dil for _, dil in DILATED_PAIRS)
N_REL_BUCKETS = 32
REL_MAX_DISTANCE = 2048
D_FF = 2816
EPS = 1e-6
MASK_VALUE = -1e30

BF16 = jnp.bfloat16
F32 = jnp.float32

LANES = 128
ATT_BLOCK = 128
ATT_UNITS_PER_STEP = 16
LOG2E = math.log2(math.e)
QKV_ROW_TILE = 1024
QKV_STAGE_SLOTS = 8
ROW_TILE = 1024
POOL_HALO = 16
CONV_HALO = 8
FF_CHUNK = 256
QKV_CHUNK = 256
MIX_CHUNK = 512
MERGE_CHUNK = 256
SHIFT_BASE = 8
V7X_VMEM_BYTES = 64 * 1024 * 1024
VMEM_LIMIT = V7X_VMEM_BYTES - 8 * 1024 * 1024


def _const_spec(shape):
    nd = len(shape)
    return pl.BlockSpec(shape, lambda *_: (0,) * nd, pipeline_mode=pl.Buffered(1))


def _layer_spec(shape, layer):
    nd = len(shape)
    return pl.BlockSpec((None,) + tuple(shape), lambda *_: (layer,) + (0,) * nd,
                        pipeline_mode=pl.Buffered(1))


def _params(n_axes):
    return pltpu.CompilerParams(
        dimension_semantics=("arbitrary",) * n_axes, vmem_limit_bytes=VMEM_LIMIT)


def _rmsnorm(x, g):
    ms = jnp.mean(x * x, axis=-1, keepdims=True)
    return x * lax.rsqrt(ms + EPS) * g


def _stage_rows(scr, slot, base, tile, prev):
    n_prev = prev.shape[0]
    scr[slot, pl.ds(base - 2 * n_prev, n_prev, stride=2), :] = prev
    scr[slot, pl.ds(base, tile.shape[0], stride=2), :] = tile


def _rows_back(scr, slot, base, n_rows, k):
    return scr[slot, pl.ds(base - 2 * k, n_rows, stride=2), :]


def _mixer_kernel(x_ref, g_ref, win_ref, cw_ref, pw_ref, ps_ref, wout_ref, *refs):
    n_cast = (len(refs) - 5) // 2
    cast_src, o_ref, cast_dst = refs[:n_cast], refs[n_cast], refs[n_cast + 1:2 * n_cast + 1]
    y_buf, carry_a, carry_p, scr = refs[2 * n_cast + 1:]
    for src, dst in zip(cast_src, cast_dst):
        dst[...] = src[...].astype(BF16)
    ts = x_ref.shape[1]
    s = pl.program_id(1)
    base = 2 * POOL_HALO

    @pl.when(s == 0)
    def _():
        carry_a[...] = jnp.zeros(carry_a.shape, F32)
        carry_p[...] = jnp.zeros(carry_p.shape, F32)

    x = x_ref[0]
    xn = _rmsnorm(x, g_ref[...]).astype(BF16)
    slabs = MIX_CHUNK // LANES

    def proj(col0):
        return jnp.dot(xn, win_ref[:, col0:col0 + MIX_CHUNK], preferred_element_type=F32)

    for c in range(A_WIDTH // MIX_CHUNK):
        h = proj(c * MIX_CHUNK)
        gate_b = proj(A_WIDTH + c * MIX_CHUNK)
        gate_c = proj(2 * A_WIDTH + c * MIX_CHUNK)
        ch = gate_c * h
        for sl in range(slabs):
            slot = c * slabs + sl
            cols = slice(slot * LANES, (slot + 1) * LANES)
            ch_sl = ch[:, sl * LANES:(sl + 1) * LANES]
            _stage_rows(scr, slot, base, ch_sl, carry_a[CONV_HALO - 2:, cols])
            carry_a[:, cols] = ch_sl[ts - CONV_HALO:, :]
            cw = cw_ref[:, cols]
            conv = (cw[0:1] * _rows_back(scr, slot, base, ts, 2)
                    + cw[1:2] * _rows_back(scr, slot, base, ts, 1) + cw[2:3] * ch_sl)
            y_buf[:, cols] = (gate_b[:, sl * LANES:(sl + 1) * LANES] * conv).astype(BF16)

    pos = s * ts + lax.broadcasted_iota(jnp.int32, (ts, 1), 0)
    for c in range(B_WIDTH // MIX_CHUNK):
        pin = proj(3 * A_WIDTH + c * MIX_CHUNK)
        for sl in range(slabs):
            g = c * slabs + sl
            k = POOL_WINDOWS[g]
            slot = A_WIDTH // LANES + g
            cols = slice(g * POOL_GROUP, (g + 1) * POOL_GROUP)
            cur = pin[:, sl * LANES:(sl + 1) * LANES]
            _stage_rows(scr, slot, base, cur, carry_p[:, cols])
            carry_p[:, cols] = cur[ts - POOL_HALO:, :]
            acc = cur
            for j in range(1, k):
                acc = acc + _rows_back(scr, slot, base, ts, j)
            cnt = jnp.minimum(pos + 1, k).astype(F32)
            pooled = acc / cnt - cur
            yb = jnp.dot(pooled.astype(BF16), pw_ref[g], preferred_element_type=F32)
            y_buf[:, A_WIDTH + g * POOL_GROUP:A_WIDTH + (g + 1) * POOL_GROUP] = (
                yb * ps_ref[:, cols]).astype(BF16)

    o_ref[0] = x + jnp.dot(y_buf[...], wout_ref[...], preferred_element_type=F32)


def _mixer(x, g, w_in, conv_w, pool_w, pool_scale, w_out, to_cast):
    b, s, d = x.shape
    ts = ROW_TILE
    n_steps = b * (s // ts)
    row_spec = pl.BlockSpec((1, ts, d), lambda i, j: (i, j, 0))
    cast_specs = [pl.BlockSpec((w.shape[0] // n_steps, w.shape[1]),
                               lambda i, j: (i * (s // ts) + j, 0)) for w in to_cast]
    return pl.pallas_call(
        _mixer_kernel,
        grid=(b, s // ts),
        in_specs=[row_spec, _const_spec((1, d)), _const_spec((d, EVEN_IN)),
                  _const_spec((CONV_WIDTH, A_WIDTH)),
                  _const_spec((len(POOL_WINDOWS), POOL_GROUP, POOL_GROUP)),
                  _const_spec((1, B_WIDTH)), _const_spec((d, d))] + cast_specs,
        out_specs=[row_spec] + cast_specs,
        out_shape=[jax.ShapeDtypeStruct(x.shape, F32)]
                  + [jax.ShapeDtypeStruct(w.shape, BF16) for w in to_cast],
        scratch_shapes=[pltpu.VMEM((ts, d), BF16),
                        pltpu.VMEM((CONV_HALO, A_WIDTH), F32),
                        pltpu.VMEM((POOL_HALO, B_WIDTH), F32),
                        pltpu.VMEM(((A_WIDTH + B_WIDTH) // LANES, 2 * (POOL_HALO + ts), LANES),
                                   F32)],
        compiler_params=_params(2),
        name="mixer",
    )(x, g, w_in, conv_w, pool_w, pool_scale, w_out, *to_cast)


def _ffn_kernel(x_ref, g_ref, wup_ref, cw_ref, cb_ref, wdn_ref, o_ref, h_buf, carry, u_scr):
    ts = x_ref.shape[1]
    s = pl.program_id(1)

    @pl.when(s == 0)
    def _():
        carry[...] = jnp.zeros(carry.shape, F32)

    x = x_ref[0]
    xn = _rmsnorm(x, g_ref[...]).astype(BF16)
    slabs = FF_CHUNK // LANES
    for j in range(D_FF // FF_CHUNK):
        halves = []
        for half in range(2):
            c0 = half * D_FF + j * FF_CHUNK
            u = jnp.dot(xn, wup_ref[:, c0:c0 + FF_CHUNK], preferred_element_type=F32)
            parts = []
            for sl in range(slabs):
                cols = slice(c0 + sl * LANES, c0 + (sl + 1) * LANES)
                slot = ((j % 2) * 2 + half) * slabs + sl
                u_sl = u[:, sl * LANES:(sl + 1) * LANES]
                _stage_rows(u_scr, slot, SHIFT_BASE, u_sl, carry[CONV_HALO - 2:, cols])
                carry[:, cols] = u_sl[ts - CONV_HALO:ts, :]
                w = cw_ref[:, cols]
                parts.append(w[0:1] * _rows_back(u_scr, slot, SHIFT_BASE, ts, 2)
                             + w[1:2] * _rows_back(u_scr, slot, SHIFT_BASE, ts, 1)
                             + w[2:3] * u_sl + cb_ref[:, cols])
            halves.append(jnp.concatenate(parts, axis=-1))
        gate, up = halves
        h_buf[:, j * FF_CHUNK:(j + 1) * FF_CHUNK] = (gate * jax.nn.sigmoid(gate) * up).astype(BF16)
    o_ref[0] = x + jnp.dot(h_buf[...], wdn_ref[...], preferred_element_type=F32)


def _ffn(x, layer, g, w_up, conv_w, conv_b, w_down):
    b, s, d = x.shape
    ts = ROW_TILE
    row_spec = pl.BlockSpec((1, ts, d), lambda i, j: (i, j, 0))
    return pl.pallas_call(
        _ffn_kernel,
        grid=(b, s // ts),
        in_specs=[row_spec, _layer_spec((1, d), layer), _layer_spec((d, 2 * D_FF), layer),
                  _layer_spec((CONV_WIDTH, 2 * D_FF), layer),
                  _layer_spec((1, 2 * D_FF), layer), _layer_spec((D_FF, d), layer)],
        out_specs=row_spec,
        out_shape=jax.ShapeDtypeStruct(x.shape, F32),
        scratch_shapes=[pltpu.VMEM((ts, D_FF), BF16),
                        pltpu.VMEM((CONV_HALO, 2 * D_FF), F32),
                        pltpu.VMEM((4 * FF_CHUNK // LANES, SHIFT_BASE + 2 * ts, LANES), F32)],
        compiler_params=_params(2),
        name="ffn",
    )(x, g, w_up, conv_w, conv_b, w_down)


def _head_pair_norm(t, g_pair, scale):
    lane = lax.broadcasted_iota(jnp.int32, t.shape, 1)
    lo = lane < HEAD_DIM
    sq = t * t
    s_lo = jnp.sum(jnp.where(lo, sq, 0.0), axis=-1, keepdims=True)
    s_hi = jnp.sum(jnp.where(lo, 0.0, sq), axis=-1, keepdims=True)
    r = lax.rsqrt(jnp.where(lo, s_lo, s_hi) * (1.0 / HEAD_DIM) + EPS)
    return t * r * (g_pair * scale)


def _qkv_kernel(x_ref, g_ref, w_ref, gq_ref, gk_ref, rows_ref, *refs, splits):
    assert DILATIONS == (1, 4, 16)
    out4, out16, bias_ref = refs[0:3], refs[3:6], refs[6]
    x_stage, stage = refs[7:]
    _write_bias_tables(rows_ref, bias_ref, splits)
    d = D_MODEL
    ts = x_ref.shape[1]
    tj = ts // 4
    for sl in range(d // LANES):
        x_stage[sl] = x_ref[0, :, sl * LANES:(sl + 1) * LANES]
    x = jnp.concatenate(
        [jnp.concatenate([x_stage[sl, pl.ds(r4, tj, stride=4), :] for sl in range(d // LANES)],
                         axis=-1) for r4 in range(4)], axis=0)
    xn = _rmsnorm(x, g_ref[...]).astype(BF16)
    slabs = QKV_CHUNK // LANES
    for c in range(3 * d // QKV_CHUNK):
        res = jnp.dot(xn, w_ref[:, c * QKV_CHUNK:(c + 1) * QKV_CHUNK],
                      preferred_element_type=F32)
        for sl in range(slabs):
            slot = c * slabs + sl
            which, hp = divmod(slot, d // LANES)
            cols = slice(hp * LANES, (hp + 1) * LANES)
            slab = res[:, sl * LANES:(sl + 1) * LANES]
            if which == 0:
                slab = _head_pair_norm(slab, gq_ref[...], HEAD_DIM ** -0.5 * LOG2E)
            elif which == 1:
                slab = _head_pair_norm(slab, gk_ref[...], 1.0)
            stage[slot % QKV_STAGE_SLOTS] = slab
            for r4 in range(4):
                out4[which][r4, :, cols] = slab[r4 * tj:(r4 + 1) * tj].astype(BF16)
                for c4 in range(4):
                    p16 = stage[slot % QKV_STAGE_SLOTS, pl.ds(r4 * tj + c4, tj // 4, stride=4), :]
                    out16[which][4 * r4 + c4, :, cols] = p16.astype(BF16)


def _qkv(x, g, w_qkv, gq_pair, gk_pair, bias_rows, splits):
    b, s, d = x.shape
    ts = QKV_ROW_TILE
    nt = s // ts
    n_br, n_heads = bias_rows.shape[:2]
    assert b * nt == n_heads
    row_spec = pl.BlockSpec((1, ts, d), lambda i, j: (i, j, 0))
    out_specs, out_shapes = [], []
    for planes in (4, 16):
        out_specs += [pl.BlockSpec((None, planes, ts // planes, d), lambda i, j: (i, 0, j, 0))] * 3
        out_shapes += [jax.ShapeDtypeStruct((b, planes, s // planes, d), BF16)] * 3
    out_specs.append(pl.BlockSpec((n_br, 1, ATT_BLOCK, 2 * ATT_BLOCK),
                                  lambda i, j: (0, i * nt + j, 0, 0)))
    out_shapes.append(jax.ShapeDtypeStruct((n_br, n_heads, ATT_BLOCK, 2 * ATT_BLOCK), F32))
    outs = pl.pallas_call(
        functools.partial(_qkv_kernel, splits=splits),
        grid=(b, nt),
        in_specs=[row_spec, _const_spec((1, d)), _const_spec((d, 3 * d)),
                  _const_spec((1, LANES)), _const_spec((1, LANES)),
                  pl.BlockSpec((n_br, 1) + bias_rows.shape[2:],
                               lambda i, j: (0, i * nt + j, 0, 0, 0))],
        out_specs=out_specs,
        out_shape=out_shapes,
        scratch_shapes=[pltpu.VMEM((d // LANES, ts, LANES), F32),
                        pltpu.VMEM((QKV_STAGE_SLOTS, ts, LANES), F32)],
        compiler_params=_params(2),
        name="qkv",
    )(x, g, w_qkv, gq_pair, gk_pair, bias_rows)
    return outs[0:3], outs[3:6], outs[6]


def _stat_lane(h):
    return HEAD_DIM + h // 2 if h % 2 == 0 else h // 2


def _rows_cat(parts):
    return parts[0] if len(parts) == 1 else jnp.concatenate(parts, axis=0)


def _attn_kernel(q_ref, k_ref, v_ref, kh_ref, vh_ref, bias_ref, o_ref, m_ref, l_ref, *scratch,
                 split, interleave):
    nb = ATT_BLOCK
    n_planes, rows, _ = q_ref.shape
    piece = nb // split
    col = lax.broadcasted_iota(jnp.int32, (nb, 2 * nb), 1)
    before_start = jnp.logical_and(col % (2 * piece) < piece, pl.program_id(2) == 0)
    if interleave:
        o_stage, m_tiles, l_tiles = scratch

    lo = lax.broadcasted_iota(jnp.int32, (nb, LANES), 1) < HEAD_DIM
    lo_kv = lax.broadcasted_iota(jnp.int32, (2 * nb, LANES), 1) < HEAD_DIM

    for grp in range(n_planes // split):
        planes = range(grp * split, (grp + 1) * split)
        for u in range(rows // piece):
            cur = slice(u * piece, (u + 1) * piece)
            tile = grp * (rows // piece) + u
            out_rows = pl.ds(interleave * nb * u + grp, nb, stride=interleave) if interleave else None

            def block_rows(ref, cols):
                return _rows_cat([ref[p, cur, cols] for p in planes])

            def window_rows(ref, halo_ref, cols):
                parts = []
                for p in planes:
                    if u == 0:
                        parts += [halo_ref[p, :, cols], ref[p, cur, cols]]
                    else:
                        parts.append(ref[p, (u - 1) * piece:(u + 1) * piece, cols])
                return _rows_cat(parts)

            def put_stat(ref, tiles, lane_cols, value):
                if interleave:
                    tiles[tile, :, lane_cols] = value
                else:
                    for n, p in enumerate(planes):
                        ref[p, cur, lane_cols] = value[n * piece:(n + 1) * piece]

            put_stat(m_ref, m_tiles if interleave else None, slice(None), jnp.zeros((nb, LANES), F32))
            put_stat(l_ref, l_tiles if interleave else None, slice(None), jnp.ones((nb, LANES), F32))
            for hp in range(N_HEADS // 2):
                cols = slice(hp * LANES, (hp + 1) * LANES)
                q_pair = block_rows(q_ref, cols)
                k_win = window_rows(k_ref, kh_ref, cols)
                v_win = window_rows(v_ref, vh_ref, cols)
                outs = []
                for sub in range(2):
                    h = 2 * hp + sub
                    own = lo if sub == 0 else jnp.logical_not(lo)
                    own_kv = lo_kv if sub == 0 else jnp.logical_not(lo_kv)
                    q_h = jnp.where(own, q_pair, jnp.zeros_like(q_pair))
                    sc = lax.dot_general(q_h, k_win, (((1,), (1,)), ((), ())),
                                         preferred_element_type=F32)
                    sc = sc + bias_ref[h]
                    if u == 0:
                        sc = jnp.where(before_start, MASK_VALUE, sc)
                    m = jnp.max(sc, axis=-1, keepdims=True)
                    p = jnp.exp2(sc - m).astype(BF16)
                    v_h = jnp.where(own_kv, v_win, jnp.ones_like(v_win))
                    pv = jnp.dot(p, v_h, preferred_element_type=F32)
                    outs.append(pv)
                    sl = slice(_stat_lane(h), _stat_lane(h) + 1)
                    put_stat(m_ref, m_tiles if interleave else None, sl, m)
                    put_stat(l_ref, l_tiles if interleave else None, sl, pv[:, sl])
                o_pair = jnp.where(lo, outs[0], outs[1])
                if interleave:
                    o_stage[hp, out_rows, :] = o_pair
                else:
                    for n, p in enumerate(planes):
                        o_ref[p, cur, cols] = o_pair[n * piece:(n + 1) * piece].astype(BF16)
            if interleave:
                m_ref[out_rows, :] = m_tiles[tile]
                l_ref[out_rows, :] = l_tiles[tile]
    if interleave:
        for hp in range(N_HEADS // 2):
            o_ref[:, hp * LANES:(hp + 1) * LANES] = o_stage[hp].astype(BF16)


def _attn_branch(q, k, v, bias, *, branch, split=1, interleave=0):
    b, n_planes, length, d = q.shape
    piece = ATT_BLOCK // split
    units = min(ATT_UNITS_PER_STEP, length // piece)
    step_planes = interleave or split * min(n_planes // split, ATT_UNITS_PER_STEP // units)
    rows = units * piece
    main = pl.BlockSpec((None, step_planes, rows, d), lambda bi, r, i: (bi, r, i, 0))
    halo = pl.BlockSpec((None, step_planes, piece, d),
                        lambda bi, r, i: (bi, r, jnp.maximum(i * units - 1, 0), 0))
    scratch = []
    if interleave:
        assert interleave == step_planes and rows == length and split == 1
        out_planes, out_len = n_planes // interleave, interleave * length
        o_spec = pl.BlockSpec((None, None, out_len, d), lambda bi, r, i: (bi, r, 0, 0))
        stat = pl.BlockSpec((None, None, out_len, LANES), lambda bi, r, i: (bi, r, 0, 0))
        n_tiles = step_planes * units
        scratch = [pltpu.VMEM((d // LANES, out_len, LANES), F32),
                   pltpu.VMEM((n_tiles, ATT_BLOCK, LANES), F32),
                   pltpu.VMEM((n_tiles, ATT_BLOCK, LANES), F32)]
    else:
        out_planes, out_len = n_planes, length
        o_spec = main
        stat = pl.BlockSpec((None, step_planes, rows, LANES), lambda bi, r, i: (bi, r, i, 0))
    stat_shape = jax.ShapeDtypeStruct((b, out_planes, out_len, LANES), F32)
    return pl.pallas_call(
        functools.partial(_attn_kernel, split=split, interleave=interleave),
        grid=(b, n_planes // step_planes, length // rows),
        in_specs=[main, main, main, halo, halo, _layer_spec(bias.shape[1:], branch)],
        out_specs=[o_spec, stat, stat],
        out_shape=[jax.ShapeDtypeStruct((b, out_planes, out_len, d), BF16), stat_shape, stat_shape],
        scratch_shapes=scratch,
        compiler_params=_params(3),
        name=f"attn_d{DILATIONS[branch]}",
    )(q, k, v, k, v, bias)


def _merge_kernel(x_ref, *refs):
    n_br = len(DILATIONS)
    branch_refs = [refs[3 * i:3 * i + 3] for i in range(n_br)]
    e_ref, wo_ref, out_ref, y_scr, merged_buf = refs[3 * n_br:]
    planes, tj, _ = branch_refs[0][0].shape

    def plane_major(ref, cols=slice(None)):
        return jnp.concatenate([ref[p, :, cols] for p in range(planes)], axis=0)

    ms = [plane_major(br[1]) for br in branch_refs]
    ls = [plane_major(br[2]) for br in branch_refs]
    m_max = functools.reduce(jnp.maximum, ms)
    es = [jnp.exp2(m - m_max) for m in ms]
    total = functools.reduce(jnp.add, [e * l for e, l in zip(es, ls)])
    weights = []
    for e in es:
        c = e / total
        c_hi = c.astype(BF16)
        weights.append(jnp.concatenate([c_hi, (c - c_hi.astype(F32)).astype(BF16)], axis=-1))
    for chunk in range(D_MODEL // MERGE_CHUNK):
        cols = slice(chunk * MERGE_CHUNK, (chunk + 1) * MERGE_CHUNK)
        merged = None
        for w, br in zip(weights, branch_refs):
            c_wide = jnp.dot(w, e_ref[:, cols], preferred_element_type=F32)
            term = c_wide * plane_major(br[0], cols).astype(F32)
            merged = term if merged is None else merged + term
        merged_buf[:, cols] = merged.astype(BF16)
    y = jnp.dot(merged_buf[...], wo_ref[...], preferred_element_type=F32)
    for slab in range(D_MODEL // LANES):
        cols = slice(slab * LANES, (slab + 1) * LANES)
        for p in range(planes):
            y_scr[slab, pl.ds(p, tj, stride=planes), :] = y[p * tj:(p + 1) * tj, cols]
        out_ref[0, :, cols] = x_ref[0, :, cols] + y_scr[slab]


def _merge_out_proj(x, branches, expand, w_o):
    b, s, d = x.shape
    ts = ROW_TILE
    planes = branches[0][0].shape[1]
    row_spec = pl.BlockSpec((1, ts, d), lambda i, j: (i, j, 0))
    o_spec = pl.BlockSpec((None, planes, ts // planes, d), lambda i, j: (i, 0, j, 0))
    stat_spec = pl.BlockSpec((None, planes, ts // planes, LANES), lambda i, j: (i, 0, j, 0))
    args = [t for br in branches for t in br]
    return pl.pallas_call(
        _merge_kernel,
        grid=(b, s // ts),
        in_specs=[row_spec] + [o_spec, stat_spec, stat_spec] * len(branches)
                 + [_const_spec(expand.shape), _const_spec((d, d))],
        out_specs=row_spec,
        out_shape=jax.ShapeDtypeStruct(x.shape, F32),
        scratch_shapes=[pltpu.VMEM((d // LANES, ts, LANES), F32), pltpu.VMEM((ts, d), BF16)],
        compiler_params=_params(2),
        name="merge_out_proj",
    )(x, *args, expand, w_o)


def _t5_causal_bucket(dist):
    max_exact = N_REL_BUCKETS // 2
    dd = jnp.maximum(dist, 1).astype(F32)
    large = max_exact + (jnp.log(dd / max_exact) / math.log(REL_MAX_DISTANCE / max_exact)
                         * (N_REL_BUCKETS - max_exact)).astype(jnp.int32)
    large = jnp.minimum(large, N_REL_BUCKETS - 1)
    return jnp.where(dist < max_exact, dist, large)


def _bias_rows(rel_table, splits):
    n = ATT_BLOCK
    n_br = len(DILATIONS)
    n_heads = rel_table.shape[1]
    assert splits[0] in (1, 4) and all(s == 1 for s in splits[1:])
    dist = (n - jnp.arange(n + 1))[None, :] * jnp.asarray(DILATIONS)[:, None]
    by_offset = LOG2E * jnp.swapaxes(rel_table.astype(F32)[_t5_causal_bucket(dist)], 1, 2)

    def mask_fill(count):
        return jnp.full((n_heads, count), MASK_VALUE, F32)

    split = splits[0]
    rows = []
    for br in range(n_br):
        w = by_offset[br]
        if br > 0 or split == 1:
            rows.append(jnp.stack([jnp.concatenate([w, mask_fill(4 * n - (n + 1))], axis=1)] * 4, 1))
            continue
        groups = []
        for r in range(split):
            parts = []
            for rp in range(split):
                first = 0 if rp >= r else 1
                vals = w[:, split * first + rp - r::split]
                parts += [mask_fill(first), vals, mask_fill(n - first - vals.shape[1])]
            groups.append(jnp.concatenate(parts, axis=1))
        rows.append(jnp.stack(groups, axis=1))
    return jnp.stack(rows)[:, :, :, None, :]


def _write_bias_tables(rows_ref, out_ref, splits):
    n = ATT_BLOCK
    for br, sp in enumerate(splits):
        if sp == 1:
            wide = jnp.broadcast_to(rows_ref[br, 0, 0][:, :3 * n], (n, 3 * n))
            tab = pltpu.roll(wide, 0, 1, stride=1, stride_axis=0)[:, :2 * n]
        else:
            groups = []
            for r in range(sp):
                wide = jnp.broadcast_to(rows_ref[br, 0, r], (n // sp, 4 * n))
                rolled = pltpu.roll(wide, 0, 1, stride=1, stride_axis=0)
                groups.append(jnp.concatenate(
                    [rolled[:, rp * n:rp * n + 2 * n // sp] for rp in range(sp)], axis=1))
            tab = jnp.concatenate(groups, axis=0)
        out_ref[br, 0] = tab


def _head_expand_matrix():
    row = jnp.arange(2 * LANES)[:, None] % LANES
    head = jnp.arange(D_MODEL)[None, :] // HEAD_DIM
    stat_lane = jnp.where(head % 2 == 0, HEAD_DIM + head // 2, head // 2)
    return (row == stat_lane).astype(BF16)


def kernel(x, rel_bias, even_norm, even_w_in, even_conv_w, even_pool_w, even_pool_scale,
           even_w_out, odd_norm, odd_w_qkv, odd_q_norm, odd_k_norm, odd_w_o, ffn_norm,
           ffn_w_up, ffn_conv_w, ffn_conv_b, ffn_w_down):
    depth, d, d_ff2 = ffn_w_up.shape
    x, w_up, w_down, w_qkv, w_o = _mixer(
        x, even_norm[0][None], even_w_in[0].astype(BF16), even_conv_w[0],
        even_pool_w[0].astype(BF16), even_pool_scale[0][None], even_w_out[0].astype(BF16),
        to_cast=[ffn_w_up.reshape(depth * d, d_ff2), ffn_w_down.reshape(depth * D_FF, d),
                 odd_w_qkv[0], odd_w_o[0]])
    ffn_params = (ffn_norm[:, None], w_up.reshape(depth, d, d_ff2), ffn_conv_w,
                  ffn_conv_b[:, None], w_down.reshape(depth, D_FF, d))

    def ffn(t, layer):
        return _ffn(t, layer, *ffn_params)

    x = ffn(x, 0)

    gq_pair = jnp.tile(odd_q_norm[0], 2)[None]
    gk_pair = jnp.tile(odd_k_norm[0], 2)[None]
    splits = (4, 1, 1)
    qkv4, qkv16, bias = _qkv(x, odd_norm[0][None], w_qkv, gq_pair, gk_pair,
                             _bias_rows(rel_bias, splits), splits)
    branches = [
        _attn_branch(*qkv4, bias, branch=0, split=splits[0]),
        _attn_branch(*qkv4, bias, branch=1),
        _attn_branch(*qkv16, bias, branch=2, interleave=4),
    ]
    x = _merge_out_proj(x, branches, _head_expand_matrix(), w_o)
    x = ffn(x, 1)
    return x
```

```python
import functools
import math

import jax
import jax.numpy as jnp
import numpy as np
from jax import lax
from jax.experimental import pallas as pl
from jax.experimental.pallas import tpu as pltpu

D_MODEL = 1024
CONV_WIDTH = 3
A_WIDTH = 512
B_WIDTH = 512
POOL_WINDOWS = (2, 4, 8, 16)
POOL_GROUP = 128
EVEN_IN = 3 * A_WIDTH + B_WIDTH
HEAD_DIM = 64
N_HEADS = 16
DILATED_PAIRS = ((128, 1), (512, 4), (2048, 16))
DILATIONS = tuple(dil for _, dil in DILATED_PAIRS)
N_REL_BUCKETS = 32
REL_MAX_DISTANCE = 2048
D_FF = 2816
EPS = 1e-6
MASK_VALUE = -1e30

BF16 = jnp.bfloat16
F32 = jnp.float32

LANES = 128
ATT_BLOCK = 128
ATT_UNITS_PER_STEP = 16
LOG2E = math.log2(math.e)
QKV_ROW_TILE = 1024
QKV_STAGE_SLOTS = 8
ROW_TILE = 1024
POOL_HALO = 16
CONV_HALO = 8
FF_CHUNK = 256
QKV_CHUNK = 256
MIX_CHUNK = 512
MERGE_CHUNK = 512
SHIFT_BASE = 8
V7X_VMEM_BYTES = 64 * 1024 * 1024
VMEM_LIMIT = V7X_VMEM_BYTES - 8 * 1024 * 1024


def _const_spec(shape):
    nd = len(shape)
    return pl.BlockSpec(shape, lambda *_: (0,) * nd, pipeline_mode=pl.Buffered(1))


def _layer_spec(shape, layer):
    nd = len(shape)
    return pl.BlockSpec((None,) + tuple(shape), lambda *_: (layer,) + (0,) * nd,
                        pipeline_mode=pl.Buffered(1))


def _params(n_axes):
    return pltpu.CompilerParams(
        dimension_semantics=("arbitrary",) * n_axes, vmem_limit_bytes=VMEM_LIMIT)


def _rmsnorm(x, g):
    ms = jnp.mean(x * x, axis=-1, keepdims=True)
    return x * lax.rsqrt(ms + EPS) * g


def _stage_rows(scr, slot, base, tile, prev):
    n_prev = prev.shape[0]
    scr[slot, pl.ds(base - 2 * n_prev, n_prev, stride=2), :] = prev
    scr[slot, pl.ds(base, tile.shape[0], stride=2), :] = tile


def _rows_back(scr, slot, base, n_rows, k):
    return scr[slot, pl.ds(base - 2 * k, n_rows, stride=2), :]


def _mixer_kernel(x_ref, g_ref, win_ref, cw_ref, pw_ref, ps_ref, wout_ref, *refs):
    n_cast = (len(refs) - 5) // 2
    cast_src, o_ref, cast_dst = refs[:n_cast], refs[n_cast], refs[n_cast + 1:2 * n_cast + 1]
    y_buf, carry_a, carry_p, scr = refs[2 * n_cast + 1:]
    for src, dst in zip(cast_src, cast_dst):
        dst[...] = src[...].astype(BF16)
    ts = x_ref.shape[1]
    s = pl.program_id(1)
    base = 2 * POOL_HALO

    @pl.when(s == 0)
    def _():
        carry_a[...] = jnp.zeros(carry_a.shape, F32)
        carry_p[...] = jnp.zeros(carry_p.shape, F32)

    x = x_ref[0]
    xn = _rmsnorm(x, g_ref[...]).astype(BF16)
    slabs = MIX_CHUNK // LANES

    def proj(col0):
        return jnp.dot(xn, win_ref[:, col0:col0 + MIX_CHUNK], preferred_element_type=F32)

    for c in range(A_WIDTH // MIX_CHUNK):
        h = proj(c * MIX_CHUNK)
        gate_b = proj(A_WIDTH + c * MIX_CHUNK)
        gate_c = proj(2 * A_WIDTH + c * MIX_CHUNK)
        ch = gate_c * h
        for sl in range(slabs):
            slot = c * slabs + sl
            cols = slice(slot * LANES, (slot + 1) * LANES)
            ch_sl = ch[:, sl * LANES:(sl + 1) * LANES]
            _stage_rows(scr, slot, base, ch_sl, carry_a[CONV_HALO - 2:, cols])
            carry_a[:, cols] = ch_sl[ts - CONV_HALO:, :]
            cw = cw_ref[:, cols]
            conv = (cw[0:1] * _rows_back(scr, slot, base, ts, 2)
                    + cw[1:2] * _rows_back(scr, slot, base, ts, 1) + cw[2:3] * ch_sl)
            y_buf[:, cols] = (gate_b[:, sl * LANES:(sl + 1) * LANES] * conv).astype(BF16)

    pos = s * ts + lax.broadcasted_iota(jnp.int32, (ts, 1), 0)
    for c in range(B_WIDTH // MIX_CHUNK):
        pin = proj(3 * A_WIDTH + c * MIX_CHUNK)
        for sl in range(slabs):
            g = c * slabs + sl
            k = POOL_WINDOWS[g]
            slot = A_WIDTH // LANES + g
            cols = slice(g * POOL_GROUP, (g + 1) * POOL_GROUP)
            cur = pin[:, sl * LANES:(sl + 1) * LANES]
            _stage_rows(scr, slot, base, cur, carry_p[:, cols])
            carry_p[:, cols] = cur[ts - POOL_HALO:, :]
            acc = cur
            for j in range(1, k):
                acc = acc + _rows_back(scr, slot, base, ts, j)
            cnt = jnp.minimum(pos + 1, k).astype(F32)
            pooled = acc / cnt - cur
            yb = jnp.dot(pooled.astype(BF16), pw_ref[g], preferred_element_type=F32)
            y_buf[:, A_WIDTH + g * POOL_GROUP:A_WIDTH + (g + 1) * POOL_GROUP] = (
                yb * ps_ref[:, cols]).astype(BF16)

    o_ref[0] = x + jnp.dot(y_buf[...], wout_ref[...], preferred_element_type=F32)


def _mixer(x, g, w_in, conv_w, pool_w, pool_scale, w_out, to_cast):
    b, s, d = x.shape
    ts = ROW_TILE
    n_steps = b * (s // ts)
    row_spec = pl.BlockSpec((1, ts, d), lambda i, j: (i, j, 0))
    cast_specs = [pl.BlockSpec((w.shape[0] // n_steps, w.shape[1]),
                               lambda i, j: (i * (s // ts) + j, 0)) for w in to_cast]
    return pl.pallas_call(
        _mixer_kernel,
        grid=(b, s // ts),
        in_specs=[row_spec, _const_spec((1, d)), _const_spec((d, EVEN_IN)),
                  _const_spec((CONV_WIDTH, A_WIDTH)),
                  _const_spec((len(POOL_WINDOWS), POOL_GROUP, POOL_GROUP)),
                  _const_spec((1, B_WIDTH)), _const_spec((d, d))] + cast_specs,
        out_specs=[row_spec] + cast_specs,
        out_shape=[jax.ShapeDtypeStruct(x.shape, F32)]
                  + [jax.ShapeDtypeStruct(w.shape, BF16) for w in to_cast],
        scratch_shapes=[pltpu.VMEM((ts, d), BF16),
                        pltpu.VMEM((CONV_HALO, A_WIDTH), F32),
                        pltpu.VMEM((POOL_HALO, B_WIDTH), F32),
                        pltpu.VMEM(((A_WIDTH + B_WIDTH) // LANES, 2 * (POOL_HALO + ts), LANES),
                                   F32)],
        compiler_params=_params(2),
        name="mixer",
    )(x, g, w_in, conv_w, pool_w, pool_scale, w_out, *to_cast)


def _ffn_kernel(x_ref, g_ref, wup_ref, cw_ref, cb_ref, wdn_ref, o_ref, h_buf, carry, u_scr):
    ts = x_ref.shape[1]
    s = pl.program_id(1)

    @pl.when(s == 0)
    def _():
        carry[...] = jnp.zeros(carry.shape, F32)

    x = x_ref[0]
    xn = _rmsnorm(x, g_ref[...]).astype(BF16)
    slabs = FF_CHUNK // LANES
    for j in range(D_FF // FF_CHUNK):
        halves = []
        for half in range(2):
            c0 = half * D_FF + j * FF_CHUNK
            u = jnp.dot(xn, wup_ref[:, c0:c0 + FF_CHUNK], preferred_element_type=F32)
            parts = []
            for sl in range(slabs):
                cols = slice(c0 + sl * LANES, c0 + (sl + 1) * LANES)
                slot = ((j % 2) * 2 + half) * slabs + sl
                u_sl = u[:, sl * LANES:(sl + 1) * LANES]
                _stage_rows(u_scr, slot, SHIFT_BASE, u_sl, carry[CONV_HALO - 2:, cols])
                carry[:, cols] = u_sl[ts - CONV_HALO:ts, :]
                w = cw_ref[:, cols]
                parts.append(w[0:1] * _rows_back(u_scr, slot, SHIFT_BASE, ts, 2)
                             + w[1:2] * _rows_back(u_scr, slot, SHIFT_BASE, ts, 1)
                             + w[2:3] * u_sl + cb_ref[:, cols])
            halves.append(jnp.concatenate(parts, axis=-1))
        gate, up = halves
        h_buf[:, j * FF_CHUNK:(j + 1) * FF_CHUNK] = (gate * jax.nn.sigmoid(gate) * up).astype(BF16)
    o_ref[0] = x + jnp.dot(h_buf[...], wdn_ref[...], preferred_element_type=F32)


def _ffn(x, layer, g, w_up, conv_w, conv_b, w_down):
    b, s, d = x.shape
    ts = ROW_TILE
    row_spec = pl.BlockSpec((1, ts, d), lambda i, j: (i, j, 0))
    return pl.pallas_call(
        _ffn_kernel,
        grid=(b, s // ts),
        in_specs=[row_spec, _layer_spec((1, d), layer), _layer_spec((d, 2 * D_FF), layer),
                  _layer_spec((CONV_WIDTH, 2 * D_FF), layer),
                  _layer_spec((1, 2 * D_FF), layer), _layer_spec((D_FF, d), layer)],
        out_specs=row_spec,
        out_shape=jax.ShapeDtypeStruct(x.shape, F32),
        scratch_shapes=[pltpu.VMEM((ts, D_FF), BF16),
                        pltpu.VMEM((CONV_HALO, 2 * D_FF), F32),
                        pltpu.VMEM((4 * FF_CHUNK // LANES, SHIFT_BASE + 2 * ts, LANES), F32)],
        compiler_params=_params(2),
        name="ffn",
    )(x, g, w_up, conv_w, conv_b, w_down)


def _head_pair_norm(t, g_pair, scale):
    lane = lax.broadcasted_iota(jnp.int32, t.shape, 1)
    lo = lane < HEAD_DIM
    sq = t * t
    s_lo = jnp.sum(jnp.where(lo, sq, 0.0), axis=-1, keepdims=True)
    s_hi = jnp.sum(jnp.where(lo, 0.0, sq), axis=-1, keepdims=True)
    r = lax.rsqrt(jnp.where(lo, s_lo, s_hi) * (1.0 / HEAD_DIM) + EPS)
    return t * r * (g_pair * scale)


def _qkv_kernel(x_ref, g_ref, w_ref, gq_ref, gk_ref, rows_ref, *refs, splits):
    assert DILATIONS == (1, 4, 16)
    out4, out16, bias_ref = refs[0:3], refs[3:6], refs[6]
    x_stage, stage = refs[7:]
    _write_bias_tables(rows_ref, bias_ref, splits)
    d = D_MODEL
    ts = x_ref.shape[1]
    tj = ts // 4
    for sl in range(d // LANES):
        x_stage[sl] = x_ref[0, :, sl * LANES:(sl + 1) * LANES]
    x = jnp.concatenate(
        [jnp.concatenate([x_stage[sl, pl.ds(r4, tj, stride=4), :] for sl in range(d // LANES)],
                         axis=-1) for r4 in range(4)], axis=0)
    xn = _rmsnorm(x, g_ref[...]).astype(BF16)
    slabs = QKV_CHUNK // LANES
    for c in range(3 * d // QKV_CHUNK):
        res = jnp.dot(xn, w_ref[:, c * QKV_CHUNK:(c + 1) * QKV_CHUNK],
                      preferred_element_type=F32)
        for sl in range(slabs):
            slot = c * slabs + sl
            which, hp = divmod(slot, d // LANES)
            cols = slice(hp * LANES, (hp + 1) * LANES)
            slab = res[:, sl * LANES:(sl + 1) * LANES]
            if which == 0:
                slab = _head_pair_norm(slab, gq_ref[...], HEAD_DIM ** -0.5 * LOG2E)
            elif which == 1:
                slab = _head_pair_norm(slab, gk_ref[...], 1.0)
            stage[slot % QKV_STAGE_SLOTS] = slab
            for r4 in range(4):
                out4[which][r4, :, cols] = slab[r4 * tj:(r4 + 1) * tj].astype(BF16)
                for c4 in range(4):
                    p16 = stage[slot % QKV_STAGE_SLOTS, pl.ds(r4 * tj + c4, tj // 4, stride=4), :]
                    out16[which][4 * r4 + c4, :, cols] = p16.astype(BF16)


def _qkv(x, g, w_qkv, gq_pair, gk_pair, bias_rows, splits):
    b, s, d = x.shape
    ts = QKV_ROW_TILE
    nt = s // ts
    n_br, n_heads = bias_rows.shape[:2]
    assert b * nt == n_heads
    row_spec = pl.BlockSpec((1, ts, d), lambda i, j: (i, j, 0))
    out_specs, out_shapes = [], []
    for planes in (4, 16):
        out_specs += [pl.BlockSpec((None, planes, ts // planes, d), lambda i, j: (i, 0, j, 0))] * 3
        out_shapes += [jax.ShapeDtypeStruct((b, planes, s // planes, d), BF16)] * 3
    out_specs.append(pl.BlockSpec((n_br, 1, ATT_BLOCK, 2 * ATT_BLOCK),
                                  lambda i, j: (0, i * nt + j, 0, 0)))
    out_shapes.append(jax.ShapeDtypeStruct((n_br, n_heads, ATT_BLOCK, 2 * ATT_BLOCK), F32))
    outs = pl.pallas_call(
        functools.partial(_qkv_kernel, splits=splits),
        grid=(b, nt),
        in_specs=[row_spec, _const_spec((1, d)), _const_spec((d, 3 * d)),
                  _const_spec((1, LANES)), _const_spec((1, LANES)),
                  pl.BlockSpec((n_br, 1) + bias_rows.shape[2:],
                               lambda i, j: (0, i * nt + j, 0, 0, 0))],
        out_specs=out_specs,
        out_shape=out_shapes,
        scratch_shapes=[pltpu.VMEM((d // LANES, ts, LANES), F32),
                        pltpu.VMEM((QKV_STAGE_SLOTS, ts, LANES), F32)],
        compiler_params=_params(2),
        name="qkv",
    )(x, g, w_qkv, gq_pair, gk_pair, bias_rows)
    return outs[0:3], outs[3:6], outs[6]


def _stat_lane(h):
    return HEAD_DIM + h // 2 if h % 2 == 0 else h // 2


def _rows_cat(parts):
    return parts[0] if len(parts) == 1 else jnp.concatenate(parts, axis=0)


def _attn_kernel(q_ref, k_ref, v_ref, kh_ref, vh_ref, bias_ref, o_ref, m_ref, l_ref, *scratch,
                 split, interleave):
    nb = ATT_BLOCK
    n_planes, rows, _ = q_ref.shape
    piece = nb // split
    col = lax.broadcasted_iota(jnp.int32, (nb, 2 * nb), 1)
    before_start = jnp.logical_and(col % (2 * piece) < piece, pl.program_id(2) == 0)
    if interleave:
        o_stage, m_tiles, l_tiles = scratch

    lo = lax.broadcasted_iota(jnp.int32, (nb, LANES), 1) < HEAD_DIM
    lo_kv = lax.broadcasted_iota(jnp.int32, (2 * nb, LANES), 1) < HEAD_DIM

    for grp in range(n_planes // split):
        planes = range(grp * split, (grp + 1) * split)
        for u in range(rows // piece):
            cur = slice(u * piece, (u + 1) * piece)
            tile = grp * (rows // piece) + u
            out_rows = pl.ds(interleave * nb * u + grp, nb, stride=interleave) if interleave else None

            def block_rows(ref, cols):
                return _rows_cat([ref[p, cur, cols] for p in planes])

            def window_rows(ref, halo_ref, cols):
                parts = []
                for p in planes:
                    if u == 0:
                        parts += [halo_ref[p, :, cols], ref[p, cur, cols]]
                    else:
                        parts.append(ref[p, (u - 1) * piece:(u + 1) * piece, cols])
                return _rows_cat(parts)

            def put_stat(ref, tiles, lane_cols, value):
                if interleave:
                    tiles[tile, :, lane_cols] = value
                else:
                    for n, p in enumerate(planes):
                        ref[p, cur, lane_cols] = value[n * piece:(n + 1) * piece]

            put_stat(m_ref, m_tiles if interleave else None, slice(None), jnp.zeros((nb, LANES), F32))
            put_stat(l_ref, l_tiles if interleave else None, slice(None), jnp.ones((nb, LANES), F32))
            for hp in range(N_HEADS // 2):
                cols = slice(hp * LANES, (hp + 1) * LANES)
                q_pair = block_rows(q_ref, cols)
                k_win = window_rows(k_ref, kh_ref, cols)
                v_win = window_rows(v_ref, vh_ref, cols)
                outs = []
                for sub in range(2):
                    h = 2 * hp + sub
                    own = lo if sub == 0 else jnp.logical_not(lo)
                    own_kv = lo_kv if sub == 0 else jnp.logical_not(lo_kv)
                    q_h = jnp.where(own, q_pair, jnp.zeros_like(q_pair))
                    sc = lax.dot_general(q_h, k_win, (((1,), (1,)), ((), ())),
                                         preferred_element_type=F32)
                    sc = sc + bias_ref[h]
                    if u == 0:
                        sc = jnp.where(before_start, MASK_VALUE, sc)
                    m = jnp.max(sc, axis=-1, keepdims=True)
                    p = jnp.exp2(sc - m).astype(BF16)
                    v_h = jnp.where(own_kv, v_win, jnp.ones_like(v_win))
                    pv = jnp.dot(p, v_h, preferred_element_type=F32)
                    outs.append(pv)
                    sl = slice(_stat_lane(h), _stat_lane(h) + 1)
                    put_stat(m_ref, m_tiles if interleave else None, sl, m)
                    put_stat(l_ref, l_tiles if interleave else None, sl, pv[:, sl])
                o_pair = jnp.where(lo, outs[0], outs[1])
                if interleave:
                    o_stage[hp, out_rows, :] = o_pair
                else:
                    for n, p in enumerate(planes):
                        o_ref[p, cur, cols] = o_pair[n * piece:(n + 1) * piece].astype(BF16)
            if interleave:
                m_ref[out_rows, :] = m_tiles[tile]
                l_ref[out_rows, :] = l_tiles[tile]
    if interleave:
        for hp in range(N_HEADS // 2):
            o_ref[:, hp * LANES:(hp + 1) * LANES] = o_stage[hp].astype(BF16)


def _attn_branch(q, k, v, bias, *, branch, split=1, interleave=0):
    b, n_planes, length, d = q.shape
    piece = ATT_BLOCK // split
    units = min(ATT_UNITS_PER_STEP, length // piece)
    step_planes = interleave or split * min(n_planes // split, ATT_UNITS_PER_STEP // units)
    rows = units * piece
    main = pl.BlockSpec((None, step_planes, rows, d), lambda bi, r, i: (bi, r, i, 0))
    halo = pl.BlockSpec((None, step_planes, piece, d),
                        lambda bi, r, i: (bi, r, jnp.maximum(i * units - 1, 0), 0))
    scratch = []
    if interleave:
        assert interleave == step_planes and rows == length and split == 1
        out_planes, out_len = n_planes // interleave, interleave * length
        o_spec = pl.BlockSpec((None, None, out_len, d), lambda bi, r, i: (bi, r, 0, 0))
        stat = pl.BlockSpec((None, None, out_len, LANES), lambda bi, r, i: (bi, r, 0, 0))
        n_tiles = step_planes * units
        scratch = [pltpu.VMEM((d // LANES, out_len, LANES), F32),
                   pltpu.VMEM((n_tiles, ATT_BLOCK, LANES), F32),
                   pltpu.VMEM((n_tiles, ATT_BLOCK, LANES), F32)]
    else:
        out_planes, out_len = n_planes, length
        o_spec = main
        stat = pl.BlockSpec((None, step_planes, rows, LANES), lambda bi, r, i: (bi, r, i, 0))
    stat_shape = jax.ShapeDtypeStruct((b, out_planes, out_len, LANES), F32)
    return pl.pallas_call(
        functools.partial(_attn_kernel, split=split, interleave=interleave),
        grid=(b, n_planes // step_planes, length // rows),
        in_specs=[main, main, main, halo, halo, _layer_spec(bias.shape[1:], branch)],
        out_specs=[o_spec, stat, stat],
        out_shape=[jax.ShapeDtypeStruct((b, out_planes, out_len, d), BF16), stat_shape, stat_shape],
        scratch_shapes=scratch,
        compiler_params=_params(3),
        name=f"attn_d{DILATIONS[branch]}",
    )(q, k, v, k, v, bias)


def _merge_kernel(x_ref, *refs):
    n_br = len(DILATIONS)
    branch_refs = [refs[3 * i:3 * i + 3] for i in range(n_br)]
    e_ref, wo_ref, out_ref, y_scr, merged_buf = refs[3 * n_br:]
    planes, tj, _ = branch_refs[0][0].shape

    def plane_major(ref, cols=slice(None)):
        return jnp.concatenate([ref[p, :, cols] for p in range(planes)], axis=0)

    ms = [plane_major(br[1]) for br in branch_refs]
    ls = [plane_major(br[2]) for br in branch_refs]
    m_max = functools.reduce(jnp.maximum, ms)
    es = [jnp.exp2(m - m_max) for m in ms]
    total = functools.reduce(jnp.add, [e * l for e, l in zip(es, ls)])
    weights = []
    for e in es:
        c = e / total
        c_hi = c.astype(BF16)
        weights.append(jnp.concatenate([c_hi, (c - c_hi.astype(F32)).astype(BF16)], axis=-1))
    for chunk in range(D_MODEL // MERGE_CHUNK):
        cols = slice(chunk * MERGE_CHUNK, (chunk + 1) * MERGE_CHUNK)
        merged = None
        for w, br in zip(weights, branch_refs):
            c_wide = jnp.dot(w, e_ref[:, cols], preferred_element_type=F32)
            term = c_wide * plane_major(br[0], cols).astype(F32)
            merged = term if merged is None else merged + term
        merged_buf[:, cols] = merged.astype(BF16)
    y = jnp.dot(merged_buf[...], wo_ref[...], preferred_element_type=F32)
    for slab in range(D_MODEL // LANES):
        cols = slice(slab * LANES, (slab + 1) * LANES)
        for p in range(planes):
            y_scr[slab, pl.ds(p, tj, stride=planes), :] = y[p * tj:(p + 1) * tj, cols]
        out_ref[0, :, cols] = x_ref[0, :, cols] + y_scr[slab]


def _merge_out_proj(x, branches, expand, w_o):
    b, s, d = x.shape
    ts = ROW_TILE
    planes = branches[0][0].shape[1]
    row_spec = pl.BlockSpec((1, ts, d), lambda i, j: (i, j, 0))
    o_spec = pl.BlockSpec((None, planes, ts // planes, d), lambda i, j: (i, 0, j, 0))
    stat_spec = pl.BlockSpec((None, planes, ts // planes, LANES), lambda i, j: (i, 0, j, 0))
    args = [t for br in branches for t in br]
    return pl.pallas_call(
        _merge_kernel,
        grid=(b, s // ts),
        in_specs=[row_spec] + [o_spec, stat_spec, stat_spec] * len(branches)
                 + [_const_spec(expand.shape), _const_spec((d, d))],
        out_specs=row_spec,
        out_shape=jax.ShapeDtypeStruct(x.shape, F32),
        scratch_shapes=[pltpu.VMEM((d // LANES, ts, LANES), F32), pltpu.VMEM((ts, d), BF16)],
        compiler_params=_params(2),
        name="merge_out_proj",
    )(x, *args, expand, w_o)


def _t5_causal_bucket(dist):
    max_exact = N_REL_BUCKETS // 2
    dd = jnp.maximum(dist, 1).astype(F32)
    large = max_exact + (jnp.log(dd / max_exact) / math.log(REL_MAX_DISTANCE / max_exact)
                         * (N_REL_BUCKETS - max_exact)).astype(jnp.int32)
    large = jnp.minimum(large, N_REL_BUCKETS - 1)
    return jnp.where(dist < max_exact, dist, large)


def _bias_rows(rel_table, splits):
    n = ATT_BLOCK
    n_br = len(DILATIONS)
    n_heads = rel_table.shape[1]
    assert splits[0] in (1, 4) and all(s == 1 for s in splits[1:])
    dist = (n - jnp.arange(n + 1))[None, :] * jnp.asarray(DILATIONS)[:, None]
    by_offset = LOG2E * jnp.swapaxes(rel_table.astype(F32)[_t5_causal_bucket(dist)], 1, 2)

    def mask_fill(count):
        return jnp.full((n_heads, count), MASK_VALUE, F32)

    split = splits[0]
    rows = []
    for br in range(n_br):
        w = by_offset[br]
        if br > 0 or split == 1:
            rows.append(jnp.stack([jnp.concatenate([w, mask_fill(4 * n - (n + 1))], axis=1)] * 4, 1))
            continue
        groups = []
        for r in range(split):
            parts = []
            for rp in range(split):
                first = 0 if rp >= r else 1
                vals = w[:, split * first + rp - r::split]
                parts += [mask_fill(first), vals, mask_fill(n - first - vals.shape[1])]
            groups.append(jnp.concatenate(parts, axis=1))
        rows.append(jnp.stack(groups, axis=1))
    return jnp.stack(rows)[:, :, :, None, :]


def _write_bias_tables(rows_ref, out_ref, splits):
    n = ATT_BLOCK
    for br, sp in enumerate(splits):
        if sp == 1:
            wide = jnp.broadcast_to(rows_ref[br, 0, 0][:, :3 * n], (n, 3 * n))
            tab = pltpu.roll(wide, 0, 1, stride=1, stride_axis=0)[:, :2 * n]
        else:
            groups = []
            for r in range(sp):
                wide = jnp.broadcast_to(rows_ref[br, 0, r], (n // sp, 4 * n))
                rolled = pltpu.roll(wide, 0, 1, stride=1, stride_axis=0)
                groups.append(jnp.concatenate(
                    [rolled[:, rp * n:rp * n + 2 * n // sp] for rp in range(sp)], axis=1))
            tab = jnp.concatenate(groups, axis=0)
        out_ref[br, 0] = tab


def _head_expand_matrix():
    row = jnp.arange(2 * LANES)[:, None] % LANES
    head = jnp.arange(D_MODEL)[None, :] // HEAD_DIM
    stat_lane = jnp.where(head % 2 == 0, HEAD_DIM + head // 2, head // 2)
    return (row == stat_lane).astype(BF16)


def kernel(x, rel_bias, even_norm, even_w_in, even_conv_w, even_pool_w, even_pool_scale,
           even_w_out, odd_norm, odd_w_qkv, odd_q_norm, odd_k_norm, odd_w_o, ffn_norm,
           ffn_w_up, ffn_conv_w, ffn_conv_b, ffn_w_down):
    depth, d, d_ff2 = ffn_w_up.shape
    x, w_up, w_down, w_qkv, w_o = _mixer(
        x, even_norm[0][None], even_w_in[0].astype(BF16), even_conv_w[0],
        even_pool_w[0].astype(BF16), even_pool_scale[0][None], even_w_out[0].astype(BF16),
        to_cast=[ffn_w_up.reshape(depth * d, d_ff2), ffn_w_down.reshape(depth * D_FF, d),
                 odd_w_qkv[0], odd_w_o[0]])
    ffn_params = (ffn_norm[:, None], w_up.reshape(depth, d, d_ff2), ffn_conv_w,
                  ffn_conv_b[:, None], w_down.reshape(depth, D_FF, d))

    def ffn(t, layer):
        return _ffn(t, layer, *ffn_params)

    x = ffn(x, 0)

    gq_pair = jnp.tile(odd_q_norm[0], 2)[None]
    gk_pair = jnp.tile(odd_k_norm[0], 2)[None]
    splits = (4, 1, 1)
    qkv4, qkv16, bias = _qkv(x, odd_norm[0][None], w_qkv, gq_pair, gk_pair,
                             _bias_rows(rel_bias, splits), splits)
    branches = [
        _attn_branch(*qkv4, bias, branch=0, split=splits[0]),
        _attn_branch(*qkv4, bias, branch=1),
        _attn_branch(*qkv16, bias, branch=2, interleave=4),
    ]
    x = _merge_out_proj(x, branches, _head_expand_matrix(), w_o)
    x = ffn(x, 1)
    return x
```
